```python
import math
import jax, jax.numpy as jnp
from jax import lax
import numpy as np

D_MODEL = 2048
BATCH = 2
SEQ = 16384
DEPTH = 1

HEAD_DIM = 128
NSA_HEADS = 8
NSA_KV_GROUPS = 2
NSA_HEADS_PER_GROUP = NSA_HEADS // NSA_KV_GROUPS
NSA_WIDTH = NSA_HEADS * HEAD_DIM
NSA_KV = NSA_KV_GROUPS * HEAD_DIM
NSA_BRANCHES = 3
CMP_BLOCK = 32
CMP_STRIDE = 16
CMP_HIDDEN = 512
SEL_BLOCK = 64
N_SELECT = 16
WINDOW = 512
DIFF_HEADS = 4
DIFF_QK_DIM = 128
DIFF_V_DIM = 2 * DIFF_QK_DIM
DIFF_WIDTH = DIFF_HEADS * DIFF_V_DIM
MIX_WIDTH = NSA_WIDTH + DIFF_WIDTH
IN_SIZES = (NSA_WIDTH,
            NSA_KV, NSA_KV,
            NSA_KV, NSA_KV,
            NSA_KV, NSA_KV,
            NSA_HEADS * NSA_BRANCHES,
            NSA_WIDTH,
            2 * DIFF_HEADS * DIFF_QK_DIM,
            2 * DIFF_HEADS * DIFF_QK_DIM,
            DIFF_WIDTH,
            DIFF_WIDTH)
IN_WIDTH = sum(IN_SIZES)
Q_BLOCK = 128
ROPE_THETA = 10000.0
NORM_EPS = 1e-6
SUBLN_EPS = 1e-5
NEG_INF = -1e30
FORCE_SCORE = 1e9

kernel_name = 'hymba_nsa_diffattn_hybrid'


def rms_norm(x, w, eps):
    xf = x.astype(jnp.float32)
    y = xf * lax.rsqrt(jnp.mean(xf * xf, axis=-1, keepdims=True) + eps)
    return (y * w.astype(jnp.float32)).astype(x.dtype)


def rope_tables(t_len, dim):
    pos = jnp.arange(t_len, dtype=jnp.float32)
    inv = ROPE_THETA ** (-jnp.arange(0, dim, 2, dtype=jnp.float32) / dim)
    ang = pos[:, None] * inv[None, :]
    return jnp.cos(ang), jnp.sin(ang)


def apply_rope(x, cos, sin):
    t_len, half = cos.shape
    bshape = (1, t_len) + (1,) * (x.ndim - 3) + (half,)
    c, s = cos.reshape(bshape), sin.reshape(bshape)
    xf = x.astype(jnp.float32)
    x1, x2 = xf[..., :half], xf[..., half:]
    return jnp.concatenate([x1 * c - x2 * s, x2 * c + x1 * s], axis=-1).astype(x.dtype)


def masked_softmax(s, mask):
    return jax.nn.softmax(jnp.where(mask, s, NEG_INF), axis=-1)


def compress_blocks(kv, pos_emb, w1, w2):
    b, t_len, g, dh = kv.shape
    n_cmp = (t_len - CMP_BLOCK) // CMP_STRIDE + 1
    idx = np.arange(n_cmp)[:, None] * CMP_STRIDE + np.arange(CMP_BLOCK)[None, :]
    blocks = kv[:, idx] + pos_emb[None, None, :, None, :].astype(kv.dtype)
    blocks = jnp.transpose(blocks, (0, 1, 3, 2, 4)).reshape(b, n_cmp, g, CMP_BLOCK * dh)
    return jax.nn.silu(blocks @ w1) @ w2


def cmp_to_sel_map(t_len):
    n_cmp = (t_len - CMP_BLOCK) // CMP_STRIDE + 1
    n_sel = t_len // SEL_BLOCK
    c0 = np.arange(n_cmp)[:, None] * CMP_STRIDE
    s0 = np.arange(n_sel)[None, :] * SEL_BLOCK
    ov = np.minimum(c0 + CMP_BLOCK, s0 + SEL_BLOCK) - np.maximum(c0, s0)
    return (np.maximum(ov, 0) / CMP_BLOCK).astype(np.float32)


def nsa_mixer(q, k_c, v_c, k_s, v_s, k_w, v_w, gate_logits, cmp_pos,
              cmp_k_w1, cmp_k_w2, cmp_v_w1, cmp_v_w2, cos, sin):
    b, t_len = q.shape[:2]
    g, hg, dh = NSA_KV_GROUPS, NSA_HEADS_PER_GROUP, HEAD_DIM
    scale = dh ** -0.5
    q = apply_rope(q.reshape(b, t_len, NSA_HEADS, dh), cos, sin).reshape(b, t_len, g, hg, dh)
    k_cmp = compress_blocks(apply_rope(k_c.reshape(b, t_len, g, dh), cos, sin), cmp_pos, cmp_k_w1, cmp_k_w2)
    v_cmp = compress_blocks(v_c.reshape(b, t_len, g, dh), cmp_pos, cmp_v_w1, cmp_v_w2)
    n_cmp = k_cmp.shape[1]
    cmp_end = jnp.arange(n_cmp) * CMP_STRIDE + CMP_BLOCK - 1
    n_sel_blocks = t_len // SEL_BLOCK
    n_top = min(N_SELECT, n_sel_blocks)
    sel_map = jnp.asarray(cmp_to_sel_map(t_len))
    blk_ids = jnp.arange(n_sel_blocks)
    sel_off = jnp.arange(SEL_BLOCK)
    ks_t = jnp.transpose(apply_rope(k_s.reshape(b, t_len, g, dh), cos, sin), (0, 2, 1, 3))
    vs_t = jnp.transpose(v_s.reshape(b, t_len, g, dh), (0, 2, 1, 3))
    b_idx = jnp.arange(b)[:, None, None, None]
    g_idx = jnp.arange(g)[None, :, None, None]
    pad = ((0, 0), (WINDOW, 0), (0, 0), (0, 0))
    kw_pad = jnp.pad(apply_rope(k_w.reshape(b, t_len, g, dh), cos, sin), pad)
    vw_pad = jnp.pad(v_w.reshape(b, t_len, g, dh), pad)
    gates = jax.nn.sigmoid(gate_logits.astype(jnp.float32)).reshape(b, t_len, g, hg, NSA_BRANCHES)

    def query_block(c):
        start = c * Q_BLOCK
        t = start + jnp.arange(Q_BLOCK)
        qc = lax.dynamic_slice_in_dim(q, start, Q_BLOCK, axis=1)
        s = jnp.einsum('bqghd,bngd->bghqn', qc, k_cmp).astype(jnp.float32) * scale
        m = cmp_end[None, :] <= t[:, None]
        p_cmp = masked_softmax(s, m) * m
        o_cmp = jnp.einsum('bghqn,bngd->bqghd', p_cmp.astype(v_cmp.dtype), v_cmp)
        imp = jnp.einsum('bghqn,nj->bgqj', p_cmp, sel_map)
        cur = t // SEL_BLOCK
        forced = (blk_ids[None, :] == 0) | (blk_ids[None, :] == cur[:, None]) | (blk_ids[None, :] == cur[:, None] - 1)
        imp = jnp.where(forced, FORCE_SCORE, imp)
        imp = jnp.where(blk_ids[None, :] > cur[:, None], NEG_INF, imp)
        _, top = lax.top_k(imp, n_top)
        tok = (top[..., None] * SEL_BLOCK + sel_off).reshape(b, g, Q_BLOCK, n_top * SEL_BLOCK)
        kg = ks_t[b_idx, g_idx, tok]
        vg = vs_t[b_idx, g_idx, tok]
        s = jnp.einsum('bqghd,bgqkd->bghqk', qc, kg).astype(jnp.float32) * scale
        m = (tok <= t[None, None, :, None])[:, :, None]
        p = masked_softmax(s, m)
        o_sel = jnp.einsum('bghqk,bgqkd->bqghd', p.astype(vg.dtype), vg)
        kwc = lax.dynamic_slice_in_dim(kw_pad, start, WINDOW + Q_BLOCK, axis=1)
        vwc = lax.dynamic_slice_in_dim(vw_pad, start, WINDOW + Q_BLOCK, axis=1)
        spos = start - WINDOW + jnp.arange(WINDOW + Q_BLOCK)
        m = (spos[None, :] <= t[:, None]) & (spos[None, :] > t[:, None] - WINDOW) & (spos[None, :] >= 0)
        s = jnp.einsum('bqghd,bkgd->bghqk', qc, kwc).astype(jnp.float32) * scale
        p = masked_softmax(s, m)
        o_win = jnp.einsum('bghqk,bkgd->bqghd', p.astype(vwc.dtype), vwc)
        gc = lax.dynamic_slice_in_dim(gates, start, Q_BLOCK, axis=1)
        o = gc[..., 0:1] * o_cmp + gc[..., 1:2] * o_sel + gc[..., 2:3] * o_win
        return o.astype(q.dtype)

    out = lax.map(query_block, jnp.arange(t_len // Q_BLOCK))
    return jnp.transpose(out, (1, 0, 2, 3, 4, 5)).reshape(b, t_len, NSA_WIDTH)


def diff_mixer(q, k, v, lam_q1, lam_k1, lam_q2, lam_k2, subln_w, lambda_init, cos, sin):
    b, t_len = q.shape[:2]
    scale = DIFF_QK_DIM ** -0.5
    q = apply_rope(q.reshape(b, t_len, 2, DIFF_HEADS, DIFF_QK_DIM), cos, sin)
    k = apply_rope(k.reshape(b, t_len, 2, DIFF_HEADS, DIFF_QK_DIM), cos, sin)
    v = v.reshape(b, t_len, DIFF_HEADS, DIFF_V_DIM)
    f32 = jnp.float32
    lam = (jnp.exp(jnp.sum(lam_q1.astype(f32) * lam_k1.astype(f32)))
           - jnp.exp(jnp.sum(lam_q2.astype(f32) * lam_k2.astype(f32))) + lambda_init)
    kpos = jnp.arange(t_len)

    def query_block(c):
        start = c * Q_BLOCK
        t = start + jnp.arange(Q_BLOCK)
        qc = lax.dynamic_slice_in_dim(q, start, Q_BLOCK, axis=1)
        s = jnp.einsum('bqshd,bkshd->bshqk', qc, k).astype(f32) * scale
        p = masked_softmax(s, kpos[None, :] <= t[:, None])
        a = p[:, 0] - lam * p[:, 1]
        return jnp.einsum('bhqk,bkhd->bqhd', a.astype(v.dtype), v)

    out = lax.map(query_block, jnp.arange(t_len // Q_BLOCK))
    out = jnp.transpose(out, (1, 0, 2, 3, 4)).reshape(b, t_len, DIFF_HEADS, DIFF_V_DIM)
    out = rms_norm(out, subln_w, SUBLN_EPS) * (1.0 - lambda_init)
    return out.reshape(b, t_len, DIFF_WIDTH)


def setup_inputs(seed: int = 0) -> dict:
    key = jax.random.key(seed)
    ks = jax.random.split(key, 16)
    f32 = jnp.float32
    nrm = lambda k, shape, s: jax.random.normal(k, shape, f32) * s
    return {
        'x': nrm(ks[0], (BATCH, SEQ, D_MODEL), 1.0),
        'norm_w': 1.0 + nrm(ks[1], (DEPTH, D_MODEL), 0.01),
        'w_in': nrm(ks[2], (DEPTH, D_MODEL, IN_WIDTH), D_MODEL ** -0.5),
        'cmp_pos': nrm(ks[3], (DEPTH, CMP_BLOCK, HEAD_DIM), 0.1),
        'cmp_k_w1': nrm(ks[4], (DEPTH, CMP_BLOCK * HEAD_DIM, CMP_HIDDEN), (CMP_BLOCK * HEAD_DIM) ** -0.5),
        'cmp_k_w2': nrm(ks[5], (DEPTH, CMP_HIDDEN, HEAD_DIM), CMP_HIDDEN ** -0.5),
        'cmp_v_w1': nrm(ks[6], (DEPTH, CMP_BLOCK * HEAD_DIM, CMP_HIDDEN), (CMP_BLOCK * HEAD_DIM) ** -0.5),
        'cmp_v_w2': nrm(ks[7], (DEPTH, CMP_HIDDEN, HEAD_DIM), CMP_HIDDEN ** -0.5),
        'lam_q1': nrm(ks[8], (DEPTH, DIFF_QK_DIM), 0.1),
        'lam_k1': nrm(ks[9], (DEPTH, DIFF_QK_DIM), 0.1),
        'lam_q2': nrm(ks[10], (DEPTH, DIFF_QK_DIM), 0.1),
        'lam_k2': nrm(ks[11], (DEPTH, DIFF_QK_DIM), 0.1),
        'subln_w': 1.0 + nrm(ks[12], (DEPTH, DIFF_V_DIM), 0.01),
        'w_out': nrm(ks[13], (DEPTH, MIX_WIDTH, D_MODEL), MIX_WIDTH ** -0.5),
        'final_norm_w': 1.0 + nrm(ks[14], (D_MODEL,), 0.01),
    }


def reference(x, norm_w, w_in, cmp_pos, cmp_k_w1, cmp_k_w2, cmp_v_w1, cmp_v_w2,
              lam_q1, lam_k1, lam_q2, lam_k2, subln_w, w_out, final_norm_w):
    t_len = x.shape[1]
    cos, sin = rope_tables(t_len, HEAD_DIM)
    split_points = np.cumsum(IN_SIZES)[:-1].tolist()
    for l in range(DEPTH):
        lambda_init = 0.8 - 0.6 * math.exp(-0.3 * l)
        h = rms_norm(x, norm_w[l], NORM_EPS)
        proj = h @ w_in[l]
        (q_a, kc_a, vc_a, ks_a, vs_a, kw_a, vw_a, g_a, z_a,
         q_b, k_b, v_b, z_b) = jnp.split(proj, split_points, axis=-1)
        o_a = nsa_mixer(q_a, kc_a, vc_a, ks_a, vs_a, kw_a, vw_a, g_a, cmp_pos[l],
                        cmp_k_w1[l], cmp_k_w2[l], cmp_v_w1[l], cmp_v_w2[l], cos, sin)
        o_b = diff_mixer(q_b, k_b, v_b, lam_q1[l], lam_k1[l], lam_q2[l], lam_k2[l],
                         subln_w[l], lambda_init, cos, sin)
        mixed = jnp.concatenate([o_a * jax.nn.silu(z_a), o_b * jax.nn.silu(z_b)], axis=-1)
        x = x + mixed @ w_out[l]
    return rms_norm(x, final_norm_w, NORM_EPS)
```

```python
import functools
import math

import numpy as np
import jax
import jax.numpy as jnp
from jax import lax
from jax.experimental import pallas as pl
from jax.experimental.pallas import tpu as pltpu

F32 = jnp.float32
BF16 = jnp.bfloat16

D_MODEL = 2048
HEAD_DIM = 128
NSA_HEADS = 8
NSA_GROUPS = 2
NSA_HPG = NSA_HEADS // NSA_GROUPS
NSA_WIDTH = NSA_HEADS * HEAD_DIM
NSA_KV = NSA_GROUPS * HEAD_DIM
NSA_BRANCHES = 3
CMP_BLOCK = 32
CMP_STRIDE = 16
CMP_HIDDEN = 512
SEL_BLOCK = 64
N_SELECT = 16
WINDOW = 512
DIFF_HEADS = 4
DIFF_QK = 128
DIFF_V = 256
DIFF_WIDTH = DIFF_HEADS * DIFF_V
MIX_WIDTH = NSA_WIDTH + DIFF_WIDTH
Q_BLOCK = 128
ROPE_THETA = 10000.0
NORM_EPS = 1e-6
SUBLN_EPS = 1e-5
NEG_INF = -1e30
FORCE_SCORE = 1e9
LAMBDA_INIT = 0.8 - 0.6 * math.exp(-0.3 * 0)

LANE = 128
VMEM_LIMIT = 56 * 1024 * 1024

SLOT_QA = 0
SLOT_QB = 8
SLOT_KB = 16
SLOT_KC = 24
SLOT_KS = 26
SLOT_KW = 28
N_ROPE_SLOTS = 30
SLOT_VC = 30
SLOT_ZA = 32
SLOT_ZB = 40
SLOT_VB = 48
SLOT_VS = 56
SLOT_VW = 58
N_SLOTS = 60
SLOTS_PER_STEP = 6

_OFF = np.concatenate([[0], np.cumsum([NSA_WIDTH, NSA_KV, NSA_KV, NSA_KV, NSA_KV, NSA_KV, NSA_KV,
                                       NSA_HEADS * NSA_BRANCHES, NSA_WIDTH, 1024, 1024, DIFF_WIDTH, DIFF_WIDTH])])
(_O_QA, _O_KC, _O_VC, _O_KS, _O_VS, _O_KW, _O_VW, _O_G, _O_ZA, _O_QB, _O_KB, _O_VB, _O_ZB, _O_END) = [int(v) for v in _OFF]


def _cparams(sem):
    return pltpu.CompilerParams(dimension_semantics=sem, vmem_limit_bytes=VMEM_LIMIT)


def _in_proj_kernel(x_ref, nw_ref, w_ref, wg_ref, cos_ref, sin_ref, p_ref, g_ref, h_ref, *, n_rope_steps):
    j = pl.program_id(1)

    @pl.when(j == 0)
    def _():
        x = x_ref[...]
        y = x * lax.rsqrt(jnp.mean(x * x, axis=-1, keepdims=True) + NORM_EPS)
        h = (y * nw_ref[...]).astype(BF16)
        h_ref[...] = h
        g_ref[0] = jnp.dot(h, wg_ref[...], preferred_element_type=F32)

    acc = jnp.dot(h_ref[...], w_ref[...], preferred_element_type=F32)

    @pl.when(j < n_rope_steps)
    def _():
        cos = cos_ref[...]
        sin = sin_ref[...]
        for s in range(SLOTS_PER_STEP):
            a = acc[:, s * LANE:(s + 1) * LANE]
            r = a * cos + pltpu.roll(a, HEAD_DIM // 2, 1) * sin
            p_ref[0, s] = r.astype(BF16)

    @pl.when(j >= n_rope_steps)
    def _():
        for s in range(SLOTS_PER_STEP):
            p_ref[0, s] = acc[:, s * LANE:(s + 1) * LANE].astype(BF16)


def _in_proj(x2, norm_w, w_r, w_g, cos_f, sin_f, batch, t_len, tm):
    m = x2.shape[0]
    tiles_per_seq = t_len // tm
    n_steps = N_SLOTS // SLOTS_PER_STEP
    tn = SLOTS_PER_STEP * LANE
    kern = functools.partial(_in_proj_kernel, n_rope_steps=N_ROPE_SLOTS // SLOTS_PER_STEP)
    return pl.pallas_call(
        kern,
        grid=(m // tm, n_steps),
        in_specs=[
            pl.BlockSpec((tm, D_MODEL), lambda i, j: (i, 0)),
            pl.BlockSpec((1, D_MODEL), lambda i, j: (0, 0)),
            pl.BlockSpec((D_MODEL, tn), lambda i, j: (0, j)),
            pl.BlockSpec((D_MODEL, LANE), lambda i, j: (0, 0)),
            pl.BlockSpec((tm, LANE), lambda i, j: (i % tiles_per_seq, 0)),
            pl.BlockSpec((tm, LANE), lambda i, j: (i % tiles_per_seq, 0)),
        ],
        out_specs=[
            pl.BlockSpec((1, SLOTS_PER_STEP, tm, LANE), lambda i, j: (i // tiles_per_seq, j, i % tiles_per_seq, 0)),
            pl.BlockSpec((1, tm, LANE), lambda i, j: (i // tiles_per_seq, i % tiles_per_seq, 0)),
        ],
        out_shape=[
            jax.ShapeDtypeStruct((batch, N_SLOTS, t_len, LANE), BF16),
            jax.ShapeDtypeStruct((batch, t_len, LANE), F32),
        ],
        scratch_shapes=[pltpu.VMEM((tm, D_MODEL), BF16)],
        compiler_params=_cparams(("parallel", "arbitrary")),
        name="in_proj",
    )(x2, norm_w, w_r, w_g, cos_f, sin_f)


def _compress_kernel(r_ref, w1_ref, pos_ref, w2_ref, o_ref, *, n_rows):
    w1 = w1_ref[0]
    a = jnp.dot(r_ref[0, 0], w1, preferred_element_type=F32)
    c = (jnp.dot(pos_ref[0], w1[:, :CMP_HIDDEN], preferred_element_type=F32)
         + jnp.dot(pos_ref[1], w1[:, CMP_HIDDEN:], preferred_element_type=F32))[0:1]
    nxt = pltpu.roll(a[:, CMP_HIDDEN:], n_rows - 1, 0)
    hid = a[:, :CMP_HIDDEN] + nxt + c
    hid = hid * jax.nn.sigmoid(hid)
    out = jnp.dot(hid.astype(BF16), w2_ref[0], preferred_element_type=F32)
    row = lax.broadcasted_iota(jnp.int32, out.shape, 0)
    o_ref[0, 0, 0] = jnp.where(row < n_rows - 1, out, 0.0).astype(BF16)


def _compress(p16, w1cat, pos2, w2, batch, n_rows):
    kern = functools.partial(_compress_kernel, n_rows=n_rows)

    def slot(b, kv, g):
        return (b, jnp.where(kv == 0, SLOT_KC, SLOT_VC) + g, 0, 0)

    return pl.pallas_call(
        kern,
        grid=(batch, 2, NSA_GROUPS),
        in_specs=[
            pl.BlockSpec((1, 1, n_rows, CMP_STRIDE * HEAD_DIM), slot),
            pl.BlockSpec((1, CMP_STRIDE * HEAD_DIM, 2 * CMP_HIDDEN), lambda b, kv, g: (kv, 0, 0)),
            pl.BlockSpec((2, 8, CMP_STRIDE * HEAD_DIM), lambda b, kv, g: (0, 0, 0)),
            pl.BlockSpec((1, CMP_HIDDEN, HEAD_DIM), lambda b, kv, g: (kv, 0, 0)),
        ],
        out_specs=pl.BlockSpec((1, 1, 1, n_rows, HEAD_DIM), lambda b, kv, g: (b, kv, g, 0, 0)),
        out_shape=jax.ShapeDtypeStruct((batch, 2, NSA_GROUPS, n_rows, HEAD_DIM), BF16),
        compiler_params=_cparams(("parallel", "parallel", "parallel")),
        name="compress",
    )(p16, w1cat, pos2, w2)


def _cmp_select_kernel(q_ref, kc_ref, vc_ref, map_ref, o_ref, bias_ref, *, n_rows, n_top, scale):
    c = pl.program_id(2)
    start = c * Q_BLOCK
    q = q_ref[0].reshape(NSA_HPG * Q_BLOCK, HEAD_DIM)
    s = lax.dot_general(q, kc_ref[0, 0, 0], (((1,), (1,)), ((), ())), preferred_element_type=F32)
    s = s.reshape(NSA_HPG, Q_BLOCK, n_rows) * scale
    t_q = start + lax.broadcasted_iota(jnp.int32, (1, Q_BLOCK, n_rows), 1)
    n_id = lax.broadcasted_iota(jnp.int32, (1, Q_BLOCK, n_rows), 2)
    valid = (n_id * CMP_STRIDE + (CMP_BLOCK - 1) <= t_q) & (n_id < n_rows - 1)
    s = jnp.where(valid, s, NEG_INF)
    mx = jnp.max(s, axis=-1, keepdims=True)
    e = jnp.where(valid, jnp.exp(s - mx), 0.0)
    den = jnp.sum(e, axis=-1, keepdims=True)
    p = e * (1.0 / jnp.maximum(den, 1e-30))
    o = jnp.dot(p.reshape(NSA_HPG * Q_BLOCK, n_rows).astype(BF16), vc_ref[0, 0, 0],
                preferred_element_type=F32)
    for h in range(NSA_HPG):
        o_ref[0, :, h * HEAD_DIM:(h + 1) * HEAD_DIM] = o[h * Q_BLOCK:(h + 1) * Q_BLOCK]

    psum = p[0] + p[1] + p[2] + p[3]
    p_hi = psum.astype(BF16)
    p_lo = (psum - p_hi.astype(F32)).astype(BF16)
    smap = map_ref[...]
    imp = jnp.dot(p_hi, smap, preferred_element_type=F32) + jnp.dot(p_lo, smap, preferred_element_type=F32)
    n_selp = imp.shape[-1]
    blk = lax.broadcasted_iota(jnp.int32, (Q_BLOCK, n_selp), 1)
    cur = (start + lax.broadcasted_iota(jnp.int32, (Q_BLOCK, n_selp), 0)) // SEL_BLOCK
    forced = (blk == 0) | (blk == cur) | (blk == cur - 1)
    imp = jnp.where(forced, FORCE_SCORE, imp)
    imp = jnp.where(blk > cur, NEG_INF, imp)
    blk_f = blk.astype(F32)
    picked = jnp.zeros((Q_BLOCK, n_selp), F32)
    for _ in range(n_top):
        top = jnp.max(imp, axis=-1, keepdims=True)
        first = jnp.min(jnp.where(imp == top, blk_f, 1e9), axis=-1, keepdims=True)
        hit = blk_f == first
        picked = jnp.where(hit, 1.0, picked)
        imp = jnp.where(hit, -jnp.inf, imp)
    keep = (picked > 0.0) & (blk <= cur)
    bias_ref[0, 0] = jnp.where(keep, 0.0, NEG_INF).astype(BF16)


def _cmp_select(p, kvc, sel_map, batch, t_len):
    n_rows = t_len // CMP_STRIDE
    n_selp = sel_map.shape[1]
    n_top = min(N_SELECT, t_len // SEL_BLOCK)
    kern = functools.partial(_cmp_select_kernel, n_rows=n_rows, n_top=n_top, scale=HEAD_DIM ** -0.5)
    return pl.pallas_call(
        kern,
        grid=(batch, NSA_GROUPS, t_len // Q_BLOCK),
        in_specs=[
            pl.BlockSpec((1, NSA_HPG, Q_BLOCK, HEAD_DIM), lambda b, g, c: (b, g, c, 0)),
            pl.BlockSpec((1, 1, 1, n_rows, HEAD_DIM), lambda b, g, c: (b, 0, g, 0, 0)),
            pl.BlockSpec((1, 1, 1, n_rows, HEAD_DIM), lambda b, g, c: (b, 1, g, 0, 0)),
            pl.BlockSpec((n_rows, n_selp), lambda b, g, c: (0, 0)),
        ],
        out_specs=[
            pl.BlockSpec((1, Q_BLOCK, NSA_HPG * HEAD_DIM), lambda b, g, c: (b, c, g)),
            pl.BlockSpec((1, 1, Q_BLOCK, n_selp), lambda b, g, c: (b, g, c, 0)),
        ],
        out_shape=[
            jax.ShapeDtypeStruct((batch, t_len, NSA_WIDTH), F32),
            jax.ShapeDtypeStruct((batch, NSA_GROUPS, t_len, n_selp), BF16),
        ],
        compiler_params=_cparams(("parallel", "parallel", "parallel")),
        name="cmp_select",
    )(p, kvc, kvc, sel_map)


def _sel_attn_kernel(q_ref, bias_ref, k_ref, v_ref, hot_ref, o_ref, qa_ref, m_ref, l_ref, acc_ref,
                     *, tk, n_halves, tiles_per_half, c2):
    c = pl.program_id(2)
    start = c * Q_BLOCK
    rows = NSA_HPG * Q_BLOCK
    q = q_ref[0].reshape(rows, HEAD_DIM)
    bias = bias_ref[0, 0]
    for hf in range(n_halves):
        b1 = bias[:, hf * LANE:(hf + 1) * LANE]
        qa_ref[hf] = jnp.concatenate([q, jnp.concatenate([b1] * NSA_HPG, axis=0)], axis=1)
    m_ref[...] = jnp.full(m_ref.shape, NEG_INF, F32)
    l_ref[...] = jnp.zeros(l_ref.shape, F32)
    acc_ref[...] = jnp.zeros(acc_ref.shape, F32)

    def tile(j, masked):
        k0 = pl.multiple_of(j * tk, tk)
        k_aug = jnp.concatenate([k_ref[0, 0, pl.ds(k0, tk), :], hot_ref[pl.ds(k0, tk), :]], axis=1)
        qa = qa_ref[j // tiles_per_half]
        s = lax.dot_general(qa, k_aug, (((1,), (1,)), ((), ())), preferred_element_type=F32)
        if masked:
            s3 = s.reshape(NSA_HPG, Q_BLOCK, tk)
            t_q = start + lax.broadcasted_iota(jnp.int32, (1, Q_BLOCK, tk), 1)
            k_pos = k0 + lax.broadcasted_iota(jnp.int32, (1, Q_BLOCK, tk), 2)
            s = jnp.where(k_pos <= t_q, s3, NEG_INF).reshape(rows, tk)
        m_prev = m_ref[...]
        m_new = jnp.maximum(m_prev, jnp.max(s, axis=-1, keepdims=True))
        alpha = jnp.exp2((m_prev - m_new) * c2)
        p = jnp.exp2(s * c2 - m_new * c2)
        l_ref[...] = alpha * l_ref[...] + jnp.sum(p, axis=-1, keepdims=True)
        acc_ref[...] = alpha * acc_ref[...] + jnp.dot(p.astype(BF16), v_ref[0, 0, pl.ds(k0, tk), :],
                                                      preferred_element_type=F32)
        m_ref[...] = m_new

    n_full = start // tk

    def body(j, carry):
        tile(j, False)
        return carry

    lax.fori_loop(0, n_full, body, 0)
    tile(n_full, True)

    o = acc_ref[...] / l_ref[...]
    for h in range(NSA_HPG):
        o_ref[0, :, h * HEAD_DIM:(h + 1) * HEAD_DIM] = o[h * Q_BLOCK:(h + 1) * Q_BLOCK]


def _sel_attn(p, bias, hot, batch, t_len, tk):
    n_selp = bias.shape[-1]
    n_halves = n_selp // LANE
    tiles_per_half = max((LANE * SEL_BLOCK) // tk, 1)
    rows = NSA_HPG * Q_BLOCK
    c2 = (HEAD_DIM ** -0.5) * math.log2(math.e)
    kern = functools.partial(_sel_attn_kernel, tk=tk, n_halves=n_halves, tiles_per_half=tiles_per_half, c2=c2)
    return pl.pallas_call(
        kern,
        grid=(batch, NSA_GROUPS, t_len // Q_BLOCK),
        in_specs=[
            pl.BlockSpec((1, NSA_HPG, Q_BLOCK, HEAD_DIM), lambda b, g, c: (b, g, c, 0)),
            pl.BlockSpec((1, 1, Q_BLOCK, n_selp), lambda b, g, c: (b, g, c, 0)),
            pl.BlockSpec((1, 1, t_len, HEAD_DIM), lambda b, g, c: (b, SLOT_KS + g, 0, 0)),
            pl.BlockSpec((1, 1, t_len, HEAD_DIM), lambda b, g, c: (b, SLOT_VS + g, 0, 0)),
            pl.BlockSpec((t_len, LANE), lambda b, g, c: (0, 0)),
        ],
        out_specs=pl.BlockSpec((1, Q_BLOCK, NSA_HPG * HEAD_DIM), lambda b, g, c: (b, c, g)),
        out_shape=jax.ShapeDtypeStruct((batch, t_len, NSA_WIDTH), F32),
        scratch_shapes=[
            pltpu.VMEM((n_halves, rows, 2 * LANE), BF16),
            pltpu.VMEM((rows, 1), F32),
            pltpu.VMEM((rows, 1), F32),
            pltpu.VMEM((rows, HEAD_DIM), F32),
        ],
        compiler_params=_cparams(("parallel", "parallel", "arbitrary")),
        name="sel_attn",
    )(p, bias, p, p, hot)


def _win_attn_kernel(q_ref, k_ref, v_ref, o_ref, *, span, scale):
    c = pl.program_id(2)
    start = c * Q_BLOCK
    rows = NSA_HPG * Q_BLOCK
    k0 = pl.multiple_of(jnp.maximum(start + Q_BLOCK - span, 0), Q_BLOCK)
    q = q_ref[0].reshape(rows, HEAD_DIM)
    s = lax.dot_general(q, k_ref[0, 0, pl.ds(k0, span), :], (((1,), (1,)), ((), ())),
                        preferred_element_type=F32)
    s = s.reshape(NSA_HPG, Q_BLOCK, span) * scale
    t_q = start + lax.broadcasted_iota(jnp.int32, (1, Q_BLOCK, span), 1)
    k_pos = k0 + lax.broadcasted_iota(jnp.int32, (1, Q_BLOCK, span), 2)
    valid = (k_pos <= t_q) & (k_pos > t_q - WINDOW)
    s = jnp.where(valid, s, NEG_INF)
    mx = jnp.max(s, axis=-1, keepdims=True)
    e = jnp.exp(s - mx)
    den = jnp.sum(e, axis=-1, keepdims=True)
    o = jnp.dot(e.reshape(rows, span).astype(BF16), v_ref[0, 0, pl.ds(k0, span), :],
                preferred_element_type=F32)
    o = o / den.reshape(rows, 1)
    for h in range(NSA_HPG):
        o_ref[0, :, h * HEAD_DIM:(h + 1) * HEAD_DIM] = o[h * Q_BLOCK:(h + 1) * Q_BLOCK]


def _win_attn(p, batch, t_len):
    span = min(WINDOW + Q_BLOCK, t_len)
    kern = functools.partial(_win_attn_kernel, span=span, scale=HEAD_DIM ** -0.5)
    return pl.pallas_call(
        kern,
        grid=(batch, NSA_GROUPS, t_len // Q_BLOCK),
        in_specs=[
            pl.BlockSpec((1, NSA_HPG, Q_BLOCK, HEAD_DIM), lambda b, g, c: (b, g, c, 0)),
            pl.BlockSpec((1, 1, t_len, HEAD_DIM), lambda b, g, c: (b, SLOT_KW + g, 0, 0)),
            pl.BlockSpec((1, 1, t_len, HEAD_DIM), lambda b, g, c: (b, SLOT_VW + g, 0, 0)),
        ],
        out_specs=pl.BlockSpec((1, Q_BLOCK, NSA_HPG * HEAD_DIM), lambda b, g, c: (b, c, g)),
        out_shape=jax.ShapeDtypeStruct((batch, t_len, NSA_WIDTH), F32),
        compiler_params=_cparams(("parallel", "parallel", "parallel")),
        name="win_attn",
    )(p, p, p)


def _diff_attn_kernel(q1_ref, q2_ref, k1_ref, k2_ref, v_ref, lam_ref, sw_ref, o_ref,
                      m_ref, l_ref, acc_ref, *, tq, tk, c2):
    qi = pl.program_id(2)
    start = qi * tq
    m_ref[...] = jnp.full(m_ref.shape, NEG_INF, F32)
    l_ref[...] = jnp.zeros(l_ref.shape, F32)
    acc_ref[...] = jnp.zeros(acc_ref.shape, F32)
    q_refs = (q1_ref, q2_ref)
    k_refs = (k1_ref, k2_ref)

    def tile(j, masked):
        k0 = pl.multiple_of(j * tk, tk)
        v = jnp.concatenate([v_ref[0, 0, pl.ds(k0, tk), :], v_ref[0, 1, pl.ds(k0, tk), :]], axis=1)
        for mp in range(2):
            s = lax.dot_general(q_refs[mp][0, 0], k_refs[mp][0, 0, pl.ds(k0, tk), :],
                                (((1,), (1,)), ((), ())), preferred_element_type=F32)
            if masked:
                t_q = start + lax.broadcasted_iota(jnp.int32, (tq, tk), 0)
                k_pos = k0 + lax.broadcasted_iota(jnp.int32, (tq, tk), 1)
                s = jnp.where(k_pos <= t_q, s, NEG_INF)
            m_prev = m_ref[mp]
            m_new = jnp.maximum(m_prev, jnp.max(s, axis=-1, keepdims=True))
            alpha = jnp.exp2((m_prev - m_new) * c2)
            p = jnp.exp2(s * c2 - m_new * c2)
            l_ref[mp] = alpha * l_ref[mp] + jnp.sum(p, axis=-1, keepdims=True)
            acc_ref[mp] = alpha * acc_ref[mp] + jnp.dot(p.astype(BF16), v, preferred_element_type=F32)
            m_ref[mp] = m_new

    n_full = start // tk

    def body(j, carry):
        tile(j, False)
        return carry

    lax.fori_loop(0, n_full, body, 0)
    for d in range(tq // tk):
        tile(n_full + d, True)

    lv = lam_ref[...]
    lam = (jnp.exp(jnp.sum(lv[0:1] * lv[1:2], axis=-1, keepdims=True))
           - jnp.exp(jnp.sum(lv[2:3] * lv[3:4], axis=-1, keepdims=True)) + LAMBDA_INIT)
    o = acc_ref[0] / l_ref[0] - lam * (acc_ref[1] / l_ref[1])
    y = o * lax.rsqrt(jnp.mean(o * o, axis=-1, keepdims=True) + SUBLN_EPS)
    o_ref[0] = y * sw_ref[...] * (1.0 - LAMBDA_INIT)


def _diff_attn(p, lam4, subln_w, batch, t_len, tq, tk):
    c2 = (DIFF_QK ** -0.5) * math.log2(math.e)
    kern = functools.partial(_diff_attn_kernel, tq=tq, tk=tk, c2=c2)
    return pl.pallas_call(
        kern,
        grid=(batch, DIFF_HEADS, t_len // tq),
        in_specs=[
            pl.BlockSpec((1, 1, tq, DIFF_QK), lambda b, h, i: (b, SLOT_QB + h, i, 0)),
            pl.BlockSpec((1, 1, tq, DIFF_QK), lambda b, h, i: (b, SLOT_QB + DIFF_HEADS + h, i, 0)),
            pl.BlockSpec((1, 1, t_len, DIFF_QK), lambda b, h, i: (b, SLOT_KB + h, 0, 0)),
            pl.BlockSpec((1, 1, t_len, DIFF_QK), lambda b, h, i: (b, SLOT_KB + DIFF_HEADS + h, 0, 0)),
            pl.BlockSpec((1, 2, t_len, LANE), lambda b, h, i: (b, SLOT_VB // 2 + h, 0, 0)),
            pl.BlockSpec((8, DIFF_QK), lambda b, h, i: (0, 0)),
            pl.BlockSpec((1, DIFF_V), lambda b, h, i: (0, 0)),
        ],
        out_specs=pl.BlockSpec((1, tq, DIFF_V), lambda b, h, i: (b, i, h)),
        out_shape=jax.ShapeDtypeStruct((batch, t_len, DIFF_WIDTH), F32),
        scratch_shapes=[
            pltpu.VMEM((2, tq, 1), F32),
            pltpu.VMEM((2, tq, 1), F32),
            pltpu.VMEM((2, tq, DIFF_V), F32),
        ],
        compiler_params=_cparams(("parallel", "parallel", "arbitrary")),
        name="diff_attn",
    )(p, p, p, p, p, lam4, subln_w)


def _out_proj_kernel(x_ref, oc_ref, os_ref, ow_ref, g_ref, za_ref, ob_ref, zb_ref, w_ref, fw_ref,
                     o_ref, mix_ref):
    gates = jax.nn.sigmoid(g_ref[0])
    tm = gates.shape[0]
    for hd in range(NSA_HEADS):
        cols = slice(hd * HEAD_DIM, (hd + 1) * HEAD_DIM)
        g0 = jnp.broadcast_to(gates[:, 3 * hd:3 * hd + 1], (tm, HEAD_DIM))
        g1 = jnp.broadcast_to(gates[:, 3 * hd + 1:3 * hd + 2], (tm, HEAD_DIM))
        g2 = jnp.broadcast_to(gates[:, 3 * hd + 2:3 * hd + 3], (tm, HEAD_DIM))
        o = g0 * oc_ref[0, :, cols] + g1 * os_ref[0, :, cols] + g2 * ow_ref[0, :, cols]
        z = za_ref[0, hd].astype(F32)
        mix_ref[:, cols] = (o * (z * jax.nn.sigmoid(z))).astype(BF16)
    for sl in range(DIFF_WIDTH // LANE):
        cols = slice(sl * LANE, (sl + 1) * LANE)
        z = zb_ref[0, sl].astype(F32)
        mix_ref[:, NSA_WIDTH + sl * LANE:NSA_WIDTH + (sl + 1) * LANE] = (
            ob_ref[0, :, cols] * (z * jax.nn.sigmoid(z))).astype(BF16)
    y = x_ref[0] + jnp.dot(mix_ref[...], w_ref[...], preferred_element_type=F32)
    y = y * lax.rsqrt(jnp.mean(y * y, axis=-1, keepdims=True) + NORM_EPS)
    o_ref[0] = y * fw_ref[...]


def _out_proj(x, o_cmp, o_sel, o_win, gl, p, o_b, w_out, final_w, batch, t_len, tm):
    row = lambda b, i: (b, i, 0)
    return pl.pallas_call(
        _out_proj_kernel,
        grid=(batch, t_len // tm),
        in_specs=[
            pl.BlockSpec((1, tm, D_MODEL), row),
            pl.BlockSpec((1, tm, NSA_WIDTH), row),
            pl.BlockSpec((1, tm, NSA_WIDTH), row),
            pl.BlockSpec((1, tm, NSA_WIDTH), row),
            pl.BlockSpec((1, tm, LANE), row),
            pl.BlockSpec((1, 8, tm, LANE), lambda b, i: (b, SLOT_ZA // 8, i, 0)),
            pl.BlockSpec((1, tm, DIFF_WIDTH), row),
            pl.BlockSpec((1, 8, tm, LANE), lambda b, i: (b, SLOT_ZB // 8, i, 0)),
            pl.BlockSpec((MIX_WIDTH, D_MODEL), lambda b, i: (0, 0)),
            pl.BlockSpec((1, D_MODEL), lambda b, i: (0, 0)),
        ],
        out_specs=pl.BlockSpec((1, tm, D_MODEL), row),
        out_shape=jax.ShapeDtypeStruct((batch, t_len, D_MODEL), F32),
        scratch_shapes=[pltpu.VMEM((tm, MIX_WIDTH), BF16)],
        compiler_params=_cparams(("parallel", "parallel")),
        name="out_proj",
    )(x, o_cmp, o_sel, o_win, gl, p, o_b, p, w_out, final_w)


def _reorder_w_in(w):
    def heads(off, n):
        return [w[:, off + i * LANE: off + (i + 1) * LANE] for i in range(n)]
    cols = (heads(_O_QA, 8) + heads(_O_QB, 8) + heads(_O_KB, 8) + heads(_O_KC, 2) + heads(_O_KS, 2)
            + heads(_O_KW, 2) + heads(_O_VC, 2) + heads(_O_ZA, 8) + heads(_O_ZB, 8) + heads(_O_VB, 8)
            + heads(_O_VS, 2) + heads(_O_VW, 2))
    w_r = jnp.concatenate(cols, axis=1).astype(BF16)
    w_g = jnp.pad(w[:, _O_G:_O_G + NSA_HEADS * NSA_BRANCHES],
                  ((0, 0), (0, LANE - NSA_HEADS * NSA_BRANCHES))).astype(BF16)
    return w_r, w_g


def _sel_map(t_len, n_selp):
    n_rows = t_len // CMP_STRIDE
    c0 = np.arange(n_rows)[:, None] * CMP_STRIDE
    s0 = np.arange(n_selp)[None, :] * SEL_BLOCK
    ov = np.minimum(c0 + CMP_BLOCK, s0 + SEL_BLOCK) - np.maximum(c0, s0)
    m = np.maximum(ov, 0) / CMP_BLOCK
    m[n_rows - 1:] = 0.0
    m[:, t_len // SEL_BLOCK:] = 0.0
    return jnp.asarray(m, dtype=BF16)


def _block_one_hot(t_len):
    blk = (np.arange(t_len) // SEL_BLOCK) % LANE
    return jnp.asarray(blk[:, None] == np.arange(LANE)[None, :], dtype=BF16)


def kernel(x, norm_w, w_in, cmp_pos, cmp_k_w1, cmp_k_w2, cmp_v_w1, cmp_v_w2, lam_q1, lam_k1, lam_q2, lam_k2,
           subln_w, w_out, final_norm_w):
    batch, t_len, _ = x.shape
    n_rows = t_len // CMP_STRIDE
    n_selp = -(-(t_len // SEL_BLOCK) // LANE) * LANE

    pos = jnp.arange(t_len, dtype=F32)
    inv = ROPE_THETA ** (-jnp.arange(0, HEAD_DIM, 2, dtype=F32) / HEAD_DIM)
    ang = pos[:, None] * inv[None, :]
    cos, sin = jnp.cos(ang), jnp.sin(ang)
    cos_f = jnp.concatenate([cos, cos], axis=1)
    sin_f = jnp.concatenate([-sin, sin], axis=1)

    w_r, w_g = _reorder_w_in(w_in[0])
    tm_in = min(1024, t_len)
    p, gl = _in_proj(x.reshape(batch * t_len, D_MODEL), norm_w[0][None, :], w_r, w_g, cos_f, sin_f,
                     batch, t_len, tm_in)

    half = CMP_STRIDE * HEAD_DIM
    w1cat = jnp.stack([jnp.concatenate([w[:half], w[half:]], axis=1) for w in (cmp_k_w1[0], cmp_v_w1[0])]).astype(BF16)
    w2 = jnp.stack([cmp_k_w2[0], cmp_v_w2[0]]).astype(BF16)
    pos2 = jnp.broadcast_to(cmp_pos[0].reshape(2, 1, half), (2, 8, half)).astype(BF16)
    kvc = _compress(p.reshape(batch, N_SLOTS, n_rows, half), w1cat, pos2, w2, batch, n_rows)

    o_cmp, bias = _cmp_select(p, kvc, _sel_map(t_len, n_selp), batch, t_len)
    o_sel = _sel_attn(p, bias, _block_one_hot(t_len), batch, t_len, tk=min(512, t_len))
    o_win = _win_attn(p, batch, t_len)

    lam4 = jnp.pad(jnp.stack([lam_q1[0], lam_k1[0], lam_q2[0], lam_k2[0]]), ((0, 4), (0, 0)))
    o_b = _diff_attn(p, lam4, subln_w[0][None, :], batch, t_len, tq=min(512, t_len), tk=min(512, t_len))

    return _out_proj(x, o_cmp, o_sel, o_win, gl, p, o_b, w_out[0].astype(BF16), final_norm_w[None, :],
                     batch, t_len, tm=min(512, t_len))
```

```python
import functools
import math

import numpy as np
import jax
import jax.numpy as jnp
from jax import lax
from jax.experimental import pallas as pl
from jax.experimental.pallas import tpu as pltpu

F32 = jnp.float32
BF16 = jnp.bfloat16

D_MODEL = 2048
HEAD_DIM = 128
NSA_HEADS = 8
NSA_GROUPS = 2
NSA_HPG = NSA_HEADS // NSA_GROUPS
NSA_WIDTH = NSA_HEADS * HEAD_DIM
NSA_KV = NSA_GROUPS * HEAD_DIM
NSA_BRANCHES = 3
CMP_BLOCK = 32
CMP_STRIDE = 16
CMP_HIDDEN = 512
SEL_BLOCK = 64
N_SELECT = 16
WINDOW = 512
DIFF_HEADS = 4
DIFF_QK = 128
DIFF_V = 256
DIFF_WIDTH = DIFF_HEADS * DIFF_V
MIX_WIDTH = NSA_WIDTH + DIFF_WIDTH
Q_BLOCK = 128
ROPE_THETA = 10000.0
NORM_EPS = 1e-6
SUBLN_EPS = 1e-5
NEG_INF = -1e30
FORCE_SCORE = 1e9
LAMBDA_INIT = 0.8 - 0.6 * math.exp(-0.3 * 0)

LANE = 128
VMEM_LIMIT = 56 * 1024 * 1024

SLOT_QA = 0
SLOT_QB = 8
SLOT_KB = 16
SLOT_KC = 24
SLOT_KS = 26
SLOT_KW = 28
N_ROPE_SLOTS = 30
SLOT_VC = 30
SLOT_ZA = 32
SLOT_ZB = 40
SLOT_VB = 48
SLOT_VS = 56
SLOT_VW = 58
N_SLOTS = 60
SLOTS_PER_STEP = 6
N_QUERY_SLOTS = 16
QK_EXP2_SCALE = (HEAD_DIM ** -0.5) * math.log2(math.e)

_OFF = np.concatenate([[0], np.cumsum([NSA_WIDTH, NSA_KV, NSA_KV, NSA_KV, NSA_KV, NSA_KV, NSA_KV,
                                       NSA_HEADS * NSA_BRANCHES, NSA_WIDTH, 1024, 1024, DIFF_WIDTH, DIFF_WIDTH])])
(_O_QA, _O_KC, _O_VC, _O_KS, _O_VS, _O_KW, _O_VW, _O_G, _O_ZA, _O_QB, _O_KB, _O_VB, _O_ZB, _O_END) = [int(v) for v in _OFF]


def _cparams(sem):
    return pltpu.CompilerParams(dimension_semantics=sem, vmem_limit_bytes=VMEM_LIMIT)


def _in_proj_kernel(x_ref, nw_ref, w_ref, wg_ref, cos_ref, sin_ref, p_ref, g_ref, h_ref, *, n_rope_steps):
    j = pl.program_id(1)

    @pl.when(j == 0)
    def _():
        x = x_ref[...]
        y = x * lax.rsqrt(jnp.mean(x * x, axis=-1, keepdims=True) + NORM_EPS)
        h = (y * nw_ref[...]).astype(BF16)
        h_ref[...] = h
        g_ref[0] = jnp.dot(h, wg_ref[...], preferred_element_type=F32)

    acc = jnp.dot(h_ref[...], w_ref[...], preferred_element_type=F32)

    @pl.when(j < n_rope_steps)
    def _():
        cos = cos_ref[...]
        sin = sin_ref[...]
        for s in range(SLOTS_PER_STEP):
            a = acc[:, s * LANE:(s + 1) * LANE]
            r = a * cos + pltpu.roll(a, HEAD_DIM // 2, 1) * sin
            f = jnp.where(j * SLOTS_PER_STEP + s < N_QUERY_SLOTS, QK_EXP2_SCALE, 1.0)
            p_ref[0, s] = (r * f).astype(BF16)

    @pl.when(j >= n_rope_steps)
    def _():
        for s in range(SLOTS_PER_STEP):
            p_ref[0, s] = acc[:, s * LANE:(s + 1) * LANE].astype(BF16)


def _in_proj(x2, norm_w, w_r, w_g, cos_f, sin_f, batch, t_len, tm):
    m = x2.shape[0]
    tiles_per_seq = t_len // tm
    n_steps = N_SLOTS // SLOTS_PER_STEP
    tn = SLOTS_PER_STEP * LANE
    kern = functools.partial(_in_proj_kernel, n_rope_steps=N_ROPE_SLOTS // SLOTS_PER_STEP)
    return pl.pallas_call(
        kern,
        grid=(m // tm, n_steps),
        in_specs=[
            pl.BlockSpec((tm, D_MODEL), lambda i, j: (i, 0)),
            pl.BlockSpec((1, D_MODEL), lambda i, j: (0, 0)),
            pl.BlockSpec((D_MODEL, tn), lambda i, j: (0, j)),
            pl.BlockSpec((D_MODEL, LANE), lambda i, j: (0, 0)),
            pl.BlockSpec((tm, LANE), lambda i, j: (i % tiles_per_seq, 0)),
            pl.BlockSpec((tm, LANE), lambda i, j: (i % tiles_per_seq, 0)),
        ],
        out_specs=[
            pl.BlockSpec((1, SLOTS_PER_STEP, tm, LANE), lambda i, j: (i // tiles_per_seq, j, i % tiles_per_seq, 0)),
            pl.BlockSpec((1, tm, LANE), lambda i, j: (i // tiles_per_seq, i % tiles_per_seq, 0)),
        ],
        out_shape=[
            jax.ShapeDtypeStruct((batch, N_SLOTS, t_len, LANE), BF16),
            jax.ShapeDtypeStruct((batch, t_len, LANE), F32),
        ],
        scratch_shapes=[pltpu.VMEM((tm, D_MODEL), BF16)],
        compiler_params=_cparams(("parallel", "arbitrary")),
        name="in_proj",
    )(x2, norm_w, w_r, w_g, cos_f, sin_f)


def _compress_kernel(r_ref, w1_ref, pos_ref, w2_ref, o_ref, *, n_rows):
    w1 = w1_ref[0]
    a = jnp.dot(r_ref[0, 0], w1, preferred_element_type=F32)
    c = (jnp.dot(pos_ref[0], w1[:, :CMP_HIDDEN], preferred_element_type=F32)
         + jnp.dot(pos_ref[1], w1[:, CMP_HIDDEN:], preferred_element_type=F32))[0:1]
    nxt = pltpu.roll(a[:, CMP_HIDDEN:], n_rows - 1, 0)
    hid = a[:, :CMP_HIDDEN] + nxt + c
    hid = hid * jax.nn.sigmoid(hid)
    out = jnp.dot(hid.astype(BF16), w2_ref[0], preferred_element_type=F32)
    row = lax.broadcasted_iota(jnp.int32, out.shape, 0)
    o_ref[0, 0, 0] = jnp.where(row < n_rows - 1, out, 0.0).astype(BF16)


def _compress(p16, w1cat, pos2, w2, batch, n_rows):
    kern = functools.partial(_compress_kernel, n_rows=n_rows)

    return pl.pallas_call(
        kern,
        grid=(batch, 2, NSA_GROUPS),
        in_specs=[
            pl.BlockSpec((1, 1, n_rows, CMP_STRIDE * HEAD_DIM), lambda b, kv, g: (b, kv * NSA_GROUPS + g, 0, 0)),
            pl.BlockSpec((1, CMP_STRIDE * HEAD_DIM, 2 * CMP_HIDDEN), lambda b, kv, g: (kv, 0, 0)),
            pl.BlockSpec((2, 8, CMP_STRIDE * HEAD_DIM), lambda b, kv, g: (0, 0, 0)),
            pl.BlockSpec((1, CMP_HIDDEN, HEAD_DIM), lambda b, kv, g: (kv, 0, 0)),
        ],
        out_specs=pl.BlockSpec((1, 1, 1, n_rows, HEAD_DIM), lambda b, kv, g: (b, kv, g, 0, 0)),
        out_shape=jax.ShapeDtypeStruct((batch, 2, NSA_GROUPS, n_rows, HEAD_DIM), BF16),
        compiler_params=_cparams(("parallel", "parallel", "parallel")),
        name="compress",
    )(p16, w1cat, pos2, w2)


def _cmp_select_kernel(q_ref, kc_ref, vc_ref, map_ref, o_ref, bias_ref, *, n_rows, n_top):
    c = pl.program_id(2)
    start = c * Q_BLOCK
    q = q_ref[0].reshape(NSA_HPG * Q_BLOCK, HEAD_DIM)
    s = lax.dot_general(q, kc_ref[0, 0, 0], (((1,), (1,)), ((), ())), preferred_element_type=F32)
    s = s.reshape(NSA_HPG, Q_BLOCK, n_rows)
    t_q = start + lax.broadcasted_iota(jnp.int32, (1, Q_BLOCK, n_rows), 1)
    n_id = lax.broadcasted_iota(jnp.int32, (1, Q_BLOCK, n_rows), 2)
    valid = n_id * CMP_STRIDE + (CMP_BLOCK - 1) <= t_q
    s = jnp.where(valid, s, NEG_INF)
    mx = jnp.max(s, axis=-1, keepdims=True)
    e = jnp.where(valid, jnp.exp2(s - mx), 0.0)
    den = jnp.sum(e, axis=-1, keepdims=True)
    p = e * (1.0 / jnp.maximum(den, 1e-30))
    o = jnp.dot(p.reshape(NSA_HPG * Q_BLOCK, n_rows).astype(BF16), vc_ref[0, 0, 0],
                preferred_element_type=F32)
    for h in range(NSA_HPG):
        o_ref[0, :, h * HEAD_DIM:(h + 1) * HEAD_DIM] = o[h * Q_BLOCK:(h + 1) * Q_BLOCK]

    psum = p[0] + p[1] + p[2] + p[3]
    p_hi = psum.astype(BF16)
    p_lo = (psum - p_hi.astype(F32)).astype(BF16)
    smap_t = map_ref[...]
    nt = (((1,), (1,)), ((), ()))
    imp = (lax.dot_general(smap_t, p_hi, nt, preferred_element_type=F32)
           + lax.dot_general(smap_t, p_lo, nt, preferred_element_type=F32))
    n_selp = imp.shape[0]
    blk = lax.broadcasted_iota(jnp.int32, (n_selp, Q_BLOCK), 0)
    cur = (start + lax.broadcasted_iota(jnp.int32, (n_selp, Q_BLOCK), 1)) // SEL_BLOCK
    forced = (blk == 0) | (blk == cur) | (blk == cur - 1)
    blk_f = blk.astype(F32)
    work = jnp.where(forced, -jnp.inf, jnp.where(blk > cur, NEG_INF, imp))
    picked = forced
    for _ in range(n_top - 3):
        top = jnp.max(work, axis=0, keepdims=True)
        first = jnp.min(jnp.where(work == top, blk_f, 1e9), axis=0, keepdims=True)
        hit = blk_f == first
        picked = picked | hit
        work = jnp.where(hit, -jnp.inf, work)
    keep = picked & (blk <= cur)
    bias_ref[0, 0] = jnp.where(keep, 0.0, NEG_INF).T.astype(BF16)


def _cmp_select(p, kvc, sel_map, batch, t_len):
    n_rows = t_len // CMP_STRIDE
    n_selp = sel_map.shape[0]
    n_top = min(N_SELECT, t_len // SEL_BLOCK)
    assert n_top >= 3
    kern = functools.partial(_cmp_select_kernel, n_rows=n_rows, n_top=n_top)
    return pl.pallas_call(
        kern,
        grid=(batch, NSA_GROUPS, t_len // Q_BLOCK),
        in_specs=[
            pl.BlockSpec((1, NSA_HPG, Q_BLOCK, HEAD_DIM), lambda b, g, c: (b, g, c, 0)),
            pl.BlockSpec((1, 1, 1, n_rows, HEAD_DIM), lambda b, g, c: (b, 0, g, 0, 0)),
            pl.BlockSpec((1, 1, 1, n_rows, HEAD_DIM), lambda b, g, c: (b, 1, g, 0, 0)),
            pl.BlockSpec((n_selp, n_rows), lambda b, g, c: (0, 0)),
        ],
        out_specs=[
            pl.BlockSpec((1, Q_BLOCK, NSA_HPG * HEAD_DIM), lambda b, g, c: (b, c, g)),
            pl.BlockSpec((1, 1, Q_BLOCK, n_selp), lambda b, g, c: (b, g, c, 0)),
        ],
        out_shape=[
            jax.ShapeDtypeStruct((batch, t_len, NSA_WIDTH), F32),
            jax.ShapeDtypeStruct((batch, NSA_GROUPS, t_len, n_selp), BF16),
        ],
        compiler_params=_cparams(("parallel", "parallel", "parallel")),
        name="cmp_select",
    )(p, kvc, kvc, sel_map)


_NT = (((1,), (1,)), ((), ()))


def _flash_update(s, v, rows, m_ref, l_ref, acc_ref):
    n_keys = s.shape[1]
    dv = acc_ref.shape[-1]
    m_prev = m_ref[rows, :]
    m_new = jnp.maximum(m_prev, jnp.max(s, axis=-1, keepdims=True))
    alpha = jnp.exp2(m_prev - m_new)
    p = jnp.exp2(s - pltpu.repeat(m_new, n_keys // LANE, axis=1))
    psum = p[:, :LANE]
    for cc in range(1, n_keys // LANE):
        psum = psum + p[:, cc * LANE:(cc + 1) * LANE]
    l_ref[rows, :] = alpha * l_ref[rows, :] + psum
    a_rep = alpha if dv == LANE else pltpu.repeat(alpha, dv // LANE, axis=1)
    acc_ref[rows, :] = a_rep * acc_ref[rows, :] + jnp.dot(p.astype(BF16), v, preferred_element_type=F32)
    m_ref[rows, :] = m_new


def _causal_triangle(rb):
    return lax.broadcasted_iota(jnp.int32, (rb, rb), 1) <= lax.broadcasted_iota(jnp.int32, (rb, rb), 0)


def _sel_attn_kernel(q_ref, bias_ref, k_ref, v_ref, hot_ref, o_ref, qa_ref, m_ref, l_ref, acc_ref,
                     *, tq, tk, rb, n_halves):
    start = pl.program_id(2) * tq
    n_rows = NSA_HPG * tq
    keys_per_half = LANE * SEL_BLOCK
    for hf in range(n_halves):
        for h in range(NSA_HPG):
            qa_ref[hf, h * tq:(h + 1) * tq, :LANE] = q_ref[0, h]
            qa_ref[hf, h * tq:(h + 1) * tq, LANE:] = bias_ref[0, 0, :, hf * LANE:(hf + 1) * LANE]
    m_ref[...] = jnp.full(m_ref.shape, NEG_INF, F32)
    l_ref[...] = jnp.zeros(l_ref.shape, F32)
    acc_ref[...] = jnp.zeros(acc_ref.shape, F32)

    def k_aug(k0, n):
        return jnp.concatenate([k_ref[0, 0, pl.ds(k0, n), :], hot_ref[pl.ds(k0, n), :]], axis=1)

    def body(j, carry):
        k0 = pl.multiple_of(j * tk, tk)
        ka, v, hf = k_aug(k0, tk), v_ref[0, 0, pl.ds(k0, tk), :], k0 // keys_per_half
        for r in range(n_rows // rb):
            rows = pl.ds(r * rb, rb)
            s = lax.dot_general(qa_ref[hf, rows, :], ka, _NT, preferred_element_type=F32)
            _flash_update(s, v, rows, m_ref, l_ref, acc_ref)
        return carry

    lax.fori_loop(0, start // tk, body, 0)

    tri = _causal_triangle(rb)
    for d in range(tq // rb):
        k0 = pl.multiple_of(start + d * rb, rb)
        ka, v, hf = k_aug(k0, rb), v_ref[0, 0, pl.ds(k0, rb), :], k0 // keys_per_half
        for r in range(n_rows // rb):
            q_off = (r * rb) % tq
            if d * rb > q_off:
                continue
            rows = pl.ds(r * rb, rb)
            s = lax.dot_general(qa_ref[hf, rows, :], ka, _NT, preferred_element_type=F32)
            if d * rb == q_off:
                s = jnp.where(tri, s, NEG_INF)
            _flash_update(s, v, rows, m_ref, l_ref, acc_ref)

    o = acc_ref[...] / jnp.sum(l_ref[...], axis=-1, keepdims=True)
    for h in range(NSA_HPG):
        o_ref[0, :, h * HEAD_DIM:(h + 1) * HEAD_DIM] = o[h * tq:(h + 1) * tq]


def _sel_attn(p, bias, hot, batch, t_len, tq, tk, rb):
    assert tq % tk == 0 and tk % rb == 0 and (LANE * SEL_BLOCK) % tk == 0
    n_selp = bias.shape[-1]
    n_halves = n_selp // LANE
    n_rows = NSA_HPG * tq
    kern = functools.partial(_sel_attn_kernel, tq=tq, tk=tk, rb=rb, n_halves=n_halves)
    return pl.pallas_call(
        kern,
        grid=(batch, NSA_GROUPS, t_len // tq),
        in_specs=[
            pl.BlockSpec((1, NSA_HPG, tq, HEAD_DIM), lambda b, g, c: (b, g, c, 0)),
            pl.BlockSpec((1, 1, tq, n_selp), lambda b, g, c: (b, g, c, 0)),
            pl.BlockSpec((1, 1, t_len, HEAD_DIM), lambda b, g, c: (b, SLOT_KS + g, 0, 0)),
            pl.BlockSpec((1, 1, t_len, HEAD_DIM), lambda b, g, c: (b, SLOT_VS + g, 0, 0)),
            pl.BlockSpec((t_len, LANE), lambda b, g, c: (0, 0)),
        ],
        out_specs=pl.BlockSpec((1, tq, NSA_HPG * HEAD_DIM), lambda b, g, c: (b, c, g)),
        out_shape=jax.ShapeDtypeStruct((batch, t_len, NSA_WIDTH), F32),
        scratch_shapes=[
            pltpu.VMEM((n_halves, n_rows, 2 * LANE), BF16),
            pltpu.VMEM((n_rows, LANE), F32),
            pltpu.VMEM((n_rows, LANE), F32),
            pltpu.VMEM((n_rows, HEAD_DIM), F32),
        ],
        compiler_params=_cparams(("parallel", "parallel", "arbitrary")),
        name="sel_attn",
    )(p, bias, p, p, hot)


def _win_attn_kernel(q_ref, k_ref, v_ref, o_ref, *, span):
    c = pl.program_id(2)
    start = c * Q_BLOCK
    rows = NSA_HPG * Q_BLOCK
    k0 = pl.multiple_of(jnp.maximum(start + Q_BLOCK - span, 0), Q_BLOCK)
    q = q_ref[0].reshape(rows, HEAD_DIM)
    s = lax.dot_general(q, k_ref[0, 0, pl.ds(k0, span), :], (((1,), (1,)), ((), ())),
                        preferred_element_type=F32)
    s = s.reshape(NSA_HPG, Q_BLOCK, span)
    t_q = start + lax.broadcasted_iota(jnp.int32, (1, Q_BLOCK, span), 1)
    k_pos = k0 + lax.broadcasted_iota(jnp.int32, (1, Q_BLOCK, span), 2)
    valid = (k_pos <= t_q) & (k_pos > t_q - WINDOW)
    s = jnp.where(valid, s, NEG_INF)
    mx = jnp.max(s, axis=-1, keepdims=True)
    e = jnp.exp2(s - mx)
    den = jnp.sum(e, axis=-1, keepdims=True)
    o = jnp.dot(e.reshape(rows, span).astype(BF16), v_ref[0, 0, pl.ds(k0, span), :],
                preferred_element_type=F32)
    o = o / den.reshape(rows, 1)
    for h in range(NSA_HPG):
        o_ref[0, :, h * HEAD_DIM:(h + 1) * HEAD_DIM] = o[h * Q_BLOCK:(h + 1) * Q_BLOCK]


def _win_attn(p, batch, t_len):
    span = min(WINDOW + Q_BLOCK, t_len)
    kern = functools.partial(_win_attn_kernel, span=span)
    return pl.pallas_call(
        kern,
        grid=(batch, NSA_GROUPS, t_len // Q_BLOCK),
        in_specs=[
            pl.BlockSpec((1, NSA_HPG, Q_BLOCK, HEAD_DIM), lambda b, g, c: (b, g, c, 0)),
            pl.BlockSpec((1, 1, t_len, HEAD_DIM), lambda b, g, c: (b, SLOT_KW + g, 0, 0)),
            pl.BlockSpec((1, 1, t_len, HEAD_DIM), lambda b, g, c: (b, SLOT_VW + g, 0, 0)),
        ],
        out_specs=pl.BlockSpec((1, Q_BLOCK, NSA_HPG * HEAD_DIM), lambda b, g, c: (b, c, g)),
        out_shape=jax.ShapeDtypeStruct((batch, t_len, NSA_WIDTH), F32),
        compiler_params=_cparams(("parallel", "parallel", "parallel")),
        name="win_attn",
    )(p, p, p)


def _diff_attn_kernel(q1_ref, q2_ref, k1_ref, k2_ref, v_ref, lam_ref, sw_ref, o_ref,
                      m_ref, l_ref, acc_ref, *, tq, tk, rb):
    start = pl.program_id(2) * tq
    m_ref[...] = jnp.full(m_ref.shape, NEG_INF, F32)
    l_ref[...] = jnp.zeros(l_ref.shape, F32)
    acc_ref[...] = jnp.zeros(acc_ref.shape, F32)
    q_refs = (q1_ref, q2_ref)
    k_refs = (k1_ref, k2_ref)

    def v_tile(k0, n):
        return jnp.concatenate([v_ref[0, 0, pl.ds(k0, n), :], v_ref[0, 1, pl.ds(k0, n), :]], axis=1)

    def body(j, carry):
        k0 = pl.multiple_of(j * tk, tk)
        v = v_tile(k0, tk)
        for mp in range(2):
            kt = k_refs[mp][0, 0, pl.ds(k0, tk), :]
            for r in range(tq // rb):
                rows = pl.ds(r * rb, rb)
                s = lax.dot_general(q_refs[mp][0, 0, rows, :], kt, _NT, preferred_element_type=F32)
                _flash_update(s, v, rows, m_ref.at[mp], l_ref.at[mp], acc_ref.at[mp])
        return carry

    lax.fori_loop(0, start // tk, body, 0)

    tri = _causal_triangle(rb)
    for d in range(tq // rb):
        k0 = pl.multiple_of(start + d * rb, rb)
        v = v_tile(k0, rb)
        for mp in range(2):
            kt = k_refs[mp][0, 0, pl.ds(k0, rb), :]
            for r in range(d, tq // rb):
                rows = pl.ds(r * rb, rb)
                s = lax.dot_general(q_refs[mp][0, 0, rows, :], kt, _NT, preferred_element_type=F32)
                if r == d:
                    s = jnp.where(tri, s, NEG_INF)
                _flash_update(s, v, rows, m_ref.at[mp], l_ref.at[mp], acc_ref.at[mp])

    lv = lam_ref[...]
    lam = (jnp.exp(jnp.sum(lv[0:1] * lv[1:2], axis=-1, keepdims=True))
           - jnp.exp(jnp.sum(lv[2:3] * lv[3:4], axis=-1, keepdims=True)) + LAMBDA_INIT)
    l1 = jnp.sum(l_ref[0], axis=-1, keepdims=True)
    l2 = jnp.sum(l_ref[1], axis=-1, keepdims=True)
    o = acc_ref[0] / l1 - lam * (acc_ref[1] / l2)
    y = o * lax.rsqrt(jnp.mean(o * o, axis=-1, keepdims=True) + SUBLN_EPS)
    o_ref[0] = y * sw_ref[...] * (1.0 - LAMBDA_INIT)


def _diff_attn(p, lam4, subln_w, batch, t_len, tq, tk, rb):
    assert tq % tk == 0 and tk % rb == 0
    kern = functools.partial(_diff_attn_kernel, tq=tq, tk=tk, rb=rb)
    return pl.pallas_call(
        kern,
        grid=(batch, DIFF_HEADS, t_len // tq),
        in_specs=[
            pl.BlockSpec((1, 1, tq, DIFF_QK), lambda b, h, i: (b, SLOT_QB + h, i, 0)),
            pl.BlockSpec((1, 1, tq, DIFF_QK), lambda b, h, i: (b, SLOT_QB + DIFF_HEADS + h, i, 0)),
            pl.BlockSpec((1, 1, t_len, DIFF_QK), lambda b, h, i: (b, SLOT_KB + h, 0, 0)),
            pl.BlockSpec((1, 1, t_len, DIFF_QK), lambda b, h, i: (b, SLOT_KB + DIFF_HEADS + h, 0, 0)),
            pl.BlockSpec((1, 2, t_len, LANE), lambda b, h, i: (b, SLOT_VB // 2 + h, 0, 0)),
            pl.BlockSpec((8, DIFF_QK), lambda b, h, i: (0, 0)),
            pl.BlockSpec((1, DIFF_V), lambda b, h, i: (0, 0)),
        ],
        out_specs=pl.BlockSpec((1, tq, DIFF_V), lambda b, h, i: (b, i, h)),
        out_shape=jax.ShapeDtypeStruct((batch, t_len, DIFF_WIDTH), F32),
        scratch_shapes=[
            pltpu.VMEM((2, tq, LANE), F32),
            pltpu.VMEM((2, tq, LANE), F32),
            pltpu.VMEM((2, tq, DIFF_V), F32),
        ],
        compiler_params=_cparams(("parallel", "parallel", "arbitrary")),
        name="diff_attn",
    )(p, p, p, p, p, lam4, subln_w)


def _out_proj_kernel(x_ref, oc_ref, os_ref, ow_ref, g_ref, za_ref, ob_ref, zb_ref, w_ref, fw_ref,
                     o_ref, mix_ref):
    gates = jax.nn.sigmoid(g_ref[0])
    tm = gates.shape[0]
    for hd in range(NSA_HEADS):
        cols = slice(hd * HEAD_DIM, (hd + 1) * HEAD_DIM)
        g0 = jnp.broadcast_to(gates[:, 3 * hd:3 * hd + 1], (tm, HEAD_DIM))
        g1 = jnp.broadcast_to(gates[:, 3 * hd + 1:3 * hd + 2], (tm, HEAD_DIM))
        g2 = jnp.broadcast_to(gates[:, 3 * hd + 2:3 * hd + 3], (tm, HEAD_DIM))
        o = g0 * oc_ref[0, :, cols] + g1 * os_ref[0, :, cols] + g2 * ow_ref[0, :, cols]
        z = za_ref[0, hd].astype(F32)
        mix_ref[:, cols] = (o * (z * jax.nn.sigmoid(z))).astype(BF16)
    for sl in range(DIFF_WIDTH // LANE):
        cols = slice(sl * LANE, (sl + 1) * LANE)
        z = zb_ref[0, sl].astype(F32)
        mix_ref[:, NSA_WIDTH + sl * LANE:NSA_WIDTH + (sl + 1) * LANE] = (
            ob_ref[0, :, cols] * (z * jax.nn.sigmoid(z))).astype(BF16)
    y = x_ref[0] + jnp.dot(mix_ref[...], w_ref[...], preferred_element_type=F32)
    y = y * lax.rsqrt(jnp.mean(y * y, axis=-1, keepdims=True) + NORM_EPS)
    o_ref[0] = y * fw_ref[...]


def _out_proj(x, o_cmp, o_sel, o_win, gl, p, o_b, w_out, final_w, batch, t_len, tm):
    row = lambda b, i: (b, i, 0)
    return pl.pallas_call(
        _out_proj_kernel,
        grid=(batch, t_len // tm),
        in_specs=[
            pl.BlockSpec((1, tm, D_MODEL), row),
            pl.BlockSpec((1, tm, NSA_WIDTH), row),
            pl.BlockSpec((1, tm, NSA_WIDTH), row),
            pl.BlockSpec((1, tm, NSA_WIDTH), row),
            pl.BlockSpec((1, tm, LANE), row),
            pl.BlockSpec((1, 8, tm, LANE), lambda b, i: (b, SLOT_ZA // 8, i, 0)),
            pl.BlockSpec((1, tm, DIFF_WIDTH), row),
            pl.BlockSpec((1, 8, tm, LANE), lambda b, i: (b, SLOT_ZB // 8, i, 0)),
            pl.BlockSpec((MIX_WIDTH, D_MODEL), lambda b, i: (0, 0)),
            pl.BlockSpec((1, D_MODEL), lambda b, i: (0, 0)),
        ],
        out_specs=pl.BlockSpec((1, tm, D_MODEL), row),
        out_shape=jax.ShapeDtypeStruct((batch, t_len, D_MODEL), F32),
        scratch_shapes=[pltpu.VMEM((tm, MIX_WIDTH), BF16)],
        compiler_params=_cparams(("parallel", "parallel")),
        name="out_proj",
    )(x, o_cmp, o_sel, o_win, gl, p, o_b, p, w_out, final_w)


def _reorder_w_in(w):
    def heads(off, n):
        return [w[:, off + i * LANE: off + (i + 1) * LANE] for i in range(n)]
    cols = (heads(_O_QA, 8) + heads(_O_QB, 8) + heads(_O_KB, 8) + heads(_O_KC, 2) + heads(_O_KS, 2)
            + heads(_O_KW, 2) + heads(_O_VC, 2) + heads(_O_ZA, 8) + heads(_O_ZB, 8) + heads(_O_VB, 8)
            + heads(_O_VS, 2) + heads(_O_VW, 2))
    w_r = jnp.concatenate(cols, axis=1).astype(BF16)
    w_g = jnp.pad(w[:, _O_G:_O_G + NSA_HEADS * NSA_BRANCHES],
                  ((0, 0), (0, LANE - NSA_HEADS * NSA_BRANCHES))).astype(BF16)
    return w_r, w_g


def _sel_map(t_len, n_selp):
    n_rows = t_len // CMP_STRIDE
    c0 = np.arange(n_rows)[:, None] * CMP_STRIDE
    s0 = np.arange(n_selp)[None, :] * SEL_BLOCK
    ov = np.minimum(c0 + CMP_BLOCK, s0 + SEL_BLOCK) - np.maximum(c0, s0)
    m = np.maximum(ov, 0) / CMP_BLOCK
    m[n_rows - 1:] = 0.0
    m[:, t_len // SEL_BLOCK:] = 0.0
    return jnp.asarray(m.T, dtype=BF16)


def _block_one_hot(t_len):
    blk = (np.arange(t_len) // SEL_BLOCK) % LANE
    return jnp.asarray(blk[:, None] == np.arange(LANE)[None, :], dtype=BF16)


def kernel(x, norm_w, w_in, cmp_pos, cmp_k_w1, cmp_k_w2, cmp_v_w1, cmp_v_w2, lam_q1, lam_k1, lam_q2, lam_k2,
           subln_w, w_out, final_norm_w):
    batch, t_len, _ = x.shape
    n_rows = t_len // CMP_STRIDE
    n_selp = -(-(t_len // SEL_BLOCK) // LANE) * LANE

    pos = jnp.arange(t_len, dtype=F32)
    inv = ROPE_THETA ** (-jnp.arange(0, HEAD_DIM, 2, dtype=F32) / HEAD_DIM)
    ang = pos[:, None] * inv[None, :]
    cos, sin = jnp.cos(ang), jnp.sin(ang)
    cos_f = jnp.concatenate([cos, cos], axis=1)
    sin_f = jnp.concatenate([-sin, sin], axis=1)

    w_r, w_g = _reorder_w_in(w_in[0])
    tm_in = min(1024, t_len)
    p, gl = _in_proj(x.reshape(batch * t_len, D_MODEL), norm_w[0][None, :], w_r, w_g, cos_f, sin_f,
                     batch, t_len, tm_in)

    half = CMP_STRIDE * HEAD_DIM
    w1cat = jnp.stack([jnp.concatenate([w[:half], w[half:]], axis=1) for w in (cmp_k_w1[0], cmp_v_w1[0])]).astype(BF16)
    w2 = jnp.stack([cmp_k_w2[0], cmp_v_w2[0]]).astype(BF16)
    pos2 = jnp.broadcast_to(cmp_pos[0].reshape(2, 1, half), (2, 8, half)).astype(BF16)
    pc = jnp.concatenate([p[:, SLOT_KC:SLOT_KC + NSA_GROUPS], p[:, SLOT_VC:SLOT_VC + NSA_GROUPS]], axis=1)
    kvc = _compress(pc.reshape(batch, 2 * NSA_GROUPS, n_rows, half), w1cat, pos2, w2, batch, n_rows)

    o_cmp, bias = _cmp_select(p, kvc, _sel_map(t_len, n_selp), batch, t_len)
    o_sel = _sel_attn(p, bias, _block_one_hot(t_len), batch, t_len, tq=512, tk=512, rb=256)
    o_win = _win_attn(p, batch, t_len)

    lam4 = jnp.pad(jnp.stack([lam_q1[0], lam_k1[0], lam_q2[0], lam_k2[0]]), ((0, 4), (0, 0)))
    o_b = _diff_attn(p, lam4, subln_w[0][None, :], batch, t_len, tq=min(2048, t_len), tk=512, rb=256)

    return _out_proj(x, o_cmp, o_sel, o_win, gl, p, o_b, w_out[0].astype(BF16), final_norm_w[None, :],
                     batch, t_len, tm=min(512, t_len))
```

```python
import functools
import math

import numpy as np
import jax
import jax.numpy as jnp
from jax import lax
from jax.experimental import pallas as pl
from jax.experimental.pallas import tpu as pltpu

F32 = jnp.float32
BF16 = jnp.bfloat16

D_MODEL = 2048
HEAD_DIM = 128
NSA_HEADS = 8
NSA_GROUPS = 2
NSA_HPG = NSA_HEADS // NSA_GROUPS
NSA_WIDTH = NSA_HEADS * HEAD_DIM
NSA_KV = NSA_GROUPS * HEAD_DIM
NSA_BRANCHES = 3
CMP_BLOCK = 32
CMP_STRIDE = 16
CMP_HIDDEN = 512
SEL_BLOCK = 64
N_SELECT = 16
WINDOW = 512
DIFF_HEADS = 4
DIFF_QK = 128
DIFF_V = 256
DIFF_WIDTH = DIFF_HEADS * DIFF_V
MIX_WIDTH = NSA_WIDTH + DIFF_WIDTH
Q_BLOCK = 128
ROPE_THETA = 10000.0
NORM_EPS = 1e-6
SUBLN_EPS = 1e-5
NEG_INF = -1e30
FORCE_SCORE = 1e9
LAMBDA_INIT = 0.8 - 0.6 * math.exp(-0.3 * 0)

LANE = 128
VMEM_LIMIT = 56 * 1024 * 1024

SLOT_QA = 0
SLOT_QB = 8
SLOT_KB = 16
SLOT_KC = 24
SLOT_KS = 26
SLOT_KW = 28
N_ROPE_SLOTS = 30
SLOT_VC = 30
SLOT_ZA = 32
SLOT_ZB = 40
SLOT_VB = 48
SLOT_VS = 56
SLOT_VW = 58
N_SLOTS = 60
SLOTS_PER_STEP = 6
N_QUERY_SLOTS = 16
QK_EXP2_SCALE = (HEAD_DIM ** -0.5) * math.log2(math.e)

_OFF = np.concatenate([[0], np.cumsum([NSA_WIDTH, NSA_KV, NSA_KV, NSA_KV, NSA_KV, NSA_KV, NSA_KV,
                                       NSA_HEADS * NSA_BRANCHES, NSA_WIDTH, 1024, 1024, DIFF_WIDTH, DIFF_WIDTH])])
(_O_QA, _O_KC, _O_VC, _O_KS, _O_VS, _O_KW, _O_VW, _O_G, _O_ZA, _O_QB, _O_KB, _O_VB, _O_ZB, _O_END) = [int(v) for v in _OFF]


def _cparams(sem):
    return pltpu.CompilerParams(dimension_semantics=sem, vmem_limit_bytes=VMEM_LIMIT)


def _in_proj_kernel(x_ref, nw_ref, w_ref, wg_ref, cos_ref, sin_ref, p_ref, g_ref, h_ref, *, n_rope_steps):
    j = pl.program_id(1)

    @pl.when(j == 0)
    def _():
        x = x_ref[...]
        y = x * lax.rsqrt(jnp.mean(x * x, axis=-1, keepdims=True) + NORM_EPS)
        h = (y * nw_ref[...]).astype(BF16)
        h_ref[...] = h
        g_ref[0] = jnp.dot(h, wg_ref[...], preferred_element_type=F32)

    acc = jnp.dot(h_ref[...], w_ref[...], preferred_element_type=F32)

    @pl.when(j < n_rope_steps)
    def _():
        cos = cos_ref[...]
        sin = sin_ref[...]
        for s in range(SLOTS_PER_STEP):
            a = acc[:, s * LANE:(s + 1) * LANE]
            r = a * cos + pltpu.roll(a, HEAD_DIM // 2, 1) * sin
            f = jnp.where(j * SLOTS_PER_STEP + s < N_QUERY_SLOTS, QK_EXP2_SCALE, 1.0)
            p_ref[0, s] = (r * f).astype(BF16)

    @pl.when(j >= n_rope_steps)
    def _():
        for s in range(SLOTS_PER_STEP):
            p_ref[0, s] = acc[:, s * LANE:(s + 1) * LANE].astype(BF16)


def _in_proj(x2, norm_w, w_r, w_g, cos_f, sin_f, batch, t_len, tm):
    m = x2.shape[0]
    tiles_per_seq = t_len // tm
    n_steps = N_SLOTS // SLOTS_PER_STEP
    tn = SLOTS_PER_STEP * LANE
    kern = functools.partial(_in_proj_kernel, n_rope_steps=N_ROPE_SLOTS // SLOTS_PER_STEP)
    return pl.pallas_call(
        kern,
        grid=(m // tm, n_steps),
        in_specs=[
            pl.BlockSpec((tm, D_MODEL), lambda i, j: (i, 0)),
            pl.BlockSpec((1, D_MODEL), lambda i, j: (0, 0)),
            pl.BlockSpec((D_MODEL, tn), lambda i, j: (0, j)),
            pl.BlockSpec((D_MODEL, LANE), lambda i, j: (0, 0)),
            pl.BlockSpec((tm, LANE), lambda i, j: (i % tiles_per_seq, 0)),
            pl.BlockSpec((tm, LANE), lambda i, j: (i % tiles_per_seq, 0)),
        ],
        out_specs=[
            pl.BlockSpec((1, SLOTS_PER_STEP, tm, LANE), lambda i, j: (i // tiles_per_seq, j, i % tiles_per_seq, 0)),
            pl.BlockSpec((1, tm, LANE), lambda i, j: (i // tiles_per_seq, i % tiles_per_seq, 0)),
        ],
        out_shape=[
            jax.ShapeDtypeStruct((batch, N_SLOTS, t_len, LANE), BF16),
            jax.ShapeDtypeStruct((batch, t_len, LANE), F32),
        ],
        scratch_shapes=[pltpu.VMEM((tm, D_MODEL), BF16)],
        compiler_params=_cparams(("parallel", "arbitrary")),
        name="in_proj",
    )(x2, norm_w, w_r, w_g, cos_f, sin_f)


def _compress_kernel(r_ref, w1_ref, pos_ref, w2_ref, o_ref, *, n_rows):
    w1 = w1_ref[0]
    a = jnp.dot(r_ref[0, 0], w1, preferred_element_type=F32)
    c = (jnp.dot(pos_ref[0], w1[:, :CMP_HIDDEN], preferred_element_type=F32)
         + jnp.dot(pos_ref[1], w1[:, CMP_HIDDEN:], preferred_element_type=F32))[0:1]
    nxt = pltpu.roll(a[:, CMP_HIDDEN:], n_rows - 1, 0)
    hid = a[:, :CMP_HIDDEN] + nxt + c
    hid = hid * jax.nn.sigmoid(hid)
    out = jnp.dot(hid.astype(BF16), w2_ref[0], preferred_element_type=F32)
    row = lax.broadcasted_iota(jnp.int32, out.shape, 0)
    o_ref[0, 0, 0] = jnp.where(row < n_rows - 1, out, 0.0).astype(BF16)


def _compress(p16, w1cat, pos2, w2, batch, n_rows):
    kern = functools.partial(_compress_kernel, n_rows=n_rows)

    return pl.pallas_call(
        kern,
        grid=(batch, 2, NSA_GROUPS),
        in_specs=[
            pl.BlockSpec((1, 1, n_rows, CMP_STRIDE * HEAD_DIM), lambda b, kv, g: (b, kv * NSA_GROUPS + g, 0, 0)),
            pl.BlockSpec((1, CMP_STRIDE * HEAD_DIM, 2 * CMP_HIDDEN), lambda b, kv, g: (kv, 0, 0)),
            pl.BlockSpec((2, 8, CMP_STRIDE * HEAD_DIM), lambda b, kv, g: (0, 0, 0)),
            pl.BlockSpec((1, CMP_HIDDEN, HEAD_DIM), lambda b, kv, g: (kv, 0, 0)),
        ],
        out_specs=pl.BlockSpec((1, 1, 1, n_rows, HEAD_DIM), lambda b, kv, g: (b, kv, g, 0, 0)),
        out_shape=jax.ShapeDtypeStruct((batch, 2, NSA_GROUPS, n_rows, HEAD_DIM), BF16),
        compiler_params=_cparams(("parallel", "parallel", "parallel")),
        name="compress",
    )(p16, w1cat, pos2, w2)


def _cmp_select_kernel(q_ref, kc_ref, vc_ref, map_ref, o_ref, bias_ref, *, n_rows, n_top):
    c = pl.program_id(2)
    start = c * Q_BLOCK
    q = q_ref[0].reshape(NSA_HPG * Q_BLOCK, HEAD_DIM)
    s = lax.dot_general(q, kc_ref[0, 0, 0], (((1,), (1,)), ((), ())), preferred_element_type=F32)
    s = s.reshape(NSA_HPG, Q_BLOCK, n_rows)
    t_q = start + lax.broadcasted_iota(jnp.int32, (1, Q_BLOCK, n_rows), 1)
    n_id = lax.broadcasted_iota(jnp.int32, (1, Q_BLOCK, n_rows), 2)
    valid = n_id * CMP_STRIDE + (CMP_BLOCK - 1) <= t_q
    s = jnp.where(valid, s, NEG_INF)
    mx = jnp.max(s, axis=-1, keepdims=True)
    e = jnp.where(valid, jnp.exp2(s - mx), 0.0)
    den = jnp.sum(e, axis=-1, keepdims=True)
    p = e * (1.0 / jnp.maximum(den, 1e-30))
    o = jnp.dot(p.reshape(NSA_HPG * Q_BLOCK, n_rows).astype(BF16), vc_ref[0, 0, 0],
                preferred_element_type=F32)
    for h in range(NSA_HPG):
        o_ref[0, :, h * HEAD_DIM:(h + 1) * HEAD_DIM] = o[h * Q_BLOCK:(h + 1) * Q_BLOCK]

    psum = p[0] + p[1] + p[2] + p[3]
    p_hi = psum.astype(BF16)
    p_lo = (psum - p_hi.astype(F32)).astype(BF16)
    smap_t = map_ref[...]
    nt = (((1,), (1,)), ((), ()))
    imp = (lax.dot_general(smap_t, p_hi, nt, preferred_element_type=F32)
           + lax.dot_general(smap_t, p_lo, nt, preferred_element_type=F32))
    n_selp = imp.shape[0]
    blk = lax.broadcasted_iota(jnp.int32, (n_selp, Q_BLOCK), 0)
    cur = (start + lax.broadcasted_iota(jnp.int32, (n_selp, Q_BLOCK), 1)) // SEL_BLOCK
    forced = (blk == 0) | (blk == cur) | (blk == cur - 1)
    blk_f = blk.astype(F32)
    work = jnp.where(forced, -jnp.inf, jnp.where(blk > cur, NEG_INF, imp))
    picked = forced
    for _ in range(n_top - 3):
        top = jnp.max(work, axis=0, keepdims=True)
        first = jnp.min(jnp.where(work == top, blk_f, 1e9), axis=0, keepdims=True)
        hit = blk_f == first
        picked = picked | hit
        work = jnp.where(hit, -jnp.inf, work)
    keep = picked & (blk <= cur)
    bias_ref[0, 0] = jnp.where(keep, 0.0, NEG_INF).T.astype(BF16)


def _cmp_select(p, kvc, sel_map, batch, t_len):
    n_rows = t_len // CMP_STRIDE
    n_selp = sel_map.shape[0]
    n_top = min(N_SELECT, t_len // SEL_BLOCK)
    assert n_top >= 3
    kern = functools.partial(_cmp_select_kernel, n_rows=n_rows, n_top=n_top)
    return pl.pallas_call(
        kern,
        grid=(batch, NSA_GROUPS, t_len // Q_BLOCK),
        in_specs=[
            pl.BlockSpec((1, NSA_HPG, Q_BLOCK, HEAD_DIM), lambda b, g, c: (b, g, c, 0)),
            pl.BlockSpec((1, 1, 1, n_rows, HEAD_DIM), lambda b, g, c: (b, 0, g, 0, 0)),
            pl.BlockSpec((1, 1, 1, n_rows, HEAD_DIM), lambda b, g, c: (b, 1, g, 0, 0)),
            pl.BlockSpec((n_selp, n_rows), lambda b, g, c: (0, 0)),
        ],
        out_specs=[
            pl.BlockSpec((1, Q_BLOCK, NSA_HPG * HEAD_DIM), lambda b, g, c: (b, c, g)),
            pl.BlockSpec((1, 1, Q_BLOCK, n_selp), lambda b, g, c: (b, g, c, 0)),
        ],
        out_shape=[
            jax.ShapeDtypeStruct((batch, t_len, NSA_WIDTH), F32),
            jax.ShapeDtypeStruct((batch, NSA_GROUPS, t_len, n_selp), BF16),
        ],
        compiler_params=_cparams(("parallel", "parallel", "parallel")),
        name="cmp_select",
    )(p, kvc, kvc, sel_map)


_NT = (((1,), (1,)), ((), ()))


def _lane_tile(x, n):
    return x if n == 1 else jnp.concatenate([x] * n, axis=1)


def _flash_update(s, v, rows, m_ref, l_ref, acc_ref):
    n_keys = s.shape[1]
    dv = acc_ref.shape[-1]
    m_prev = m_ref[rows, :]
    m_new = jnp.maximum(m_prev, jnp.max(s, axis=-1, keepdims=True))
    alpha = jnp.exp2(m_prev - m_new)
    p = jnp.exp2(s - _lane_tile(m_new, n_keys // LANE))
    if l_ref is not None:
        psum = p[:, :LANE]
        for cc in range(1, n_keys // LANE):
            psum = psum + p[:, cc * LANE:(cc + 1) * LANE]
        l_ref[rows, :] = alpha * l_ref[rows, :] + psum
    acc_ref[rows, :] = (_lane_tile(alpha, dv // LANE) * acc_ref[rows, :]
                        + jnp.dot(p.astype(BF16), v, preferred_element_type=F32))
    m_ref[rows, :] = m_new


def _causal_triangle(rb):
    return lax.broadcasted_iota(jnp.int32, (rb, rb), 1) <= lax.broadcasted_iota(jnp.int32, (rb, rb), 0)


def _sel_attn_kernel(q_ref, bias_ref, k_ref, v_ref, hot_ref, o_ref, qa_ref, m_ref, acc_ref,
                     *, tq, tk, rb, n_halves):
    start = pl.program_id(2) * tq
    n_rows = NSA_HPG * tq
    keys_per_half = LANE * SEL_BLOCK
    for hf in range(n_halves):
        for h in range(NSA_HPG):
            qa_ref[hf, h * tq:(h + 1) * tq, :LANE] = q_ref[0, h]
            qa_ref[hf, h * tq:(h + 1) * tq, LANE:] = bias_ref[0, 0, :, hf * LANE:(hf + 1) * LANE]
    m_ref[...] = jnp.full(m_ref.shape, NEG_INF, F32)
    acc_ref[...] = jnp.zeros(acc_ref.shape, F32)

    def k_aug(k0, n):
        return jnp.concatenate([k_ref[0, 0, pl.ds(k0, n), :], hot_ref[pl.ds(k0, n), :]], axis=1)

    def v_aug(k0, n):
        return jnp.concatenate([v_ref[0, 0, pl.ds(k0, n), :], jnp.ones((n, LANE), BF16)], axis=1)

    def body(j, carry):
        k0 = pl.multiple_of(j * tk, tk)
        ka, v, hf = k_aug(k0, tk), v_aug(k0, tk), k0 // keys_per_half
        for r in range(n_rows // rb):
            rows = pl.ds(r * rb, rb)
            s = lax.dot_general(qa_ref[hf, rows, :], ka, _NT, preferred_element_type=F32)
            _flash_update(s, v, rows, m_ref, None, acc_ref)
        return carry

    lax.fori_loop(0, start // tk, body, 0)

    tri = _causal_triangle(rb)
    for d in range(tq // rb):
        k0 = pl.multiple_of(start + d * rb, rb)
        ka, v, hf = k_aug(k0, rb), v_aug(k0, rb), k0 // keys_per_half
        for r in range(n_rows // rb):
            q_off = (r * rb) % tq
            if d * rb > q_off:
                continue
            rows = pl.ds(r * rb, rb)
            s = lax.dot_general(qa_ref[hf, rows, :], ka, _NT, preferred_element_type=F32)
            if d * rb == q_off:
                s = jnp.where(tri, s, NEG_INF)
            _flash_update(s, v, rows, m_ref, None, acc_ref)

    for h in range(NSA_HPG):
        hr = slice(h * tq, (h + 1) * tq)
        o_ref[0, :, h * HEAD_DIM:(h + 1) * HEAD_DIM] = acc_ref[hr, :HEAD_DIM] / acc_ref[hr, HEAD_DIM:]


def _resident(block_shape, index_map):
    return pl.BlockSpec(block_shape, index_map, pipeline_mode=pl.Buffered(1))


def _sel_attn(p, bias, hot, batch, t_len, tq, tk, rb):
    assert tq % tk == 0 and tk % rb == 0 and (LANE * SEL_BLOCK) % tk == 0
    n_selp = bias.shape[-1]
    n_halves = n_selp // LANE
    n_rows = NSA_HPG * tq
    kern = functools.partial(_sel_attn_kernel, tq=tq, tk=tk, rb=rb, n_halves=n_halves)
    return pl.pallas_call(
        kern,
        grid=(batch, NSA_GROUPS, t_len // tq),
        in_specs=[
            pl.BlockSpec((1, NSA_HPG, tq, HEAD_DIM), lambda b, g, c: (b, g, c, 0)),
            pl.BlockSpec((1, 1, tq, n_selp), lambda b, g, c: (b, g, c, 0)),
            _resident((1, 1, t_len, HEAD_DIM), lambda b, g, c: (b, SLOT_KS + g, 0, 0)),
            _resident((1, 1, t_len, HEAD_DIM), lambda b, g, c: (b, SLOT_VS + g, 0, 0)),
            _resident((t_len, LANE), lambda b, g, c: (0, 0)),
        ],
        out_specs=pl.BlockSpec((1, tq, NSA_HPG * HEAD_DIM), lambda b, g, c: (b, c, g)),
        out_shape=jax.ShapeDtypeStruct((batch, t_len, NSA_WIDTH), F32),
        scratch_shapes=[
            pltpu.VMEM((n_halves, n_rows, 2 * LANE), BF16),
            pltpu.VMEM((n_rows, LANE), F32),
            pltpu.VMEM((n_rows, 2 * HEAD_DIM), F32),
        ],
        compiler_params=_cparams(("parallel", "parallel", "arbitrary")),
        name="sel_attn",
    )(p, bias, p, p, hot)


def _win_attn_kernel(q_ref, k_ref, v_ref, o_ref, *, span):
    c = pl.program_id(2)
    start = c * Q_BLOCK
    rows = NSA_HPG * Q_BLOCK
    k0 = pl.multiple_of(jnp.maximum(start + Q_BLOCK - span, 0), Q_BLOCK)
    q = q_ref[0].reshape(rows, HEAD_DIM)
    s = lax.dot_general(q, k_ref[0, 0, pl.ds(k0, span), :], (((1,), (1,)), ((), ())),
                        preferred_element_type=F32)
    s = s.reshape(NSA_HPG, Q_BLOCK, span)
    t_q = start + lax.broadcasted_iota(jnp.int32, (1, Q_BLOCK, span), 1)
    k_pos = k0 + lax.broadcasted_iota(jnp.int32, (1, Q_BLOCK, span), 2)
    valid = (k_pos <= t_q) & (k_pos > t_q - WINDOW)
    s = jnp.where(valid, s, NEG_INF)
    mx = jnp.max(s, axis=-1, keepdims=True)
    e = jnp.exp2(s - mx)
    den = jnp.sum(e, axis=-1, keepdims=True)
    o = jnp.dot(e.reshape(rows, span).astype(BF16), v_ref[0, 0, pl.ds(k0, span), :],
                preferred_element_type=F32)
    o = o / den.reshape(rows, 1)
    for h in range(NSA_HPG):
        o_ref[0, :, h * HEAD_DIM:(h + 1) * HEAD_DIM] = o[h * Q_BLOCK:(h + 1) * Q_BLOCK]


def _win_attn(p, batch, t_len):
    span = min(WINDOW + Q_BLOCK, t_len)
    kern = functools.partial(_win_attn_kernel, span=span)
    return pl.pallas_call(
        kern,
        grid=(batch, NSA_GROUPS, t_len // Q_BLOCK),
        in_specs=[
            pl.BlockSpec((1, NSA_HPG, Q_BLOCK, HEAD_DIM), lambda b, g, c: (b, g, c, 0)),
            pl.BlockSpec((1, 1, t_len, HEAD_DIM), lambda b, g, c: (b, SLOT_KW + g, 0, 0)),
            pl.BlockSpec((1, 1, t_len, HEAD_DIM), lambda b, g, c: (b, SLOT_VW + g, 0, 0)),
        ],
        out_specs=pl.BlockSpec((1, Q_BLOCK, NSA_HPG * HEAD_DIM), lambda b, g, c: (b, c, g)),
        out_shape=jax.ShapeDtypeStruct((batch, t_len, NSA_WIDTH), F32),
        compiler_params=_cparams(("parallel", "parallel", "parallel")),
        name="win_attn",
    )(p, p, p)


def _diff_attn_kernel(q1_ref, q2_ref, k1_ref, k2_ref, v_ref, lam_ref, sw_ref, o_ref,
                      m_ref, l_ref, acc_ref, *, tq, tk, rb):
    start = pl.program_id(2) * tq
    m_ref[...] = jnp.full(m_ref.shape, NEG_INF, F32)
    l_ref[...] = jnp.zeros(l_ref.shape, F32)
    acc_ref[...] = jnp.zeros(acc_ref.shape, F32)
    q_refs = (q1_ref, q2_ref)
    k_refs = (k1_ref, k2_ref)

    def v_tile(k0, n):
        return jnp.concatenate([v_ref[0, 0, pl.ds(k0, n), :], v_ref[0, 1, pl.ds(k0, n), :]], axis=1)

    def body(j, carry):
        k0 = pl.multiple_of(j * tk, tk)
        v = v_tile(k0, tk)
        for mp in range(2):
            kt = k_refs[mp][0, 0, pl.ds(k0, tk), :]
            for r in range(tq // rb):
                rows = pl.ds(r * rb, rb)
                s = lax.dot_general(q_refs[mp][0, 0, rows, :], kt, _NT, preferred_element_type=F32)
                _flash_update(s, v, rows, m_ref.at[mp], l_ref.at[mp], acc_ref.at[mp])
        return carry

    lax.fori_loop(0, start // tk, body, 0)

    tri = _causal_triangle(rb)
    for d in range(tq // rb):
        k0 = pl.multiple_of(start + d * rb, rb)
        v = v_tile(k0, rb)
        for mp in range(2):
            kt = k_refs[mp][0, 0, pl.ds(k0, rb), :]
            for r in range(d, tq // rb):
                rows = pl.ds(r * rb, rb)
                s = lax.dot_general(q_refs[mp][0, 0, rows, :], kt, _NT, preferred_element_type=F32)
                if r == d:
                    s = jnp.where(tri, s, NEG_INF)
                _flash_update(s, v, rows, m_ref.at[mp], l_ref.at[mp], acc_ref.at[mp])

    lv = lam_ref[...]
    lam = (jnp.exp(jnp.sum(lv[0:1] * lv[1:2], axis=-1, keepdims=True))
           - jnp.exp(jnp.sum(lv[2:3] * lv[3:4], axis=-1, keepdims=True)) + LAMBDA_INIT)
    l1 = jnp.sum(l_ref[0], axis=-1, keepdims=True)
    l2 = jnp.sum(l_ref[1], axis=-1, keepdims=True)
    o = acc_ref[0] / l1 - lam * (acc_ref[1] / l2)
    y = o * lax.rsqrt(jnp.mean(o * o, axis=-1, keepdims=True) + SUBLN_EPS)
    o_ref[0] = y * sw_ref[...] * (1.0 - LAMBDA_INIT)


def _diff_attn(p, lam4, subln_w, batch, t_len, tq, tk, rb):
    assert tq % tk == 0 and tk % rb == 0
    kern = functools.partial(_diff_attn_kernel, tq=tq, tk=tk, rb=rb)
    return pl.pallas_call(
        kern,
        grid=(batch, DIFF_HEADS, t_len // tq),
        in_specs=[
            pl.BlockSpec((1, 1, tq, DIFF_QK), lambda b, h, i: (b, SLOT_QB + h, i, 0)),
            pl.BlockSpec((1, 1, tq, DIFF_QK), lambda b, h, i: (b, SLOT_QB + DIFF_HEADS + h, i, 0)),
            _resident((1, 1, t_len, DIFF_QK), lambda b, h, i: (b, SLOT_KB + h, 0, 0)),
            _resident((1, 1, t_len, DIFF_QK), lambda b, h, i: (b, SLOT_KB + DIFF_HEADS + h, 0, 0)),
            _resident((1, 2, t_len, LANE), lambda b, h, i: (b, SLOT_VB // 2 + h, 0, 0)),
            pl.BlockSpec((8, DIFF_QK), lambda b, h, i: (0, 0)),
            pl.BlockSpec((1, DIFF_V), lambda b, h, i: (0, 0)),
        ],
        out_specs=pl.BlockSpec((1, tq, DIFF_V), lambda b, h, i: (b, i, h)),
        out_shape=jax.ShapeDtypeStruct((batch, t_len, DIFF_WIDTH), F32),
        scratch_shapes=[
            pltpu.VMEM((2, tq, LANE), F32),
            pltpu.VMEM((2, tq, LANE), F32),
            pltpu.VMEM((2, tq, DIFF_V), F32),
        ],
        compiler_params=_cparams(("parallel", "parallel", "arbitrary")),
        name="diff_attn",
    )(p, p, p, p, p, lam4, subln_w)


def _out_proj_kernel(x_ref, oc_ref, os_ref, ow_ref, g_ref, za_ref, ob_ref, zb_ref, w_ref, fw_ref,
                     o_ref, mix_ref):
    gates = jax.nn.sigmoid(g_ref[0])
    tm = gates.shape[0]
    for hd in range(NSA_HEADS):
        cols = slice(hd * HEAD_DIM, (hd + 1) * HEAD_DIM)
        g0 = jnp.broadcast_to(gates[:, 3 * hd:3 * hd + 1], (tm, HEAD_DIM))
        g1 = jnp.broadcast_to(gates[:, 3 * hd + 1:3 * hd + 2], (tm, HEAD_DIM))
        g2 = jnp.broadcast_to(gates[:, 3 * hd + 2:3 * hd + 3], (tm, HEAD_DIM))
        o = g0 * oc_ref[0, :, cols] + g1 * os_ref[0, :, cols] + g2 * ow_ref[0, :, cols]
        z = za_ref[0, hd].astype(F32)
        mix_ref[:, cols] = (o * (z * jax.nn.sigmoid(z))).astype(BF16)
    for sl in range(DIFF_WIDTH // LANE):
        cols = slice(sl * LANE, (sl + 1) * LANE)
        z = zb_ref[0, sl].astype(F32)
        mix_ref[:, NSA_WIDTH + sl * LANE:NSA_WIDTH + (sl + 1) * LANE] = (
            ob_ref[0, :, cols] * (z * jax.nn.sigmoid(z))).astype(BF16)
    y = x_ref[0] + jnp.dot(mix_ref[...], w_ref[...], preferred_element_type=F32)
    y = y * lax.rsqrt(jnp.mean(y * y, axis=-1, keepdims=True) + NORM_EPS)
    o_ref[0] = y * fw_ref[...]


def _out_proj(x, o_cmp, o_sel, o_win, gl, p, o_b, w_out, final_w, batch, t_len, tm):
    row = lambda b, i: (b, i, 0)
    return pl.pallas_call(
        _out_proj_kernel,
        grid=(batch, t_len // tm),
        in_specs=[
            pl.BlockSpec((1, tm, D_MODEL), row),
            pl.BlockSpec((1, tm, NSA_WIDTH), row),
            pl.BlockSpec((1, tm, NSA_WIDTH), row),
            pl.BlockSpec((1, tm, NSA_WIDTH), row),
            pl.BlockSpec((1, tm, LANE), row),
            pl.BlockSpec((1, 8, tm, LANE), lambda b, i: (b, SLOT_ZA // 8, i, 0)),
            pl.BlockSpec((1, tm, DIFF_WIDTH), row),
            pl.BlockSpec((1, 8, tm, LANE), lambda b, i: (b, SLOT_ZB // 8, i, 0)),
            pl.BlockSpec((MIX_WIDTH, D_MODEL), lambda b, i: (0, 0)),
            pl.BlockSpec((1, D_MODEL), lambda b, i: (0, 0)),
        ],
        out_specs=pl.BlockSpec((1, tm, D_MODEL), row),
        out_shape=jax.ShapeDtypeStruct((batch, t_len, D_MODEL), F32),
        scratch_shapes=[pltpu.VMEM((tm, MIX_WIDTH), BF16)],
        compiler_params=_cparams(("parallel", "parallel")),
        name="out_proj",
    )(x, o_cmp, o_sel, o_win, gl, p, o_b, p, w_out, final_w)


def _reorder_w_in(w):
    def heads(off, n):
        return [w[:, off + i * LANE: off + (i + 1) * LANE] for i in range(n)]
    cols = (heads(_O_QA, 8) + heads(_O_QB, 8) + heads(_O_KB, 8) + heads(_O_KC, 2) + heads(_O_KS, 2)
            + heads(_O_KW, 2) + heads(_O_VC, 2) + heads(_O_ZA, 8) + heads(_O_ZB, 8) + heads(_O_VB, 8)
            + heads(_O_VS, 2) + heads(_O_VW, 2))
    w_r = jnp.concatenate(cols, axis=1).astype(BF16)
    w_g = jnp.pad(w[:, _O_G:_O_G + NSA_HEADS * NSA_BRANCHES],
                  ((0, 0), (0, LANE - NSA_HEADS * NSA_BRANCHES))).astype(BF16)
    return w_r, w_g


def _sel_map(t_len, n_selp):
    n_rows = t_len // CMP_STRIDE
    c0 = np.arange(n_rows)[:, None] * CMP_STRIDE
    s0 = np.arange(n_selp)[None, :] * SEL_BLOCK
    ov = np.minimum(c0 + CMP_BLOCK, s0 + SEL_BLOCK) - np.maximum(c0, s0)
    m = np.maximum(ov, 0) / CMP_BLOCK
    m[n_rows - 1:] = 0.0
    m[:, t_len // SEL_BLOCK:] = 0.0
    return jnp.asarray(m.T, dtype=BF16)


def _block_one_hot(t_len):
    blk = (np.arange(t_len) // SEL_BLOCK) % LANE
    return jnp.asarray(blk[:, None] == np.arange(LANE)[None, :], dtype=BF16)


def kernel(x, norm_w, w_in, cmp_pos, cmp_k_w1, cmp_k_w2, cmp_v_w1, cmp_v_w2, lam_q1, lam_k1, lam_q2, lam_k2,
           subln_w, w_out, final_norm_w):
    batch, t_len, _ = x.shape
    n_rows = t_len // CMP_STRIDE
    n_selp = -(-(t_len // SEL_BLOCK) // LANE) * LANE

    pos = jnp.arange(t_len, dtype=F32)
    inv = ROPE_THETA ** (-jnp.arange(0, HEAD_DIM, 2, dtype=F32) / HEAD_DIM)
    ang = pos[:, None] * inv[None, :]
    cos, sin = jnp.cos(ang), jnp.sin(ang)
    cos_f = jnp.concatenate([cos, cos], axis=1)
    sin_f = jnp.concatenate([-sin, sin], axis=1)

    w_r, w_g = _reorder_w_in(w_in[0])
    tm_in = min(1024, t_len)
    p, gl = _in_proj(x.reshape(batch * t_len, D_MODEL), norm_w[0][None, :], w_r, w_g, cos_f, sin_f,
                     batch, t_len, tm_in)

    half = CMP_STRIDE * HEAD_DIM
    w1cat = jnp.stack([jnp.concatenate([w[:half], w[half:]], axis=1) for w in (cmp_k_w1[0], cmp_v_w1[0])]).astype(BF16)
    w2 = jnp.stack([cmp_k_w2[0], cmp_v_w2[0]]).astype(BF16)
    pos2 = jnp.broadcast_to(cmp_pos[0].reshape(2, 1, half), (2, 8, half)).astype(BF16)
    pc = jnp.concatenate([p[:, SLOT_KC:SLOT_KC + NSA_GROUPS], p[:, SLOT_VC:SLOT_VC + NSA_GROUPS]], axis=1)
    kvc = _compress(pc.reshape(batch, 2 * NSA_GROUPS, n_rows, half), w1cat, pos2, w2, batch, n_rows)

    o_cmp, bias = _cmp_select(p, kvc, _sel_map(t_len, n_selp), batch, t_len)
    o_sel = _sel_attn(p, bias, _block_one_hot(t_len), batch, t_len, tq=min(2048, t_len), tk=512, rb=256)
    o_win = _win_attn(p, batch, t_len)

    lam4 = jnp.pad(jnp.stack([lam_q1[0], lam_k1[0], lam_q2[0], lam_k2[0]]), ((0, 4), (0, 0)))
    o_b = _diff_attn(p, lam4, subln_w[0][None, :], batch, t_len, tq=min(2048, t_len), tk=512, rb=256)

    return _out_proj(x, o_cmp, o_sel, o_win, gl, p, o_b, w_out[0].astype(BF16), final_norm_w[None, :],
                     batch, t_len, tm=min(512, t_len))
```

```python
import functools
import math

import numpy as np
import jax
import jax.numpy as jnp
from jax import lax
from jax.experimental import pallas as pl
from jax.experimental.pallas import tpu as pltpu

F32 = jnp.float32
BF16 = jnp.bfloat16

D_MODEL = 2048
HEAD_DIM = 128
NSA_HEADS = 8
NSA_GROUPS = 2
NSA_HPG = NSA_HEADS // NSA_GROUPS
NSA_WIDTH = NSA_HEADS * HEAD_DIM
NSA_KV = NSA_GROUPS * HEAD_DIM
NSA_BRANCHES = 3
CMP_BLOCK = 32
CMP_STRIDE = 16
CMP_HIDDEN = 512
SEL_BLOCK = 64
N_SELECT = 16
WINDOW = 512
DIFF_HEADS = 4
DIFF_QK = 128
DIFF_V = 256
DIFF_WIDTH = DIFF_HEADS * DIFF_V
MIX_WIDTH = NSA_WIDTH + DIFF_WIDTH
Q_BLOCK = 128
ROPE_THETA = 10000.0
NORM_EPS = 1e-6
SUBLN_EPS = 1e-5
NEG_INF = -1e30
FORCE_SCORE = 1e9
LAMBDA_INIT = 0.8 - 0.6 * math.exp(-0.3 * 0)

LANE = 128
VMEM_LIMIT = 56 * 1024 * 1024

SLOT_QA = 0
SLOT_QB = 8
SLOT_KB = 16
SLOT_KC = 24
SLOT_KS = 26
SLOT_KW = 28
N_ROPE_SLOTS = 30
SLOT_VC = 30
SLOT_ZA = 32
SLOT_ZB = 40
SLOT_VB = 48
SLOT_VS = 56
SLOT_VW = 58
N_SLOTS = 60
SLOTS_PER_STEP = 6
N_QUERY_SLOTS = 16
QK_EXP2_SCALE = (HEAD_DIM ** -0.5) * math.log2(math.e)

_OFF = np.concatenate([[0], np.cumsum([NSA_WIDTH, NSA_KV, NSA_KV, NSA_KV, NSA_KV, NSA_KV, NSA_KV,
                                       NSA_HEADS * NSA_BRANCHES, NSA_WIDTH, 1024, 1024, DIFF_WIDTH, DIFF_WIDTH])])
(_O_QA, _O_KC, _O_VC, _O_KS, _O_VS, _O_KW, _O_VW, _O_G, _O_ZA, _O_QB, _O_KB, _O_VB, _O_ZB, _O_END) = [int(v) for v in _OFF]


def _cparams(sem):
    return pltpu.CompilerParams(dimension_semantics=sem, vmem_limit_bytes=VMEM_LIMIT)


def _in_proj_kernel(x_ref, nw_ref, w_ref, wg_ref, cos_ref, sin_ref, p_ref, g_ref, h_ref, *, n_rope_steps):
    j = pl.program_id(1)

    @pl.when(j == 0)
    def _():
        x = x_ref[...]
        y = x * lax.rsqrt(jnp.mean(x * x, axis=-1, keepdims=True) + NORM_EPS)
        h = (y * nw_ref[...]).astype(BF16)
        h_ref[...] = h
        g_ref[0] = jnp.dot(h, wg_ref[...], preferred_element_type=F32)

    acc = jnp.dot(h_ref[...], w_ref[...], preferred_element_type=F32)

    rope = j < n_rope_steps
    cos = jnp.where(rope, cos_ref[...], 1.0)
    sin = jnp.where(rope, sin_ref[...], 0.0)
    for s in range(SLOTS_PER_STEP):
        a = acc[:, s * LANE:(s + 1) * LANE]
        r = a * cos + pltpu.roll(a, HEAD_DIM // 2, 1) * sin
        f = jnp.where(j * SLOTS_PER_STEP + s < N_QUERY_SLOTS, QK_EXP2_SCALE, 1.0)
        p_ref[0, s] = (r * f).astype(BF16)


def _in_proj(x2, norm_w, w_r, w_g, cos_f, sin_f, batch, t_len, tm):
    m = x2.shape[0]
    tiles_per_seq = t_len // tm
    n_steps = N_SLOTS // SLOTS_PER_STEP
    tn = SLOTS_PER_STEP * LANE
    kern = functools.partial(_in_proj_kernel, n_rope_steps=N_ROPE_SLOTS // SLOTS_PER_STEP)
    return pl.pallas_call(
        kern,
        grid=(m // tm, n_steps),
        in_specs=[
            pl.BlockSpec((tm, D_MODEL), lambda i, j: (i, 0)),
            pl.BlockSpec((1, D_MODEL), lambda i, j: (0, 0)),
            pl.BlockSpec((D_MODEL, tn), lambda i, j: (0, j)),
            pl.BlockSpec((D_MODEL, LANE), lambda i, j: (0, 0)),
            pl.BlockSpec((tm, LANE), lambda i, j: (i % tiles_per_seq, 0)),
            pl.BlockSpec((tm, LANE), lambda i, j: (i % tiles_per_seq, 0)),
        ],
        out_specs=[
            pl.BlockSpec((1, SLOTS_PER_STEP, tm, LANE), lambda i, j: (i // tiles_per_seq, j, i % tiles_per_seq, 0)),
            pl.BlockSpec((1, tm, LANE), lambda i, j: (i // tiles_per_seq, i % tiles_per_seq, 0)),
        ],
        out_shape=[
            jax.ShapeDtypeStruct((batch, N_SLOTS, t_len, LANE), BF16),
            jax.ShapeDtypeStruct((batch, t_len, LANE), F32),
        ],
        scratch_shapes=[pltpu.VMEM((tm, D_MODEL), BF16)],
        compiler_params=_cparams(("parallel", "arbitrary")),
        name="in_proj",
    )(x2, norm_w, w_r, w_g, cos_f, sin_f)


def _compress_kernel(r_ref, w1_ref, pos_ref, w2_ref, o_ref, *, n_rows):
    w1 = w1_ref[0]
    a = jnp.dot(r_ref[0, 0], w1, preferred_element_type=F32)
    c = (jnp.dot(pos_ref[0], w1[:, :CMP_HIDDEN], preferred_element_type=F32)
         + jnp.dot(pos_ref[1], w1[:, CMP_HIDDEN:], preferred_element_type=F32))[0:1]
    nxt = pltpu.roll(a[:, CMP_HIDDEN:], n_rows - 1, 0)
    hid = a[:, :CMP_HIDDEN] + nxt + c
    hid = hid * jax.nn.sigmoid(hid)
    out = jnp.dot(hid.astype(BF16), w2_ref[0], preferred_element_type=F32)
    row = lax.broadcasted_iota(jnp.int32, out.shape, 0)
    o_ref[0, 0, 0] = jnp.where(row < n_rows - 1, out, 0.0).astype(BF16)


def _compress(p16, w1cat, pos2, w2, batch, n_rows):
    kern = functools.partial(_compress_kernel, n_rows=n_rows)

    return pl.pallas_call(
        kern,
        grid=(batch, 2, NSA_GROUPS),
        in_specs=[
            pl.BlockSpec((1, 1, n_rows, CMP_STRIDE * HEAD_DIM), lambda b, kv, g: (b, kv * NSA_GROUPS + g, 0, 0)),
            pl.BlockSpec((1, CMP_STRIDE * HEAD_DIM, 2 * CMP_HIDDEN), lambda b, kv, g: (kv, 0, 0)),
            pl.BlockSpec((2, 8, CMP_STRIDE * HEAD_DIM), lambda b, kv, g: (0, 0, 0)),
            pl.BlockSpec((1, CMP_HIDDEN, HEAD_DIM), lambda b, kv, g: (kv, 0, 0)),
        ],
        out_specs=pl.BlockSpec((1, 1, 1, n_rows, HEAD_DIM), lambda b, kv, g: (b, kv, g, 0, 0)),
        out_shape=jax.ShapeDtypeStruct((batch, 2, NSA_GROUPS, n_rows, HEAD_DIM), BF16),
        compiler_params=_cparams(("parallel", "parallel", "parallel")),
        name="compress",
    )(p16, w1cat, pos2, w2)


_NT = (((1,), (1,)), ((), ()))
CMP_PER_SEL = SEL_BLOCK // CMP_STRIDE


def _cmp_select_body(q_ref, kc_ref, vc_ref, map_ref, o_ref, bias_ref, start, *, n_top, w, bw):
    n_selp = map_ref.shape[0]
    q = q_ref[0].reshape(NSA_HPG * Q_BLOCK, HEAD_DIM)
    s = lax.dot_general(q, kc_ref[0, 0, 0, :w, :], _NT, preferred_element_type=F32)
    s = s.reshape(NSA_HPG, Q_BLOCK, w)
    t_q = start + lax.broadcasted_iota(jnp.int32, (1, Q_BLOCK, w), 1)
    n_id = lax.broadcasted_iota(jnp.int32, (1, Q_BLOCK, w), 2)
    valid = n_id * CMP_STRIDE + (CMP_BLOCK - 1) <= t_q
    s = jnp.where(valid, s, NEG_INF)
    e = jnp.exp2(s - jnp.max(s, axis=-1, keepdims=True))
    den = jnp.sum(e, axis=-1, keepdims=True)
    sees_any = start + lax.broadcasted_iota(jnp.int32, (1, Q_BLOCK, 1), 1) >= CMP_BLOCK - 1
    p = e * jnp.where(sees_any, 1.0 / den, 0.0)
    o = jnp.dot(p.reshape(NSA_HPG * Q_BLOCK, w).astype(BF16), vc_ref[0, 0, 0, :w, :],
                preferred_element_type=F32)
    for h in range(NSA_HPG):
        o_ref[0, :, h * HEAD_DIM:(h + 1) * HEAD_DIM] = o[h * Q_BLOCK:(h + 1) * Q_BLOCK]

    psum = p[0] + p[1] + p[2] + p[3]
    p_hi = psum.astype(BF16)
    p_lo = (psum - p_hi.astype(F32)).astype(BF16)
    smap_t = map_ref[:bw, :w]
    imp = (lax.dot_general(smap_t, p_hi, _NT, preferred_element_type=F32)
           + lax.dot_general(smap_t, p_lo, _NT, preferred_element_type=F32))
    blk = lax.broadcasted_iota(jnp.int32, (bw, Q_BLOCK), 0)
    cur = (start + lax.broadcasted_iota(jnp.int32, (bw, Q_BLOCK), 1)) // SEL_BLOCK
    forced = (blk == 0) | (blk == cur) | (blk == cur - 1)
    blk_f = blk.astype(F32)
    work = jnp.where(forced, -jnp.inf, jnp.where(blk > cur, NEG_INF, imp))
    for _ in range(n_top - 3):
        top = jnp.max(work, axis=0, keepdims=True)
        first = jnp.min(jnp.where(work == top, blk_f, 1e9), axis=0, keepdims=True)
        work = jnp.where(blk_f == first, -jnp.inf, work)
    keep = (work == -jnp.inf) & (blk <= cur)
    bias_t = jnp.where(keep, 0.0, NEG_INF)
    if bw < n_selp:
        bias_t = jnp.concatenate([bias_t, jnp.full((n_selp - bw, Q_BLOCK), NEG_INF, F32)], axis=0)
    bias_ref[0, 0] = bias_t.T.astype(BF16)


def _cmp_select_kernel(q_ref, kc_ref, vc_ref, map_ref, o_ref, bias_ref, *, n_rows, n_top, chunk):
    start = pl.program_id(2) * Q_BLOCK
    variant = ((start + Q_BLOCK - 1) // SEL_BLOCK) // (chunk // CMP_PER_SEL)
    for v in range(n_rows // chunk):
        w = (v + 1) * chunk
        pl.when(variant == v)(functools.partial(
            _cmp_select_body, q_ref, kc_ref, vc_ref, map_ref, o_ref, bias_ref, start,
            n_top=n_top, w=w, bw=min(w // CMP_PER_SEL, map_ref.shape[0])))


def _cmp_select(p, kvc, sel_map, batch, t_len):
    n_rows = t_len // CMP_STRIDE
    n_selp = sel_map.shape[0]
    n_top = min(N_SELECT, t_len // SEL_BLOCK)
    assert n_top >= 3
    chunk = min(256, n_rows)
    assert n_rows % chunk == 0 and (chunk // CMP_PER_SEL) % 8 == 0
    kern = functools.partial(_cmp_select_kernel, n_rows=n_rows, n_top=n_top, chunk=chunk)
    return pl.pallas_call(
        kern,
        grid=(batch, NSA_GROUPS, t_len // Q_BLOCK),
        in_specs=[
            pl.BlockSpec((1, NSA_HPG, Q_BLOCK, HEAD_DIM), lambda b, g, c: (b, g, c, 0)),
            pl.BlockSpec((1, 1, 1, n_rows, HEAD_DIM), lambda b, g, c: (b, 0, g, 0, 0)),
            pl.BlockSpec((1, 1, 1, n_rows, HEAD_DIM), lambda b, g, c: (b, 1, g, 0, 0)),
            pl.BlockSpec((n_selp, n_rows), lambda b, g, c: (0, 0)),
        ],
        out_specs=[
            pl.BlockSpec((1, Q_BLOCK, NSA_HPG * HEAD_DIM), lambda b, g, c: (b, c, g)),
            pl.BlockSpec((1, 1, Q_BLOCK, n_selp), lambda b, g, c: (b, g, c, 0)),
        ],
        out_shape=[
            jax.ShapeDtypeStruct((batch, t_len, NSA_WIDTH), F32),
            jax.ShapeDtypeStruct((batch, NSA_GROUPS, t_len, n_selp), BF16),
        ],
        compiler_params=_cparams(("parallel", "parallel", "parallel")),
        name="cmp_select",
    )(p, kvc, kvc, sel_map)


def _lane_tile(x, n):
    return x if n == 1 else jnp.concatenate([x] * n, axis=1)


def _flash_update(s, v, rows, m_ref, l_ref, acc_ref):
    n_keys = s.shape[1]
    dv = acc_ref.shape[-1]
    m_prev = m_ref[rows, :]
    m_new = jnp.maximum(m_prev, jnp.max(s, axis=-1, keepdims=True))
    alpha = jnp.exp2(m_prev - m_new)
    p = jnp.exp2(s - _lane_tile(m_new, n_keys // LANE))
    if l_ref is not None:
        psum = p[:, :LANE]
        for cc in range(1, n_keys // LANE):
            psum = psum + p[:, cc * LANE:(cc + 1) * LANE]
        l_ref[rows, :] = alpha * l_ref[rows, :] + psum
    acc_ref[rows, :] = (_lane_tile(alpha, dv // LANE) * acc_ref[rows, :]
                        + jnp.dot(p.astype(BF16), v, preferred_element_type=F32))
    m_ref[rows, :] = m_new


def _causal_triangle(rb):
    return lax.broadcasted_iota(jnp.int32, (rb, rb), 1) <= lax.broadcasted_iota(jnp.int32, (rb, rb), 0)


def _sel_attn_kernel(q_ref, bias_ref, k_ref, v_ref, hot_ref, o_ref, qa_ref, m_ref, acc_ref,
                     *, tq, tk, rb, n_halves):
    start = pl.program_id(2) * tq
    n_rows = NSA_HPG * tq
    keys_per_half = LANE * SEL_BLOCK
    for hf in range(n_halves):
        for h in range(NSA_HPG):
            qa_ref[hf, h * tq:(h + 1) * tq, :LANE] = q_ref[0, h]
            qa_ref[hf, h * tq:(h + 1) * tq, LANE:] = bias_ref[0, 0, :, hf * LANE:(hf + 1) * LANE]
    m_ref[...] = jnp.full(m_ref.shape, NEG_INF, F32)
    acc_ref[...] = jnp.zeros(acc_ref.shape, F32)

    def k_aug(k0, n):
        return jnp.concatenate([k_ref[0, 0, pl.ds(k0, n), :], hot_ref[pl.ds(k0, n), :]], axis=1)

    def v_aug(k0, n):
        return jnp.concatenate([v_ref[0, 0, pl.ds(k0, n), :], jnp.ones((n, LANE), BF16)], axis=1)

    def body(j, carry):
        k0 = pl.multiple_of(j * tk, tk)
        ka, v, hf = k_aug(k0, tk), v_aug(k0, tk), k0 // keys_per_half
        for r in range(n_rows // rb):
            rows = pl.ds(r * rb, rb)
            s = lax.dot_general(qa_ref[hf, rows, :], ka, _NT, preferred_element_type=F32)
            _flash_update(s, v, rows, m_ref, None, acc_ref)
        return carry

    lax.fori_loop(0, start // tk, body, 0)

    tri = _causal_triangle(rb)
    for d in range(tq // rb):
        k0 = pl.multiple_of(start + d * rb, rb)
        ka, v, hf = k_aug(k0, rb), v_aug(k0, rb), k0 // keys_per_half
        for r in range(n_rows // rb):
            q_off = (r * rb) % tq
            if d * rb > q_off:
                continue
            rows = pl.ds(r * rb, rb)
            s = lax.dot_general(qa_ref[hf, rows, :], ka, _NT, preferred_element_type=F32)
            if d * rb == q_off:
                s = jnp.where(tri, s, NEG_INF)
            _flash_update(s, v, rows, m_ref, None, acc_ref)

    for h in range(NSA_HPG):
        hr = slice(h * tq, (h + 1) * tq)
        o_ref[0, :, h * HEAD_DIM:(h + 1) * HEAD_DIM] = acc_ref[hr, :HEAD_DIM] / acc_ref[hr, HEAD_DIM:]


def _resident(block_shape, index_map):
    return pl.BlockSpec(block_shape, index_map, pipeline_mode=pl.Buffered(1))


def _sel_attn(p, bias, hot, batch, t_len, tq, tk, rb):
    assert tq % tk == 0 and tk % rb == 0 and (LANE * SEL_BLOCK) % tk == 0
    n_selp = bias.shape[-1]
    n_halves = n_selp // LANE
    n_rows = NSA_HPG * tq
    kern = functools.partial(_sel_attn_kernel, tq=tq, tk=tk, rb=rb, n_halves=n_halves)
    return pl.pallas_call(
        kern,
        grid=(batch, NSA_GROUPS, t_len // tq),
        in_specs=[
            pl.BlockSpec((1, NSA_HPG, tq, HEAD_DIM), lambda b, g, c: (b, g, c, 0)),
            pl.BlockSpec((1, 1, tq, n_selp), lambda b, g, c: (b, g, c, 0)),
            _resident((1, 1, t_len, HEAD_DIM), lambda b, g, c: (b, SLOT_KS + g, 0, 0)),
            _resident((1, 1, t_len, HEAD_DIM), lambda b, g, c: (b, SLOT_VS + g, 0, 0)),
            _resident((t_len, LANE), lambda b, g, c: (0, 0)),
        ],
        out_specs=pl.BlockSpec((1, tq, NSA_HPG * HEAD_DIM), lambda b, g, c: (b, c, g)),
        out_shape=jax.ShapeDtypeStruct((batch, t_len, NSA_WIDTH), F32),
        scratch_shapes=[
            pltpu.VMEM((n_halves, n_rows, 2 * LANE), BF16),
            pltpu.VMEM((n_rows, LANE), F32),
            pltpu.VMEM((n_rows, 2 * HEAD_DIM), F32),
        ],
        compiler_params=_cparams(("parallel", "parallel", "arbitrary")),
        name="sel_attn",
    )(p, bias, p, p, hot)


def _win_attn_kernel(q_ref, k_ref, v_ref, o_ref, *, tq, span):
    start = pl.program_id(2) * tq
    rows = NSA_HPG * Q_BLOCK
    for qc in range(tq // Q_BLOCK):
        q_rows = slice(qc * Q_BLOCK, (qc + 1) * Q_BLOCK)
        q0 = start + qc * Q_BLOCK
        k0 = pl.multiple_of(jnp.maximum(q0 + Q_BLOCK - span, 0), Q_BLOCK)
        q = q_ref[0, :, q_rows, :].reshape(rows, HEAD_DIM)
        s = lax.dot_general(q, k_ref[0, 0, pl.ds(k0, span), :], _NT, preferred_element_type=F32)
        s = s.reshape(NSA_HPG, Q_BLOCK, span)
        t_q = q0 + lax.broadcasted_iota(jnp.int32, (1, Q_BLOCK, span), 1)
        k_pos = k0 + lax.broadcasted_iota(jnp.int32, (1, Q_BLOCK, span), 2)
        valid = (k_pos <= t_q) & (k_pos > t_q - WINDOW)
        s = jnp.where(valid, s, NEG_INF)
        e = jnp.exp2(s - jnp.max(s, axis=-1, keepdims=True))
        v = jnp.concatenate([v_ref[0, 0, pl.ds(k0, span), :], jnp.ones((span, LANE), BF16)], axis=1)
        o = jnp.dot(e.reshape(rows, span).astype(BF16), v, preferred_element_type=F32)
        o = o[:, :HEAD_DIM] / o[:, HEAD_DIM:]
        for h in range(NSA_HPG):
            o_ref[0, q_rows, h * HEAD_DIM:(h + 1) * HEAD_DIM] = o[h * Q_BLOCK:(h + 1) * Q_BLOCK]


def _win_attn(p, batch, t_len, tq):
    span = min(WINDOW + Q_BLOCK, t_len)
    kern = functools.partial(_win_attn_kernel, tq=tq, span=span)
    return pl.pallas_call(
        kern,
        grid=(batch, NSA_GROUPS, t_len // tq),
        in_specs=[
            pl.BlockSpec((1, NSA_HPG, tq, HEAD_DIM), lambda b, g, c: (b, g, c, 0)),
            _resident((1, 1, t_len, HEAD_DIM), lambda b, g, c: (b, SLOT_KW + g, 0, 0)),
            _resident((1, 1, t_len, HEAD_DIM), lambda b, g, c: (b, SLOT_VW + g, 0, 0)),
        ],
        out_specs=pl.BlockSpec((1, tq, NSA_HPG * HEAD_DIM), lambda b, g, c: (b, c, g)),
        out_shape=jax.ShapeDtypeStruct((batch, t_len, NSA_WIDTH), F32),
        compiler_params=_cparams(("parallel", "parallel", "parallel")),
        name="win_attn",
    )(p, p, p)


def _diff_attn_kernel(q1_ref, q2_ref, k1_ref, k2_ref, v_ref, lam_ref, sw_ref, o_ref,
                      m_ref, l_ref, acc_ref, *, tq, tk, rb):
    start = pl.program_id(2) * tq
    m_ref[...] = jnp.full(m_ref.shape, NEG_INF, F32)
    l_ref[...] = jnp.zeros(l_ref.shape, F32)
    acc_ref[...] = jnp.zeros(acc_ref.shape, F32)
    q_refs = (q1_ref, q2_ref)
    k_refs = (k1_ref, k2_ref)

    def v_tile(k0, n):
        return jnp.concatenate([v_ref[0, 0, pl.ds(k0, n), :], v_ref[0, 1, pl.ds(k0, n), :]], axis=1)

    def body(j, carry):
        k0 = pl.multiple_of(j * tk, tk)
        v = v_tile(k0, tk)
        for mp in range(2):
            kt = k_refs[mp][0, 0, pl.ds(k0, tk), :]
            for r in range(tq // rb):
                rows = pl.ds(r * rb, rb)
                s = lax.dot_general(q_refs[mp][0, 0, rows, :], kt, _NT, preferred_element_type=F32)
                _flash_update(s, v, rows, m_ref.at[mp], l_ref.at[mp], acc_ref.at[mp])
        return carry

    lax.fori_loop(0, start // tk, body, 0)

    tri = _causal_triangle(rb)
    for d in range(tq // rb):
        k0 = pl.multiple_of(start + d * rb, rb)
        v = v_tile(k0, rb)
        for mp in range(2):
            kt = k_refs[mp][0, 0, pl.ds(k0, rb), :]
            for r in range(d, tq // rb):
                rows = pl.ds(r * rb, rb)
                s = lax.dot_general(q_refs[mp][0, 0, rows, :], kt, _NT, preferred_element_type=F32)
                if r == d:
                    s = jnp.where(tri, s, NEG_INF)
                _flash_update(s, v, rows, m_ref.at[mp], l_ref.at[mp], acc_ref.at[mp])

    lv = lam_ref[...]
    lam = (jnp.exp(jnp.sum(lv[0:1] * lv[1:2], axis=-1, keepdims=True))
           - jnp.exp(jnp.sum(lv[2:3] * lv[3:4], axis=-1, keepdims=True)) + LAMBDA_INIT)
    l1 = jnp.sum(l_ref[0], axis=-1, keepdims=True)
    l2 = jnp.sum(l_ref[1], axis=-1, keepdims=True)
    o = acc_ref[0] / l1 - lam * (acc_ref[1] / l2)
    y = o * lax.rsqrt(jnp.mean(o * o, axis=-1, keepdims=True) + SUBLN_EPS)
    o_ref[0] = y * sw_ref[...] * (1.0 - LAMBDA_INIT)


def _diff_attn(p, lam4, subln_w, batch, t_len, tq, tk, rb):
    assert tq % tk == 0 and tk % rb == 0
    kern = functools.partial(_diff_attn_kernel, tq=tq, tk=tk, rb=rb)
    return pl.pallas_call(
        kern,
        grid=(batch, DIFF_HEADS, t_len // tq),
        in_specs=[
            pl.BlockSpec((1, 1, tq, DIFF_QK), lambda b, h, i: (b, SLOT_QB + h, i, 0)),
            pl.BlockSpec((1, 1, tq, DIFF_QK), lambda b, h, i: (b, SLOT_QB + DIFF_HEADS + h, i, 0)),
            _resident((1, 1, t_len, DIFF_QK), lambda b, h, i: (b, SLOT_KB + h, 0, 0)),
            _resident((1, 1, t_len, DIFF_QK), lambda b, h, i: (b, SLOT_KB + DIFF_HEADS + h, 0, 0)),
            _resident((1, 2, t_len, LANE), lambda b, h, i: (b, SLOT_VB // 2 + h, 0, 0)),
            pl.BlockSpec((8, DIFF_QK), lambda b, h, i: (0, 0)),
            pl.BlockSpec((1, DIFF_V), lambda b, h, i: (0, 0)),
        ],
        out_specs=pl.BlockSpec((1, tq, DIFF_V), lambda b, h, i: (b, i, h)),
        out_shape=jax.ShapeDtypeStruct((batch, t_len, DIFF_WIDTH), F32),
        scratch_shapes=[
            pltpu.VMEM((2, tq, LANE), F32),
            pltpu.VMEM((2, tq, LANE), F32),
            pltpu.VMEM((2, tq, DIFF_V), F32),
        ],
        compiler_params=_cparams(("parallel", "parallel", "arbitrary")),
        name="diff_attn",
    )(p, p, p, p, p, lam4, subln_w)


def _out_proj_kernel(x_ref, oc_ref, os_ref, ow_ref, g_ref, za_ref, ob_ref, zb_ref, w_ref, fw_ref,
                     o_ref, mix_ref):
    gates = jax.nn.sigmoid(g_ref[0])
    tm = gates.shape[0]
    for hd in range(NSA_HEADS):
        cols = slice(hd * HEAD_DIM, (hd + 1) * HEAD_DIM)
        g0 = jnp.broadcast_to(gates[:, 3 * hd:3 * hd + 1], (tm, HEAD_DIM))
        g1 = jnp.broadcast_to(gates[:, 3 * hd + 1:3 * hd + 2], (tm, HEAD_DIM))
        g2 = jnp.broadcast_to(gates[:, 3 * hd + 2:3 * hd + 3], (tm, HEAD_DIM))
        o = g0 * oc_ref[0, :, cols] + g1 * os_ref[0, :, cols] + g2 * ow_ref[0, :, cols]
        z = za_ref[0, hd].astype(F32)
        mix_ref[:, cols] = (o * (z * jax.nn.sigmoid(z))).astype(BF16)
    for sl in range(DIFF_WIDTH // LANE):
        cols = slice(sl * LANE, (sl + 1) * LANE)
        z = zb_ref[0, sl].astype(F32)
        mix_ref[:, NSA_WIDTH + sl * LANE:NSA_WIDTH + (sl + 1) * LANE] = (
            ob_ref[0, :, cols] * (z * jax.nn.sigmoid(z))).astype(BF16)
    y = x_ref[0] + jnp.dot(mix_ref[...], w_ref[...], preferred_element_type=F32)
    y = y * lax.rsqrt(jnp.mean(y * y, axis=-1, keepdims=True) + NORM_EPS)
    o_ref[0] = y * fw_ref[...]


def _out_proj(x, o_cmp, o_sel, o_win, gl, p, o_b, w_out, final_w, batch, t_len, tm):
    row = lambda b, i: (b, i, 0)
    return pl.pallas_call(
        _out_proj_kernel,
        grid=(batch, t_len // tm),
        in_specs=[
            pl.BlockSpec((1, tm, D_MODEL), row),
            pl.BlockSpec((1, tm, NSA_WIDTH), row),
            pl.BlockSpec((1, tm, NSA_WIDTH), row),
            pl.BlockSpec((1, tm, NSA_WIDTH), row),
            pl.BlockSpec((1, tm, LANE), row),
            pl.BlockSpec((1, 8, tm, LANE), lambda b, i: (b, SLOT_ZA // 8, i, 0)),
            pl.BlockSpec((1, tm, DIFF_WIDTH), row),
            pl.BlockSpec((1, 8, tm, LANE), lambda b, i: (b, SLOT_ZB // 8, i, 0)),
            pl.BlockSpec((MIX_WIDTH, D_MODEL), lambda b, i: (0, 0)),
            pl.BlockSpec((1, D_MODEL), lambda b, i: (0, 0)),
        ],
        out_specs=pl.BlockSpec((1, tm, D_MODEL), row),
        out_shape=jax.ShapeDtypeStruct((batch, t_len, D_MODEL), F32),
        scratch_shapes=[pltpu.VMEM((tm, MIX_WIDTH), BF16)],
        compiler_params=_cparams(("parallel", "parallel")),
        name="out_proj",
    )(x, o_cmp, o_sel, o_win, gl, p, o_b, p, w_out, final_w)


def _reorder_w_in(w):
    def heads(off, n):
        return [w[:, off + i * LANE: off + (i + 1) * LANE] for i in range(n)]
    cols = (heads(_O_QA, 8) + heads(_O_QB, 8) + heads(_O_KB, 8) + heads(_O_KC, 2) + heads(_O_KS, 2)
            + heads(_O_KW, 2) + heads(_O_VC, 2) + heads(_O_ZA, 8) + heads(_O_ZB, 8) + heads(_O_VB, 8)
            + heads(_O_VS, 2) + heads(_O_VW, 2))
    w_r = jnp.concatenate(cols, axis=1).astype(BF16)
    w_g = jnp.pad(w[:, _O_G:_O_G + NSA_HEADS * NSA_BRANCHES],
                  ((0, 0), (0, LANE - NSA_HEADS * NSA_BRANCHES))).astype(BF16)
    return w_r, w_g


def _sel_map(t_len, n_selp):
    n_rows = t_len // CMP_STRIDE
    c0 = np.arange(n_rows)[:, None] * CMP_STRIDE
    s0 = np.arange(n_selp)[None, :] * SEL_BLOCK
    ov = np.minimum(c0 + CMP_BLOCK, s0 + SEL_BLOCK) - np.maximum(c0, s0)
    m = np.maximum(ov, 0) / CMP_BLOCK
    m[n_rows - 1:] = 0.0
    m[:, t_len // SEL_BLOCK:] = 0.0
    return jnp.asarray(m.T, dtype=BF16)


def _block_one_hot(t_len):
    blk = (np.arange(t_len) // SEL_BLOCK) % LANE
    return jnp.asarray(blk[:, None] == np.arange(LANE)[None, :], dtype=BF16)


def kernel(x, norm_w, w_in, cmp_pos, cmp_k_w1, cmp_k_w2, cmp_v_w1, cmp_v_w2, lam_q1, lam_k1, lam_q2, lam_k2,
           subln_w, w_out, final_norm_w):
    batch, t_len, _ = x.shape
    n_rows = t_len // CMP_STRIDE
    n_selp = -(-(t_len // SEL_BLOCK) // LANE) * LANE

    pos = jnp.arange(t_len, dtype=F32)
    inv = ROPE_THETA ** (-jnp.arange(0, HEAD_DIM, 2, dtype=F32) / HEAD_DIM)
    ang = pos[:, None] * inv[None, :]
    cos, sin = jnp.cos(ang), jnp.sin(ang)
    cos_f = jnp.concatenate([cos, cos], axis=1)
    sin_f = jnp.concatenate([-sin, sin], axis=1)

    w_r, w_g = _reorder_w_in(w_in[0])
    tm_in = min(1024, t_len)
    p, gl = _in_proj(x.reshape(batch * t_len, D_MODEL), norm_w[0][None, :], w_r, w_g, cos_f, sin_f,
                     batch, t_len, tm_in)

    half = CMP_STRIDE * HEAD_DIM
    w1cat = jnp.stack([jnp.concatenate([w[:half], w[half:]], axis=1) for w in (cmp_k_w1[0], cmp_v_w1[0])]).astype(BF16)
    w2 = jnp.stack([cmp_k_w2[0], cmp_v_w2[0]]).astype(BF16)
    pos2 = jnp.broadcast_to(cmp_pos[0].reshape(2, 1, half), (2, 8, half)).astype(BF16)
    pc = jnp.concatenate([p[:, SLOT_KC:SLOT_KC + NSA_GROUPS], p[:, SLOT_VC:SLOT_VC + NSA_GROUPS]], axis=1)
    kvc = _compress(pc.reshape(batch, 2 * NSA_GROUPS, n_rows, half), w1cat, pos2, w2, batch, n_rows)

    o_cmp, bias = _cmp_select(p, kvc, _sel_map(t_len, n_selp), batch, t_len)
    o_sel = _sel_attn(p, bias, _block_one_hot(t_len), batch, t_len, tq=min(2048, t_len), tk=512, rb=256)
    o_win = _win_attn(p, batch, t_len, tq=512)

    lam4 = jnp.pad(jnp.stack([lam_q1[0], lam_k1[0], lam_q2[0], lam_k2[0]]), ((0, 4), (0, 0)))
    o_b = _diff_attn(p, lam4, subln_w[0][None, :], batch, t_len, tq=min(2048, t_len), tk=512, rb=256)

    return _out_proj(x, o_cmp, o_sel, o_win, gl, p, o_b, w_out[0].astype(BF16), final_norm_w[None, :],
                     batch, t_len, tm=min(512, t_len))
```

```python
import functools
import math

import numpy as np
import jax
import jax.numpy as jnp
from jax import lax
from jax.experimental import pallas as pl
from jax.experimental.pallas import tpu as pltpu

F32 = jnp.float32
BF16 = jnp.bfloat16

D_MODEL = 2048
HEAD_DIM = 128
NSA_HEADS = 8
NSA_GROUPS = 2
NSA_HPG = NSA_HEADS // NSA_GROUPS
NSA_WIDTH = NSA_HEADS * HEAD_DIM
NSA_KV = NSA_GROUPS * HEAD_DIM
NSA_BRANCHES = 3
CMP_BLOCK = 32
CMP_STRIDE = 16
CMP_HIDDEN = 512
SEL_BLOCK = 64
N_SELECT = 16
WINDOW = 512
DIFF_HEADS = 4
DIFF_QK = 128
DIFF_V = 256
DIFF_WIDTH = DIFF_HEADS * DIFF_V
MIX_WIDTH = NSA_WIDTH + DIFF_WIDTH
Q_BLOCK = 128
ROPE_THETA = 10000.0
NORM_EPS = 1e-6
SUBLN_EPS = 1e-5
NEG_INF = -1e30
FORCE_SCORE = 1e9
LAMBDA_INIT = 0.8 - 0.6 * math.exp(-0.3 * 0)

LANE = 128
VMEM_LIMIT = 56 * 1024 * 1024

SLOT_QA = 0
SLOT_QB = 8
SLOT_KB = 16
SLOT_KC = 24
SLOT_KS = 26
SLOT_KW = 28
N_ROPE_SLOTS = 30
SLOT_VC = 30
SLOT_ZA = 32
SLOT_ZB = 40
SLOT_VB = 48
SLOT_VS = 56
SLOT_VW = 58
N_SLOTS = 60
SLOTS_PER_STEP = 12
N_QUERY_SLOTS = 16
QK_EXP2_SCALE = (HEAD_DIM ** -0.5) * math.log2(math.e)

_OFF = np.concatenate([[0], np.cumsum([NSA_WIDTH, NSA_KV, NSA_KV, NSA_KV, NSA_KV, NSA_KV, NSA_KV,
                                       NSA_HEADS * NSA_BRANCHES, NSA_WIDTH, 1024, 1024, DIFF_WIDTH, DIFF_WIDTH])])
(_O_QA, _O_KC, _O_VC, _O_KS, _O_VS, _O_KW, _O_VW, _O_G, _O_ZA, _O_QB, _O_KB, _O_VB, _O_ZB, _O_END) = [int(v) for v in _OFF]


def _cparams(sem):
    return pltpu.CompilerParams(dimension_semantics=sem, vmem_limit_bytes=VMEM_LIMIT)


def _in_proj_kernel(x_ref, nw_ref, w_ref, wg_ref, cos_ref, sin_ref, p_ref, g_ref, h_ref):
    j = pl.program_id(1)

    @pl.when(j == 0)
    def _():
        x = x_ref[...]
        y = x * lax.rsqrt(jnp.mean(x * x, axis=-1, keepdims=True) + NORM_EPS)
        h = (y * nw_ref[...]).astype(BF16)
        h_ref[...] = h
        g_ref[0] = jnp.dot(h, wg_ref[...], preferred_element_type=F32)

    acc = jnp.dot(h_ref[...], w_ref[...], preferred_element_type=F32)

    cos_t = cos_ref[...]
    sin_t = sin_ref[...]
    for s in range(SLOTS_PER_STEP):
        slot = j * SLOTS_PER_STEP + s
        cos = jnp.where(slot < N_ROPE_SLOTS, cos_t, 1.0)
        sin = jnp.where(slot < N_ROPE_SLOTS, sin_t, 0.0)
        a = acc[:, s * LANE:(s + 1) * LANE]
        r = a * cos + pltpu.roll(a, HEAD_DIM // 2, 1) * sin
        f = jnp.where(slot < N_QUERY_SLOTS, QK_EXP2_SCALE, 1.0)
        p_ref[0, s] = (r * f).astype(BF16)


def _in_proj(x2, norm_w, w_r, w_g, cos_f, sin_f, batch, t_len, tm):
    m = x2.shape[0]
    tiles_per_seq = t_len // tm
    n_steps = N_SLOTS // SLOTS_PER_STEP
    tn = SLOTS_PER_STEP * LANE
    return pl.pallas_call(
        _in_proj_kernel,
        grid=(m // tm, n_steps),
        in_specs=[
            pl.BlockSpec((tm, D_MODEL), lambda i, j: (i, 0)),
            pl.BlockSpec((1, D_MODEL), lambda i, j: (0, 0)),
            pl.BlockSpec((D_MODEL, tn), lambda i, j: (0, j)),
            pl.BlockSpec((D_MODEL, LANE), lambda i, j: (0, 0)),
            pl.BlockSpec((tm, LANE), lambda i, j: (i % tiles_per_seq, 0)),
            pl.BlockSpec((tm, LANE), lambda i, j: (i % tiles_per_seq, 0)),
        ],
        out_specs=[
            pl.BlockSpec((1, SLOTS_PER_STEP, tm, LANE), lambda i, j: (i // tiles_per_seq, j, i % tiles_per_seq, 0)),
            pl.BlockSpec((1, tm, LANE), lambda i, j: (i // tiles_per_seq, i % tiles_per_seq, 0)),
        ],
        out_shape=[
            jax.ShapeDtypeStruct((batch, N_SLOTS, t_len, LANE), BF16),
            jax.ShapeDtypeStruct((batch, t_len, LANE), F32),
        ],
        scratch_shapes=[pltpu.VMEM((tm, D_MODEL), BF16)],
        compiler_params=_cparams(("parallel", "arbitrary")),
        name="in_proj",
    )(x2, norm_w, w_r, w_g, cos_f, sin_f)


def _compress_kernel(r_ref, w1_ref, pos_ref, w2_ref, o_ref, *, n_rows):
    w1 = w1_ref[0]
    a = jnp.dot(r_ref[0, 0], w1, preferred_element_type=F32)
    c = (jnp.dot(pos_ref[0], w1[:, :CMP_HIDDEN], preferred_element_type=F32)
         + jnp.dot(pos_ref[1], w1[:, CMP_HIDDEN:], preferred_element_type=F32))[0:1]
    nxt = pltpu.roll(a[:, CMP_HIDDEN:], n_rows - 1, 0)
    hid = a[:, :CMP_HIDDEN] + nxt + c
    hid = hid * jax.nn.sigmoid(hid)
    out = jnp.dot(hid.astype(BF16), w2_ref[0], preferred_element_type=F32)
    row = lax.broadcasted_iota(jnp.int32, out.shape, 0)
    o_ref[0, 0, 0] = jnp.where(row < n_rows - 1, out, 0.0).astype(BF16)


def _compress(p16, w1cat, pos2, w2, batch, n_rows):
    kern = functools.partial(_compress_kernel, n_rows=n_rows)

    return pl.pallas_call(
        kern,
        grid=(batch, 2, NSA_GROUPS),
        in_specs=[
            pl.BlockSpec((1, 1, n_rows, CMP_STRIDE * HEAD_DIM), lambda b, kv, g: (b, kv * NSA_GROUPS + g, 0, 0)),
            pl.BlockSpec((1, CMP_STRIDE * HEAD_DIM, 2 * CMP_HIDDEN), lambda b, kv, g: (kv, 0, 0)),
            pl.BlockSpec((2, 8, CMP_STRIDE * HEAD_DIM), lambda b, kv, g: (0, 0, 0)),
            pl.BlockSpec((1, CMP_HIDDEN, HEAD_DIM), lambda b, kv, g: (kv, 0, 0)),
        ],
        out_specs=pl.BlockSpec((1, 1, 1, n_rows, HEAD_DIM), lambda b, kv, g: (b, kv, g, 0, 0)),
        out_shape=jax.ShapeDtypeStruct((batch, 2, NSA_GROUPS, n_rows, HEAD_DIM), BF16),
        compiler_params=_cparams(("parallel", "parallel", "parallel")),
        name="compress",
    )(p16, w1cat, pos2, w2)


_NT = (((1,), (1,)), ((), ()))
CMP_PER_SEL = SEL_BLOCK // CMP_STRIDE


def _cmp_select_body(q_ref, kc_ref, vc_ref, map_ref, o_ref, bias_ref, start, *, n_top, w, bw):
    n_selp = map_ref.shape[0]
    q = q_ref[0].reshape(NSA_HPG * Q_BLOCK, HEAD_DIM)
    s = lax.dot_general(q, kc_ref[0, 0, 0, :w, :], _NT, preferred_element_type=F32)
    s = s.reshape(NSA_HPG, Q_BLOCK, w)
    t_q = start + lax.broadcasted_iota(jnp.int32, (1, Q_BLOCK, w), 1)
    n_id = lax.broadcasted_iota(jnp.int32, (1, Q_BLOCK, w), 2)
    valid = n_id * CMP_STRIDE + (CMP_BLOCK - 1) <= t_q
    s = jnp.where(valid, s, NEG_INF)
    e = jnp.exp2(s - jnp.max(s, axis=-1, keepdims=True))
    den = jnp.sum(e, axis=-1, keepdims=True)
    sees_any = start + lax.broadcasted_iota(jnp.int32, (1, Q_BLOCK, 1), 1) >= CMP_BLOCK - 1
    p = e * jnp.where(sees_any, 1.0 / den, 0.0)
    o = jnp.dot(p.reshape(NSA_HPG * Q_BLOCK, w).astype(BF16), vc_ref[0, 0, 0, :w, :],
                preferred_element_type=F32)
    for h in range(NSA_HPG):
        o_ref[0, :, h * HEAD_DIM:(h + 1) * HEAD_DIM] = o[h * Q_BLOCK:(h + 1) * Q_BLOCK].astype(BF16)

    psum = p[0] + p[1] + p[2] + p[3]
    p_hi = psum.astype(BF16)
    p_lo = (psum - p_hi.astype(F32)).astype(BF16)
    smap_t = map_ref[:bw, :w]
    imp = (lax.dot_general(smap_t, p_hi, _NT, preferred_element_type=F32)
           + lax.dot_general(smap_t, p_lo, _NT, preferred_element_type=F32))
    blk = lax.broadcasted_iota(jnp.int32, (bw, Q_BLOCK), 0)
    cur = (start + lax.broadcasted_iota(jnp.int32, (bw, Q_BLOCK), 1)) // SEL_BLOCK
    forced = (blk == 0) | (blk == cur) | (blk == cur - 1)
    blk_f = blk.astype(F32)
    work = jnp.where(forced, -jnp.inf, jnp.where(blk > cur, NEG_INF, imp))
    for _ in range(n_top - 3):
        top = jnp.max(work, axis=0, keepdims=True)
        first = jnp.min(jnp.where(work == top, blk_f, 1e9), axis=0, keepdims=True)
        work = jnp.where(blk_f == first, -jnp.inf, work)
    keep = (work == -jnp.inf) & (blk <= cur)
    bias_t = jnp.where(keep, 0.0, NEG_INF)
    if bw < n_selp:
        bias_t = jnp.concatenate([bias_t, jnp.full((n_selp - bw, Q_BLOCK), NEG_INF, F32)], axis=0)
    bias_ref[0, 0] = bias_t.T.astype(BF16)


def _cmp_select_kernel(q_ref, kc_ref, vc_ref, map_ref, o_ref, bias_ref, *, n_rows, n_top, chunk):
    start = pl.program_id(2) * Q_BLOCK
    variant = ((start + Q_BLOCK - 1) // SEL_BLOCK) // (chunk // CMP_PER_SEL)
    for v in range(n_rows // chunk):
        w = (v + 1) * chunk
        pl.when(variant == v)(functools.partial(
            _cmp_select_body, q_ref, kc_ref, vc_ref, map_ref, o_ref, bias_ref, start,
            n_top=n_top, w=w, bw=min(w // CMP_PER_SEL, map_ref.shape[0])))


def _cmp_select(p, kvc, sel_map, batch, t_len):
    n_rows = t_len // CMP_STRIDE
    n_selp = sel_map.shape[0]
    n_top = min(N_SELECT, t_len // SEL_BLOCK)
    assert n_top >= 3
    chunk = min(256, n_rows)
    assert n_rows % chunk == 0 and (chunk // CMP_PER_SEL) % 8 == 0
    kern = functools.partial(_cmp_select_kernel, n_rows=n_rows, n_top=n_top, chunk=chunk)
    return pl.pallas_call(
        kern,
        grid=(batch, NSA_GROUPS, t_len // Q_BLOCK),
        in_specs=[
            pl.BlockSpec((1, NSA_HPG, Q_BLOCK, HEAD_DIM), lambda b, g, c: (b, g, c, 0)),
            pl.BlockSpec((1, 1, 1, n_rows, HEAD_DIM), lambda b, g, c: (b, 0, g, 0, 0)),
            pl.BlockSpec((1, 1, 1, n_rows, HEAD_DIM), lambda b, g, c: (b, 1, g, 0, 0)),
            pl.BlockSpec((n_selp, n_rows), lambda b, g, c: (0, 0)),
        ],
        out_specs=[
            pl.BlockSpec((1, Q_BLOCK, NSA_HPG * HEAD_DIM), lambda b, g, c: (b, c, g)),
            pl.BlockSpec((1, 1, Q_BLOCK, n_selp), lambda b, g, c: (b, g, c, 0)),
        ],
        out_shape=[
            jax.ShapeDtypeStruct((batch, t_len, NSA_WIDTH), BF16),
            jax.ShapeDtypeStruct((batch, NSA_GROUPS, t_len, n_selp), BF16),
        ],
        compiler_params=_cparams(("parallel", "parallel", "parallel")),
        name="cmp_select",
    )(p, kvc, kvc, sel_map)


def _lane_tile(x, n):
    return x if n == 1 else jnp.concatenate([x] * n, axis=1)


def _flash_update(s, v, rows, m_ref, l_ref, acc_ref):
    n_keys = s.shape[1]
    dv = acc_ref.shape[-1]
    m_prev = m_ref[rows, :]
    m_new = jnp.maximum(m_prev, jnp.max(s, axis=-1, keepdims=True))
    alpha = jnp.exp2(m_prev - m_new)
    p = jnp.exp2(s - _lane_tile(m_new, n_keys // LANE))
    if l_ref is not None:
        psum = p[:, :LANE]
        for cc in range(1, n_keys // LANE):
            psum = psum + p[:, cc * LANE:(cc + 1) * LANE]
        l_ref[rows, :] = alpha * l_ref[rows, :] + psum
    acc_ref[rows, :] = (_lane_tile(alpha, dv // LANE) * acc_ref[rows, :]
                        + jnp.dot(p.astype(BF16), v, preferred_element_type=F32))
    m_ref[rows, :] = m_new


def _causal_triangle(rb):
    return lax.broadcasted_iota(jnp.int32, (rb, rb), 1) <= lax.broadcasted_iota(jnp.int32, (rb, rb), 0)


def _sel_attn_kernel(q_ref, bias_ref, k_ref, v_ref, hot_ref, o_ref, qa_ref, m_ref, acc_ref,
                     *, tq, tk, rb, n_halves):
    start = pl.program_id(2) * tq
    n_rows = NSA_HPG * tq
    keys_per_half = LANE * SEL_BLOCK
    for hf in range(n_halves):
        for h in range(NSA_HPG):
            qa_ref[hf, h * tq:(h + 1) * tq, :LANE] = q_ref[0, h]
            qa_ref[hf, h * tq:(h + 1) * tq, LANE:] = bias_ref[0, 0, :, hf * LANE:(hf + 1) * LANE]
    m_ref[...] = jnp.full(m_ref.shape, NEG_INF, F32)
    acc_ref[...] = jnp.zeros(acc_ref.shape, F32)

    def k_aug(k0, n):
        return jnp.concatenate([k_ref[0, 0, pl.ds(k0, n), :], hot_ref[pl.ds(k0, n), :]], axis=1)

    def v_aug(k0, n):
        return jnp.concatenate([v_ref[0, 0, pl.ds(k0, n), :], jnp.ones((n, LANE), BF16)], axis=1)

    def body(j, carry):
        k0 = pl.multiple_of(j * tk, tk)
        ka, v, hf = k_aug(k0, tk), v_aug(k0, tk), k0 // keys_per_half
        for r in range(n_rows // rb):
            rows = pl.ds(r * rb, rb)
            s = lax.dot_general(qa_ref[hf, rows, :], ka, _NT, preferred_element_type=F32)
            _flash_update(s, v, rows, m_ref, None, acc_ref)
        return carry

    lax.fori_loop(0, start // tk, body, 0)

    tri = _causal_triangle(rb)
    for d in range(tq // rb):
        k0 = pl.multiple_of(start + d * rb, rb)
        ka, v, hf = k_aug(k0, rb), v_aug(k0, rb), k0 // keys_per_half
        for r in range(n_rows // rb):
            q_off = (r * rb) % tq
            if d * rb > q_off:
                continue
            rows = pl.ds(r * rb, rb)
            s = lax.dot_general(qa_ref[hf, rows, :], ka, _NT, preferred_element_type=F32)
            if d * rb == q_off:
                s = jnp.where(tri, s, NEG_INF)
            _flash_update(s, v, rows, m_ref, None, acc_ref)

    for h in range(NSA_HPG):
        hr = slice(h * tq, (h + 1) * tq)
        o_ref[0, :, h * HEAD_DIM:(h + 1) * HEAD_DIM] = (acc_ref[hr, :HEAD_DIM] / acc_ref[hr, HEAD_DIM:]).astype(BF16)


def _resident(block_shape, index_map):
    return pl.BlockSpec(block_shape, index_map, pipeline_mode=pl.Buffered(1))


def _sel_attn(p, bias, hot, batch, t_len, tq, tk, rb):
    assert tq % tk == 0 and tk % rb == 0 and (LANE * SEL_BLOCK) % tk == 0
    n_selp = bias.shape[-1]
    n_halves = n_selp // LANE
    n_rows = NSA_HPG * tq
    kern = functools.partial(_sel_attn_kernel, tq=tq, tk=tk, rb=rb, n_halves=n_halves)
    return pl.pallas_call(
        kern,
        grid=(batch, NSA_GROUPS, t_len // tq),
        in_specs=[
            pl.BlockSpec((1, NSA_HPG, tq, HEAD_DIM), lambda b, g, c: (b, g, c, 0)),
            pl.BlockSpec((1, 1, tq, n_selp), lambda b, g, c: (b, g, c, 0)),
            _resident((1, 1, t_len, HEAD_DIM), lambda b, g, c: (b, SLOT_KS + g, 0, 0)),
            _resident((1, 1, t_len, HEAD_DIM), lambda b, g, c: (b, SLOT_VS + g, 0, 0)),
            _resident((t_len, LANE), lambda b, g, c: (0, 0)),
        ],
        out_specs=pl.BlockSpec((1, tq, NSA_HPG * HEAD_DIM), lambda b, g, c: (b, c, g)),
        out_shape=jax.ShapeDtypeStruct((batch, t_len, NSA_WIDTH), BF16),
        scratch_shapes=[
            pltpu.VMEM((n_halves, n_rows, 2 * LANE), BF16),
            pltpu.VMEM((n_rows, LANE), F32),
            pltpu.VMEM((n_rows, 2 * HEAD_DIM), F32),
        ],
        compiler_params=_cparams(("parallel", "parallel", "arbitrary")),
        name="sel_attn",
    )(p, bias, p, p, hot)


def _win_attn_kernel(q_ref, k_ref, v_ref, o_ref, *, tq, span):
    start = pl.program_id(2) * tq
    rows = NSA_HPG * Q_BLOCK
    for qc in range(tq // Q_BLOCK):
        q_rows = slice(qc * Q_BLOCK, (qc + 1) * Q_BLOCK)
        q0 = start + qc * Q_BLOCK
        k0 = pl.multiple_of(jnp.maximum(q0 + Q_BLOCK - span, 0), Q_BLOCK)
        q = q_ref[0, :, q_rows, :].reshape(rows, HEAD_DIM)
        s = lax.dot_general(q, k_ref[0, 0, pl.ds(k0, span), :], _NT, preferred_element_type=F32)
        s = s.reshape(NSA_HPG, Q_BLOCK, span)
        t_q = q0 + lax.broadcasted_iota(jnp.int32, (1, Q_BLOCK, span), 1)
        k_pos = k0 + lax.broadcasted_iota(jnp.int32, (1, Q_BLOCK, span), 2)
        valid = (k_pos <= t_q) & (k_pos > t_q - WINDOW)
        s = jnp.where(valid, s, NEG_INF)
        e = jnp.exp2(s - jnp.max(s, axis=-1, keepdims=True))
        v = jnp.concatenate([v_ref[0, 0, pl.ds(k0, span), :], jnp.ones((span, LANE), BF16)], axis=1)
        o = jnp.dot(e.reshape(rows, span).astype(BF16), v, preferred_element_type=F32)
        o = o[:, :HEAD_DIM] / o[:, HEAD_DIM:]
        for h in range(NSA_HPG):
            o_ref[0, q_rows, h * HEAD_DIM:(h + 1) * HEAD_DIM] = o[h * Q_BLOCK:(h + 1) * Q_BLOCK].astype(BF16)


def _win_attn(p, batch, t_len, tq):
    span = min(WINDOW + Q_BLOCK, t_len)
    kern = functools.partial(_win_attn_kernel, tq=tq, span=span)
    return pl.pallas_call(
        kern,
        grid=(batch, NSA_GROUPS, t_len // tq),
        in_specs=[
            pl.BlockSpec((1, NSA_HPG, tq, HEAD_DIM), lambda b, g, c: (b, g, c, 0)),
            _resident((1, 1, t_len, HEAD_DIM), lambda b, g, c: (b, SLOT_KW + g, 0, 0)),
            _resident((1, 1, t_len, HEAD_DIM), lambda b, g, c: (b, SLOT_VW + g, 0, 0)),
        ],
        out_specs=pl.BlockSpec((1, tq, NSA_HPG * HEAD_DIM), lambda b, g, c: (b, c, g)),
        out_shape=jax.ShapeDtypeStruct((batch, t_len, NSA_WIDTH), BF16),
        compiler_params=_cparams(("parallel", "parallel", "parallel")),
        name="win_attn",
    )(p, p, p)


def _diff_attn_kernel(q1_ref, q2_ref, k1_ref, k2_ref, v_ref, lam_ref, sw_ref, o_ref,
                      m_ref, l_ref, acc_ref, *, tq, tk, rb):
    start = pl.program_id(2) * tq
    m_ref[...] = jnp.full(m_ref.shape, NEG_INF, F32)
    l_ref[...] = jnp.zeros(l_ref.shape, F32)
    acc_ref[...] = jnp.zeros(acc_ref.shape, F32)
    q_refs = (q1_ref, q2_ref)
    k_refs = (k1_ref, k2_ref)

    def v_tile(k0, n):
        return jnp.concatenate([v_ref[0, 0, pl.ds(k0, n), :], v_ref[0, 1, pl.ds(k0, n), :]], axis=1)

    def body(j, carry):
        k0 = pl.multiple_of(j * tk, tk)
        v = v_tile(k0, tk)
        for mp in range(2):
            kt = k_refs[mp][0, 0, pl.ds(k0, tk), :]
            for r in range(tq // rb):
                rows = pl.ds(r * rb, rb)
                s = lax.dot_general(q_refs[mp][0, 0, rows, :], kt, _NT, preferred_element_type=F32)
                _flash_update(s, v, rows, m_ref.at[mp], l_ref.at[mp], acc_ref.at[mp])
        return carry

    lax.fori_loop(0, start // tk, body, 0)

    tri = _causal_triangle(rb)
    for d in range(tq // rb):
        k0 = pl.multiple_of(start + d * rb, rb)
        v = v_tile(k0, rb)
        for mp in range(2):
            kt = k_refs[mp][0, 0, pl.ds(k0, rb), :]
            for r in range(d, tq // rb):
                rows = pl.ds(r * rb, rb)
                s = lax.dot_general(q_refs[mp][0, 0, rows, :], kt, _NT, preferred_element_type=F32)
                if r == d:
                    s = jnp.where(tri, s, NEG_INF)
                _flash_update(s, v, rows, m_ref.at[mp], l_ref.at[mp], acc_ref.at[mp])

    lv = lam_ref[...]
    lam = (jnp.exp(jnp.sum(lv[0:1] * lv[1:2], axis=-1, keepdims=True))
           - jnp.exp(jnp.sum(lv[2:3] * lv[3:4], axis=-1, keepdims=True)) + LAMBDA_INIT)
    l1 = jnp.sum(l_ref[0], axis=-1, keepdims=True)
    l2 = jnp.sum(l_ref[1], axis=-1, keepdims=True)
    o = acc_ref[0] / l1 - lam * (acc_ref[1] / l2)
    y = o * lax.rsqrt(jnp.mean(o * o, axis=-1, keepdims=True) + SUBLN_EPS)
    o_ref[0] = (y * sw_ref[...] * (1.0 - LAMBDA_INIT)).astype(BF16)


def _diff_attn(p, lam4, subln_w, batch, t_len, tq, tk, rb):
    assert tq % tk == 0 and tk % rb == 0
    kern = functools.partial(_diff_attn_kernel, tq=tq, tk=tk, rb=rb)
    return pl.pallas_call(
        kern,
        grid=(batch, DIFF_HEADS, t_len // tq),
        in_specs=[
            pl.BlockSpec((1, 1, tq, DIFF_QK), lambda b, h, i: (b, SLOT_QB + h, i, 0)),
            pl.BlockSpec((1, 1, tq, DIFF_QK), lambda b, h, i: (b, SLOT_QB + DIFF_HEADS + h, i, 0)),
            _resident((1, 1, t_len, DIFF_QK), lambda b, h, i: (b, SLOT_KB + h, 0, 0)),
            _resident((1, 1, t_len, DIFF_QK), lambda b, h, i: (b, SLOT_KB + DIFF_HEADS + h, 0, 0)),
            _resident((1, 2, t_len, LANE), lambda b, h, i: (b, SLOT_VB // 2 + h, 0, 0)),
            pl.BlockSpec((8, DIFF_QK), lambda b, h, i: (0, 0)),
            pl.BlockSpec((1, DIFF_V), lambda b, h, i: (0, 0)),
        ],
        out_specs=pl.BlockSpec((1, tq, DIFF_V), lambda b, h, i: (b, i, h)),
        out_shape=jax.ShapeDtypeStruct((batch, t_len, DIFF_WIDTH), BF16),
        scratch_shapes=[
            pltpu.VMEM((2, tq, LANE), F32),
            pltpu.VMEM((2, tq, LANE), F32),
            pltpu.VMEM((2, tq, DIFF_V), F32),
        ],
        compiler_params=_cparams(("parallel", "parallel", "arbitrary")),
        name="diff_attn",
    )(p, p, p, p, p, lam4, subln_w)


OUT_K_SLOTS = 2


def _out_proj_kernel(x_ref, oc_ref, os_ref, ow_ref, g_ref, za_ref, ob_ref, zb_ref, w_ref, fw_ref, o_ref):
    gates = jax.nn.sigmoid(g_ref[0])
    tm = gates.shape[0]

    def nsa_piece(hd):
        cols = slice(hd * HEAD_DIM, (hd + 1) * HEAD_DIM)
        g0 = jnp.broadcast_to(gates[:, 3 * hd:3 * hd + 1], (tm, HEAD_DIM))
        g1 = jnp.broadcast_to(gates[:, 3 * hd + 1:3 * hd + 2], (tm, HEAD_DIM))
        g2 = jnp.broadcast_to(gates[:, 3 * hd + 2:3 * hd + 3], (tm, HEAD_DIM))
        o = (g0 * oc_ref[0, :, cols].astype(F32) + g1 * os_ref[0, :, cols].astype(F32)
             + g2 * ow_ref[0, :, cols].astype(F32))
        z = za_ref[0, hd].astype(F32)
        return (o * (z * jax.nn.sigmoid(z))).astype(BF16)

    def diff_piece(sl):
        z = zb_ref[0, sl].astype(F32)
        return (ob_ref[0, :, sl * LANE:(sl + 1) * LANE].astype(F32) * (z * jax.nn.sigmoid(z))).astype(BF16)

    pieces = [functools.partial(nsa_piece, hd) for hd in range(NSA_HEADS)]
    pieces += [functools.partial(diff_piece, sl) for sl in range(DIFF_WIDTH // LANE)]
    y = x_ref[0]
    for kc in range(0, len(pieces), OUT_K_SLOTS):
        chunk = jnp.concatenate([f() for f in pieces[kc:kc + OUT_K_SLOTS]], axis=1)
        y = y + jnp.dot(chunk, w_ref[kc * LANE:(kc + OUT_K_SLOTS) * LANE, :], preferred_element_type=F32)
    y = y * lax.rsqrt(jnp.mean(y * y, axis=-1, keepdims=True) + NORM_EPS)
    o_ref[0] = y * fw_ref[...]


def _out_proj(x, o_cmp, o_sel, o_win, gl, p, o_b, w_out, final_w, batch, t_len, tm):
    row = lambda b, i: (b, i, 0)
    return pl.pallas_call(
        _out_proj_kernel,
        grid=(batch, t_len // tm),
        in_specs=[
            pl.BlockSpec((1, tm, D_MODEL), row),
            pl.BlockSpec((1, tm, NSA_WIDTH), row),
            pl.BlockSpec((1, tm, NSA_WIDTH), row),
            pl.BlockSpec((1, tm, NSA_WIDTH), row),
            pl.BlockSpec((1, tm, LANE), row),
            pl.BlockSpec((1, 8, tm, LANE), lambda b, i: (b, SLOT_ZA // 8, i, 0)),
            pl.BlockSpec((1, tm, DIFF_WIDTH), row),
            pl.BlockSpec((1, 8, tm, LANE), lambda b, i: (b, SLOT_ZB // 8, i, 0)),
            pl.BlockSpec((MIX_WIDTH, D_MODEL), lambda b, i: (0, 0)),
            pl.BlockSpec((1, D_MODEL), lambda b, i: (0, 0)),
        ],
        out_specs=pl.BlockSpec((1, tm, D_MODEL), row),
        out_shape=jax.ShapeDtypeStruct((batch, t_len, D_MODEL), F32),
        compiler_params=_cparams(("parallel", "parallel")),
        name="out_proj",
    )(x, o_cmp, o_sel, o_win, gl, p, o_b, p, w_out, final_w)


def _reorder_w_in(w):
    def heads(off, n):
        return [w[:, off + i * LANE: off + (i + 1) * LANE] for i in range(n)]
    cols = (heads(_O_QA, 8) + heads(_O_QB, 8) + heads(_O_KB, 8) + heads(_O_KC, 2) + heads(_O_KS, 2)
            + heads(_O_KW, 2) + heads(_O_VC, 2) + heads(_O_ZA, 8) + heads(_O_ZB, 8) + heads(_O_VB, 8)
            + heads(_O_VS, 2) + heads(_O_VW, 2))
    w_r = jnp.concatenate(cols, axis=1).astype(BF16)
    w_g = jnp.pad(w[:, _O_G:_O_G + NSA_HEADS * NSA_BRANCHES],
                  ((0, 0), (0, LANE - NSA_HEADS * NSA_BRANCHES))).astype(BF16)
    return w_r, w_g


def _sel_map(t_len, n_selp):
    n_rows = t_len // CMP_STRIDE
    c0 = np.arange(n_rows)[:, None] * CMP_STRIDE
    s0 = np.arange(n_selp)[None, :] * SEL_BLOCK
    ov = np.minimum(c0 + CMP_BLOCK, s0 + SEL_BLOCK) - np.maximum(c0, s0)
    m = np.maximum(ov, 0) / CMP_BLOCK
    m[n_rows - 1:] = 0.0
    m[:, t_len // SEL_BLOCK:] = 0.0
    return jnp.asarray(m.T, dtype=BF16)


def _block_one_hot(t_len):
    blk = (np.arange(t_len) // SEL_BLOCK) % LANE
    return jnp.asarray(blk[:, None] == np.arange(LANE)[None, :], dtype=BF16)


def kernel(x, norm_w, w_in, cmp_pos, cmp_k_w1, cmp_k_w2, cmp_v_w1, cmp_v_w2, lam_q1, lam_k1, lam_q2, lam_k2,
           subln_w, w_out, final_norm_w):
    batch, t_len, _ = x.shape
    n_rows = t_len // CMP_STRIDE
    n_selp = -(-(t_len // SEL_BLOCK) // LANE) * LANE

    pos = jnp.arange(t_len, dtype=F32)
    inv = ROPE_THETA ** (-jnp.arange(0, HEAD_DIM, 2, dtype=F32) / HEAD_DIM)
    ang = pos[:, None] * inv[None, :]
    cos, sin = jnp.cos(ang), jnp.sin(ang)
    cos_f = jnp.concatenate([cos, cos], axis=1)
    sin_f = jnp.concatenate([-sin, sin], axis=1)

    w_r, w_g = _reorder_w_in(w_in[0])
    tm_in = min(1024, t_len)
    p, gl = _in_proj(x.reshape(batch * t_len, D_MODEL), norm_w[0][None, :], w_r, w_g, cos_f, sin_f,
                     batch, t_len, tm_in)

    half = CMP_STRIDE * HEAD_DIM
    w1cat = jnp.stack([jnp.concatenate([w[:half], w[half:]], axis=1) for w in (cmp_k_w1[0], cmp_v_w1[0])]).astype(BF16)
    w2 = jnp.stack([cmp_k_w2[0], cmp_v_w2[0]]).astype(BF16)
    pos2 = jnp.broadcast_to(cmp_pos[0].reshape(2, 1, half), (2, 8, half)).astype(BF16)
    pc = jnp.concatenate([p[:, SLOT_KC:SLOT_KC + NSA_GROUPS], p[:, SLOT_VC:SLOT_VC + NSA_GROUPS]], axis=1)
    kvc = _compress(pc.reshape(batch, 2 * NSA_GROUPS, n_rows, half), w1cat, pos2, w2, batch, n_rows)

    o_cmp, bias = _cmp_select(p, kvc, _sel_map(t_len, n_selp), batch, t_len)
    o_sel = _sel_attn(p, bias, _block_one_hot(t_len), batch, t_len, tq=min(2048, t_len), tk=512, rb=256)
    o_win = _win_attn(p, batch, t_len, tq=512)

    lam4 = jnp.pad(jnp.stack([lam_q1[0], lam_k1[0], lam_q2[0], lam_k2[0]]), ((0, 4), (0, 0)))
    o_b = _diff_attn(p, lam4, subln_w[0][None, :], batch, t_len, tq=min(2048, t_len), tk=512, rb=256)

    return _out_proj(x, o_cmp, o_sel, o_win, gl, p, o_b, w_out[0].astype(BF16), final_norm_w[None, :],
                     batch, t_len, tm=min(512, t_len))
```

```python
import functools
import math

import numpy as np
import jax
import jax.numpy as jnp
from jax import lax
from jax.experimental import pallas as pl
from jax.experimental.pallas import tpu as pltpu

F32 = jnp.float32
BF16 = jnp.bfloat16

D_MODEL = 2048
HEAD_DIM = 128
NSA_HEADS = 8
NSA_GROUPS = 2
NSA_HPG = NSA_HEADS // NSA_GROUPS
NSA_WIDTH = NSA_HEADS * HEAD_DIM
NSA_KV = NSA_GROUPS * HEAD_DIM
NSA_BRANCHES = 3
CMP_BLOCK = 32
CMP_STRIDE = 16
CMP_HIDDEN = 512
SEL_BLOCK = 64
N_SELECT = 16
WINDOW = 512
DIFF_HEADS = 4
DIFF_QK = 128
DIFF_V = 256
DIFF_WIDTH = DIFF_HEADS * DIFF_V
MIX_WIDTH = NSA_WIDTH + DIFF_WIDTH
Q_BLOCK = 128
ROPE_THETA = 10000.0
NORM_EPS = 1e-6
SUBLN_EPS = 1e-5
NEG_INF = -1e30
FORCE_SCORE = 1e9
LAMBDA_INIT = 0.8 - 0.6 * math.exp(-0.3 * 0)

LANE = 128
VMEM_LIMIT = 56 * 1024 * 1024

SLOT_QA = 0
SLOT_QB = 8
SLOT_KB = 16
SLOT_KC = 24
SLOT_KS = 26
SLOT_KW = 28
N_ROPE_SLOTS = 30
SLOT_VC = 30
SLOT_ZA = 32
SLOT_ZB = 40
SLOT_VB = 48
SLOT_VS = 56
SLOT_VW = 58
N_SLOTS = 60
SLOTS_PER_STEP = 12
N_QUERY_SLOTS = 16
QK_EXP2_SCALE = (HEAD_DIM ** -0.5) * math.log2(math.e)

_OFF = np.concatenate([[0], np.cumsum([NSA_WIDTH, NSA_KV, NSA_KV, NSA_KV, NSA_KV, NSA_KV, NSA_KV,
                                       NSA_HEADS * NSA_BRANCHES, NSA_WIDTH, 1024, 1024, DIFF_WIDTH, DIFF_WIDTH])])
(_O_QA, _O_KC, _O_VC, _O_KS, _O_VS, _O_KW, _O_VW, _O_G, _O_ZA, _O_QB, _O_KB, _O_VB, _O_ZB, _O_END) = [int(v) for v in _OFF]


def _cparams(sem):
    return pltpu.CompilerParams(dimension_semantics=sem, vmem_limit_bytes=VMEM_LIMIT)


def _in_proj_kernel(x_ref, nw_ref, w_ref, wg_ref, cos_ref, sin_ref, p_ref, g_ref, h_ref):
    j = pl.program_id(1)

    @pl.when(j == 0)
    def _():
        x = x_ref[...]
        y = x * lax.rsqrt(jnp.mean(x * x, axis=-1, keepdims=True) + NORM_EPS)
        h = (y * nw_ref[...]).astype(BF16)
        h_ref[...] = h
        g_ref[0] = jnp.dot(h, wg_ref[...], preferred_element_type=F32)

    acc = jnp.dot(h_ref[...], w_ref[...], preferred_element_type=F32)

    cos_t = cos_ref[...]
    sin_t = sin_ref[...]
    for s in range(SLOTS_PER_STEP):
        slot = j * SLOTS_PER_STEP + s
        cos = jnp.where(slot < N_ROPE_SLOTS, cos_t, 1.0)
        sin = jnp.where(slot < N_ROPE_SLOTS, sin_t, 0.0)
        a = acc[:, s * LANE:(s + 1) * LANE]
        r = a * cos + pltpu.roll(a, HEAD_DIM // 2, 1) * sin
        f = jnp.where(slot < N_QUERY_SLOTS, QK_EXP2_SCALE, 1.0)
        p_ref[0, s] = (r * f).astype(BF16)


def _in_proj(x2, norm_w, w_r, w_g, cos_f, sin_f, batch, t_len, tm):
    m = x2.shape[0]
    tiles_per_seq = t_len // tm
    n_steps = N_SLOTS // SLOTS_PER_STEP
    tn = SLOTS_PER_STEP * LANE
    return pl.pallas_call(
        _in_proj_kernel,
        grid=(m // tm, n_steps),
        in_specs=[
            pl.BlockSpec((tm, D_MODEL), lambda i, j: (i, 0)),
            pl.BlockSpec((1, D_MODEL), lambda i, j: (0, 0)),
            pl.BlockSpec((D_MODEL, tn), lambda i, j: (0, j)),
            pl.BlockSpec((D_MODEL, LANE), lambda i, j: (0, 0)),
            pl.BlockSpec((tm, LANE), lambda i, j: (i % tiles_per_seq, 0)),
            pl.BlockSpec((tm, LANE), lambda i, j: (i % tiles_per_seq, 0)),
        ],
        out_specs=[
            pl.BlockSpec((1, SLOTS_PER_STEP, tm, LANE), lambda i, j: (i // tiles_per_seq, j, i % tiles_per_seq, 0)),
            pl.BlockSpec((1, tm, LANE), lambda i, j: (i // tiles_per_seq, i % tiles_per_seq, 0)),
        ],
        out_shape=[
            jax.ShapeDtypeStruct((batch, N_SLOTS, t_len, LANE), BF16),
            jax.ShapeDtypeStruct((batch, t_len, LANE), F32),
        ],
        scratch_shapes=[pltpu.VMEM((tm, D_MODEL), BF16)],
        compiler_params=_cparams(("parallel", "arbitrary")),
        name="in_proj",
    )(x2, norm_w, w_r, w_g, cos_f, sin_f)


def _compress_kernel(r_ref, w1_ref, pos_ref, w2_ref, o_ref, *, n_rows):
    w1 = w1_ref[0]
    a = jnp.dot(r_ref[0, 0], w1, preferred_element_type=F32)
    c = (jnp.dot(pos_ref[0], w1[:, :CMP_HIDDEN], preferred_element_type=F32)
         + jnp.dot(pos_ref[1], w1[:, CMP_HIDDEN:], preferred_element_type=F32))[0:1]
    nxt = pltpu.roll(a[:, CMP_HIDDEN:], n_rows - 1, 0)
    hid = a[:, :CMP_HIDDEN] + nxt + c
    hid = hid * jax.nn.sigmoid(hid)
    out = jnp.dot(hid.astype(BF16), w2_ref[0], preferred_element_type=F32)
    row = lax.broadcasted_iota(jnp.int32, out.shape, 0)
    o_ref[0, 0, 0] = jnp.where(row < n_rows - 1, out, 0.0).astype(BF16)


def _compress(p16, w1cat, pos2, w2, batch, n_rows):
    kern = functools.partial(_compress_kernel, n_rows=n_rows)

    return pl.pallas_call(
        kern,
        grid=(batch, 2, NSA_GROUPS),
        in_specs=[
            pl.BlockSpec((1, 1, n_rows, CMP_STRIDE * HEAD_DIM), lambda b, kv, g: (b, kv * NSA_GROUPS + g, 0, 0)),
            pl.BlockSpec((1, CMP_STRIDE * HEAD_DIM, 2 * CMP_HIDDEN), lambda b, kv, g: (kv, 0, 0)),
            pl.BlockSpec((2, 8, CMP_STRIDE * HEAD_DIM), lambda b, kv, g: (0, 0, 0)),
            pl.BlockSpec((1, CMP_HIDDEN, HEAD_DIM), lambda b, kv, g: (kv, 0, 0)),
        ],
        out_specs=pl.BlockSpec((1, 1, 1, n_rows, HEAD_DIM), lambda b, kv, g: (b, kv, g, 0, 0)),
        out_shape=jax.ShapeDtypeStruct((batch, 2, NSA_GROUPS, n_rows, HEAD_DIM), BF16),
        compiler_params=_cparams(("parallel", "parallel", "parallel")),
        name="compress",
    )(p16, w1cat, pos2, w2)


_NT = (((1,), (1,)), ((), ()))
CMP_PER_SEL = SEL_BLOCK // CMP_STRIDE


def _cmp_select_body(q_ref, kc_ref, vc_ref, map_ref, o_ref, bias_ref, start, *, n_top, w, bw):
    n_selp = map_ref.shape[0]
    q = q_ref[0].reshape(NSA_HPG * Q_BLOCK, HEAD_DIM)
    s = lax.dot_general(q, kc_ref[0, 0, 0, :w, :], _NT, preferred_element_type=F32)
    s = s.reshape(NSA_HPG, Q_BLOCK, w)
    t_q = start + lax.broadcasted_iota(jnp.int32, (1, Q_BLOCK, w), 1)
    n_id = lax.broadcasted_iota(jnp.int32, (1, Q_BLOCK, w), 2)
    valid = n_id * CMP_STRIDE + (CMP_BLOCK - 1) <= t_q
    s = jnp.where(valid, s, NEG_INF)
    e = jnp.exp2(s - jnp.max(s, axis=-1, keepdims=True))
    den = jnp.sum(e, axis=-1, keepdims=True)
    sees_any = start + lax.broadcasted_iota(jnp.int32, (1, Q_BLOCK, 1), 1) >= CMP_BLOCK - 1
    p = e * jnp.where(sees_any, 1.0 / den, 0.0)
    o = jnp.dot(p.reshape(NSA_HPG * Q_BLOCK, w).astype(BF16), vc_ref[0, 0, 0, :w, :],
                preferred_element_type=F32)
    for h in range(NSA_HPG):
        o_ref[0, :, h * HEAD_DIM:(h + 1) * HEAD_DIM] = o[h * Q_BLOCK:(h + 1) * Q_BLOCK].astype(BF16)

    psum = p[0] + p[1] + p[2] + p[3]
    p_hi = psum.astype(BF16)
    p_lo = (psum - p_hi.astype(F32)).astype(BF16)
    smap_t = map_ref[:bw, :w]
    imp = (lax.dot_general(smap_t, p_hi, _NT, preferred_element_type=F32)
           + lax.dot_general(smap_t, p_lo, _NT, preferred_element_type=F32))
    blk = lax.broadcasted_iota(jnp.int32, (bw, Q_BLOCK), 0)
    cur = (start + lax.broadcasted_iota(jnp.int32, (bw, Q_BLOCK), 1)) // SEL_BLOCK
    forced = (blk == 0) | (blk == cur) | (blk == cur - 1)
    blk_f = blk.astype(F32)
    work = jnp.where(forced, -jnp.inf, jnp.where(blk > cur, NEG_INF, imp))
    for _ in range(n_top - 3):
        top = jnp.max(work, axis=0, keepdims=True)
        first = jnp.min(jnp.where(work == top, blk_f, 1e9), axis=0, keepdims=True)
        work = jnp.where(blk_f == first, -jnp.inf, work)
    keep = (work == -jnp.inf) & (blk <= cur)
    bias_t = jnp.where(keep, 0.0, NEG_INF)
    if bw < n_selp:
        bias_t = jnp.concatenate([bias_t, jnp.full((n_selp - bw, Q_BLOCK), NEG_INF, F32)], axis=0)
    bias_ref[0, 0] = bias_t.T.astype(BF16)


def _cmp_select_kernel(q_ref, kc_ref, vc_ref, map_ref, o_ref, bias_ref, *, n_rows, n_top, chunk):
    start = pl.program_id(2) * Q_BLOCK
    variant = ((start + Q_BLOCK - 1) // SEL_BLOCK) // (chunk // CMP_PER_SEL)
    for v in range(n_rows // chunk):
        w = (v + 1) * chunk
        pl.when(variant == v)(functools.partial(
            _cmp_select_body, q_ref, kc_ref, vc_ref, map_ref, o_ref, bias_ref, start,
            n_top=n_top, w=w, bw=min(w // CMP_PER_SEL, map_ref.shape[0])))


def _cmp_select(p, kvc, sel_map, batch, t_len):
    n_rows = t_len // CMP_STRIDE
    n_selp = sel_map.shape[0]
    n_top = min(N_SELECT, t_len // SEL_BLOCK)
    assert n_top >= 3
    chunk = min(256, n_rows)
    assert n_rows % chunk == 0 and (chunk // CMP_PER_SEL) % 8 == 0
    kern = functools.partial(_cmp_select_kernel, n_rows=n_rows, n_top=n_top, chunk=chunk)
    return pl.pallas_call(
        kern,
        grid=(batch, NSA_GROUPS, t_len // Q_BLOCK),
        in_specs=[
            pl.BlockSpec((1, NSA_HPG, Q_BLOCK, HEAD_DIM), lambda b, g, c: (b, g, c, 0)),
            pl.BlockSpec((1, 1, 1, n_rows, HEAD_DIM), lambda b, g, c: (b, 0, g, 0, 0)),
            pl.BlockSpec((1, 1, 1, n_rows, HEAD_DIM), lambda b, g, c: (b, 1, g, 0, 0)),
            pl.BlockSpec((n_selp, n_rows), lambda b, g, c: (0, 0)),
        ],
        out_specs=[
            pl.BlockSpec((1, Q_BLOCK, NSA_HPG * HEAD_DIM), lambda b, g, c: (b, c, g)),
            pl.BlockSpec((1, 1, Q_BLOCK, n_selp), lambda b, g, c: (b, g, c, 0)),
        ],
        out_shape=[
            jax.ShapeDtypeStruct((batch, t_len, NSA_WIDTH), BF16),
            jax.ShapeDtypeStruct((batch, NSA_GROUPS, t_len, n_selp), BF16),
        ],
        compiler_params=_cparams(("parallel", "parallel", "parallel")),
        name="cmp_select",
    )(p, kvc, kvc, sel_map)


def _lane_tile(x, n):
    return x if n == 1 else jnp.concatenate([x] * n, axis=1)


def _flash_update(s, v, rows, m_ref, l_ref, acc_ref):
    n_keys = s.shape[1]
    dv = acc_ref.shape[-1]
    m_prev = m_ref[rows, :]
    m_new = jnp.maximum(m_prev, jnp.max(s, axis=-1, keepdims=True))
    alpha = jnp.exp2(m_prev - m_new)
    p = jnp.exp2(s - _lane_tile(m_new, n_keys // LANE))
    if l_ref is not None:
        psum = p[:, :LANE]
        for cc in range(1, n_keys // LANE):
            psum = psum + p[:, cc * LANE:(cc + 1) * LANE]
        l_ref[rows, :] = alpha * l_ref[rows, :] + psum
    acc_ref[rows, :] = (_lane_tile(alpha, dv // LANE) * acc_ref[rows, :]
                        + jnp.dot(p.astype(BF16), v, preferred_element_type=F32))
    m_ref[rows, :] = m_new


def _causal_triangle(rb):
    return lax.broadcasted_iota(jnp.int32, (rb, rb), 1) <= lax.broadcasted_iota(jnp.int32, (rb, rb), 0)


def _sel_attn_kernel(q_ref, bias_ref, k_ref, v_ref, hot_ref, o_ref, qa_ref, m_ref, acc_ref,
                     *, tq, tk, rb, n_halves):
    start = pl.program_id(2) * tq
    n_rows = NSA_HPG * tq
    keys_per_half = LANE * SEL_BLOCK
    for hf in range(n_halves):
        for h in range(NSA_HPG):
            qa_ref[hf, h * tq:(h + 1) * tq, :LANE] = q_ref[0, h]
            qa_ref[hf, h * tq:(h + 1) * tq, LANE:] = bias_ref[0, 0, :, hf * LANE:(hf + 1) * LANE]
    m_ref[...] = jnp.full(m_ref.shape, NEG_INF, F32)
    acc_ref[...] = jnp.zeros(acc_ref.shape, F32)

    def k_aug(k0, n):
        return jnp.concatenate([k_ref[0, 0, pl.ds(k0, n), :], hot_ref[pl.ds(k0, n), :]], axis=1)

    def v_aug(k0, n):
        return jnp.concatenate([v_ref[0, 0, pl.ds(k0, n), :], jnp.ones((n, LANE), BF16)], axis=1)

    def body(j, carry):
        k0 = pl.multiple_of(j * tk, tk)
        ka, v, hf = k_aug(k0, tk), v_aug(k0, tk), k0 // keys_per_half
        for r in range(n_rows // rb):
            rows = pl.ds(r * rb, rb)
            s = lax.dot_general(qa_ref[hf, rows, :], ka, _NT, preferred_element_type=F32)
            _flash_update(s, v, rows, m_ref, None, acc_ref)
        return carry

    lax.fori_loop(0, start // tk, body, 0)

    tri = _causal_triangle(rb)
    for d in range(tq // rb):
        k0 = pl.multiple_of(start + d * rb, rb)
        ka, v, hf = k_aug(k0, rb), v_aug(k0, rb), k0 // keys_per_half
        for r in range(n_rows // rb):
            q_off = (r * rb) % tq
            if d * rb > q_off:
                continue
            rows = pl.ds(r * rb, rb)
            s = lax.dot_general(qa_ref[hf, rows, :], ka, _NT, preferred_element_type=F32)
            if d * rb == q_off:
                s = jnp.where(tri, s, NEG_INF)
            _flash_update(s, v, rows, m_ref, None, acc_ref)

    for h in range(NSA_HPG):
        hr = slice(h * tq, (h + 1) * tq)
        o_ref[0, :, h * HEAD_DIM:(h + 1) * HEAD_DIM] = (acc_ref[hr, :HEAD_DIM] / acc_ref[hr, HEAD_DIM:]).astype(BF16)


def _resident(block_shape, index_map):
    return pl.BlockSpec(block_shape, index_map, pipeline_mode=pl.Buffered(1))


def _sel_attn(p, bias, hot, batch, t_len, tq, tk, rb):
    assert tq % tk == 0 and tk % rb == 0 and (LANE * SEL_BLOCK) % tk == 0
    n_selp = bias.shape[-1]
    n_halves = n_selp // LANE
    n_rows = NSA_HPG * tq
    kern = functools.partial(_sel_attn_kernel, tq=tq, tk=tk, rb=rb, n_halves=n_halves)
    return pl.pallas_call(
        kern,
        grid=(batch, NSA_GROUPS, t_len // tq),
        in_specs=[
            pl.BlockSpec((1, NSA_HPG, tq, HEAD_DIM), lambda b, g, c: (b, g, c, 0)),
            pl.BlockSpec((1, 1, tq, n_selp), lambda b, g, c: (b, g, c, 0)),
            _resident((1, 1, t_len, HEAD_DIM), lambda b, g, c: (b, SLOT_KS + g, 0, 0)),
            _resident((1, 1, t_len, HEAD_DIM), lambda b, g, c: (b, SLOT_VS + g, 0, 0)),
            _resident((t_len, LANE), lambda b, g, c: (0, 0)),
        ],
        out_specs=pl.BlockSpec((1, tq, NSA_HPG * HEAD_DIM), lambda b, g, c: (b, c, g)),
        out_shape=jax.ShapeDtypeStruct((batch, t_len, NSA_WIDTH), BF16),
        scratch_shapes=[
            pltpu.VMEM((n_halves, n_rows, 2 * LANE), BF16),
            pltpu.VMEM((n_rows, LANE), F32),
            pltpu.VMEM((n_rows, 2 * HEAD_DIM), F32),
        ],
        compiler_params=_cparams(("parallel", "parallel", "arbitrary")),
        name="sel_attn",
    )(p, bias, p, p, hot)


def _win_attn_kernel(q_ref, k_ref, v_ref, o_ref, *, tq, span):
    start = pl.program_id(2) * tq
    rows = NSA_HPG * Q_BLOCK
    for qc in range(tq // Q_BLOCK):
        q_rows = slice(qc * Q_BLOCK, (qc + 1) * Q_BLOCK)
        q0 = start + qc * Q_BLOCK
        k0 = pl.multiple_of(jnp.maximum(q0 + Q_BLOCK - span, 0), Q_BLOCK)
        q = q_ref[0, :, q_rows, :].reshape(rows, HEAD_DIM)
        s = lax.dot_general(q, k_ref[0, 0, pl.ds(k0, span), :], _NT, preferred_element_type=F32)
        s = s.reshape(NSA_HPG, Q_BLOCK, span)
        t_q = q0 + lax.broadcasted_iota(jnp.int32, (1, Q_BLOCK, span), 1)
        k_pos = k0 + lax.broadcasted_iota(jnp.int32, (1, Q_BLOCK, span), 2)
        valid = (k_pos <= t_q) & (k_pos > t_q - WINDOW)
        s = jnp.where(valid, s, NEG_INF)
        e = jnp.exp2(s - jnp.max(s, axis=-1, keepdims=True))
        v = jnp.concatenate([v_ref[0, 0, pl.ds(k0, span), :], jnp.ones((span, LANE), BF16)], axis=1)
        o = jnp.dot(e.reshape(rows, span).astype(BF16), v, preferred_element_type=F32)
        o = o[:, :HEAD_DIM] / o[:, HEAD_DIM:]
        for h in range(NSA_HPG):
            o_ref[0, q_rows, h * HEAD_DIM:(h + 1) * HEAD_DIM] = o[h * Q_BLOCK:(h + 1) * Q_BLOCK].astype(BF16)


def _win_attn(p, batch, t_len, tq):
    span = min(WINDOW + Q_BLOCK, t_len)
    kern = functools.partial(_win_attn_kernel, tq=tq, span=span)
    return pl.pallas_call(
        kern,
        grid=(batch, NSA_GROUPS, t_len // tq),
        in_specs=[
            pl.BlockSpec((1, NSA_HPG, tq, HEAD_DIM), lambda b, g, c: (b, g, c, 0)),
            _resident((1, 1, t_len, HEAD_DIM), lambda b, g, c: (b, SLOT_KW + g, 0, 0)),
            _resident((1, 1, t_len, HEAD_DIM), lambda b, g, c: (b, SLOT_VW + g, 0, 0)),
        ],
        out_specs=pl.BlockSpec((1, tq, NSA_HPG * HEAD_DIM), lambda b, g, c: (b, c, g)),
        out_shape=jax.ShapeDtypeStruct((batch, t_len, NSA_WIDTH), BF16),
        compiler_params=_cparams(("parallel", "parallel", "parallel")),
        name="win_attn",
    )(p, p, p)


def _diff_attn_kernel(q1_ref, q2_ref, k1_ref, k2_ref, v_ref, lam_ref, sw_ref, o_ref,
                      m_ref, l_ref, acc_ref, *, tq, tk, rb, rl):
    start = pl.program_id(2) * tq
    m_ref[...] = jnp.full(m_ref.shape, NEG_INF, F32)
    l_ref[...] = jnp.zeros(l_ref.shape, F32)
    acc_ref[...] = jnp.zeros(acc_ref.shape, F32)
    q_refs = (q1_ref, q2_ref)
    k_refs = (k1_ref, k2_ref)

    def v_tile(k0, n):
        return jnp.concatenate([v_ref[0, 0, pl.ds(k0, n), :], v_ref[0, 1, pl.ds(k0, n), :]], axis=1)

    def body(j, carry):
        k0 = pl.multiple_of(j * tk, tk)
        v = v_tile(k0, tk)
        for mp in range(2):
            kt = k_refs[mp][0, 0, pl.ds(k0, tk), :]
            for r in range(tq // rl):
                rows = pl.ds(r * rl, rl)
                s = lax.dot_general(q_refs[mp][0, 0, rows, :], kt, _NT, preferred_element_type=F32)
                _flash_update(s, v, rows, m_ref.at[mp], l_ref.at[mp], acc_ref.at[mp])
        return carry

    lax.fori_loop(0, start // tk, body, 0)

    tri = _causal_triangle(rb)
    for d in range(tq // rb):
        k0 = pl.multiple_of(start + d * rb, rb)
        v = v_tile(k0, rb)
        for mp in range(2):
            kt = k_refs[mp][0, 0, pl.ds(k0, rb), :]
            for r in range(d, tq // rb):
                rows = pl.ds(r * rb, rb)
                s = lax.dot_general(q_refs[mp][0, 0, rows, :], kt, _NT, preferred_element_type=F32)
                if r == d:
                    s = jnp.where(tri, s, NEG_INF)
                _flash_update(s, v, rows, m_ref.at[mp], l_ref.at[mp], acc_ref.at[mp])

    lv = lam_ref[...]
    lam = (jnp.exp(jnp.sum(lv[0:1] * lv[1:2], axis=-1, keepdims=True))
           - jnp.exp(jnp.sum(lv[2:3] * lv[3:4], axis=-1, keepdims=True)) + LAMBDA_INIT)
    l1 = jnp.sum(l_ref[0], axis=-1, keepdims=True)
    l2 = jnp.sum(l_ref[1], axis=-1, keepdims=True)
    o = acc_ref[0] / l1 - lam * (acc_ref[1] / l2)
    y = o * lax.rsqrt(jnp.mean(o * o, axis=-1, keepdims=True) + SUBLN_EPS)
    o_ref[0] = (y * sw_ref[...] * (1.0 - LAMBDA_INIT)).astype(BF16)


def _diff_attn(p, lam4, subln_w, batch, t_len, tq, tk, rb, rl):
    assert tq % tk == 0 and tk % rb == 0 and tq % rl == 0
    kern = functools.partial(_diff_attn_kernel, tq=tq, tk=tk, rb=rb, rl=rl)
    return pl.pallas_call(
        kern,
        grid=(batch, DIFF_HEADS, t_len // tq),
        in_specs=[
            pl.BlockSpec((1, 1, tq, DIFF_QK), lambda b, h, i: (b, SLOT_QB + h, i, 0)),
            pl.BlockSpec((1, 1, tq, DIFF_QK), lambda b, h, i: (b, SLOT_QB + DIFF_HEADS + h, i, 0)),
            _resident((1, 1, t_len, DIFF_QK), lambda b, h, i: (b, SLOT_KB + h, 0, 0)),
            _resident((1, 1, t_len, DIFF_QK), lambda b, h, i: (b, SLOT_KB + DIFF_HEADS + h, 0, 0)),
            _resident((1, 2, t_len, LANE), lambda b, h, i: (b, SLOT_VB // 2 + h, 0, 0)),
            pl.BlockSpec((8, DIFF_QK), lambda b, h, i: (0, 0)),
            pl.BlockSpec((1, DIFF_V), lambda b, h, i: (0, 0)),
        ],
        out_specs=pl.BlockSpec((1, tq, DIFF_V), lambda b, h, i: (b, i, h)),
        out_shape=jax.ShapeDtypeStruct((batch, t_len, DIFF_WIDTH), BF16),
        scratch_shapes=[
            pltpu.VMEM((2, tq, LANE), F32),
            pltpu.VMEM((2, tq, LANE), F32),
            pltpu.VMEM((2, tq, DIFF_V), F32),
        ],
        compiler_params=_cparams(("parallel", "parallel", "arbitrary")),
        name="diff_attn",
    )(p, p, p, p, p, lam4, subln_w)


OUT_K_SLOTS = 2


def _out_proj_kernel(x_ref, oc_ref, os_ref, ow_ref, g_ref, za_ref, ob_ref, zb_ref, w_ref, fw_ref, o_ref):
    gates = jax.nn.sigmoid(g_ref[0])
    tm = gates.shape[0]

    def nsa_piece(hd):
        cols = slice(hd * HEAD_DIM, (hd + 1) * HEAD_DIM)
        g0 = jnp.broadcast_to(gates[:, 3 * hd:3 * hd + 1], (tm, HEAD_DIM))
        g1 = jnp.broadcast_to(gates[:, 3 * hd + 1:3 * hd + 2], (tm, HEAD_DIM))
        g2 = jnp.broadcast_to(gates[:, 3 * hd + 2:3 * hd + 3], (tm, HEAD_DIM))
        o = (g0 * oc_ref[0, :, cols].astype(F32) + g1 * os_ref[0, :, cols].astype(F32)
             + g2 * ow_ref[0, :, cols].astype(F32))
        z = za_ref[0, hd].astype(F32)
        return (o * (z * jax.nn.sigmoid(z))).astype(BF16)

    def diff_piece(sl):
        z = zb_ref[0, sl].astype(F32)
        return (ob_ref[0, :, sl * LANE:(sl + 1) * LANE].astype(F32) * (z * jax.nn.sigmoid(z))).astype(BF16)

    pieces = [functools.partial(nsa_piece, hd) for hd in range(NSA_HEADS)]
    pieces += [functools.partial(diff_piece, sl) for sl in range(DIFF_WIDTH // LANE)]
    y = x_ref[0]
    for kc in range(0, len(pieces), OUT_K_SLOTS):
        chunk = jnp.concatenate([f() for f in pieces[kc:kc + OUT_K_SLOTS]], axis=1)
        y = y + jnp.dot(chunk, w_ref[kc * LANE:(kc + OUT_K_SLOTS) * LANE, :], preferred_element_type=F32)
    y = y * lax.rsqrt(jnp.mean(y * y, axis=-1, keepdims=True) + NORM_EPS)
    o_ref[0] = y * fw_ref[...]


def _out_proj(x, o_cmp, o_sel, o_win, gl, p, o_b, w_out, final_w, batch, t_len, tm):
    row = lambda b, i: (b, i, 0)
    return pl.pallas_call(
        _out_proj_kernel,
        grid=(batch, t_len // tm),
        in_specs=[
            pl.BlockSpec((1, tm, D_MODEL), row),
            pl.BlockSpec((1, tm, NSA_WIDTH), row),
            pl.BlockSpec((1, tm, NSA_WIDTH), row),
            pl.BlockSpec((1, tm, NSA_WIDTH), row),
            pl.BlockSpec((1, tm, LANE), row),
            pl.BlockSpec((1, 8, tm, LANE), lambda b, i: (b, SLOT_ZA // 8, i, 0)),
            pl.BlockSpec((1, tm, DIFF_WIDTH), row),
            pl.BlockSpec((1, 8, tm, LANE), lambda b, i: (b, SLOT_ZB // 8, i, 0)),
            pl.BlockSpec((MIX_WIDTH, D_MODEL), lambda b, i: (0, 0)),
            pl.BlockSpec((1, D_MODEL), lambda b, i: (0, 0)),
        ],
        out_specs=pl.BlockSpec((1, tm, D_MODEL), row),
        out_shape=jax.ShapeDtypeStruct((batch, t_len, D_MODEL), F32),
        compiler_params=_cparams(("parallel", "parallel")),
        name="out_proj",
    )(x, o_cmp, o_sel, o_win, gl, p, o_b, p, w_out, final_w)


def _reorder_w_in(w):
    def heads(off, n):
        return [w[:, off + i * LANE: off + (i + 1) * LANE] for i in range(n)]
    cols = (heads(_O_QA, 8) + heads(_O_QB, 8) + heads(_O_KB, 8) + heads(_O_KC, 2) + heads(_O_KS, 2)
            + heads(_O_KW, 2) + heads(_O_VC, 2) + heads(_O_ZA, 8) + heads(_O_ZB, 8) + heads(_O_VB, 8)
            + heads(_O_VS, 2) + heads(_O_VW, 2))
    w_r = jnp.concatenate(cols, axis=1).astype(BF16)
    w_g = jnp.pad(w[:, _O_G:_O_G + NSA_HEADS * NSA_BRANCHES],
                  ((0, 0), (0, LANE - NSA_HEADS * NSA_BRANCHES))).astype(BF16)
    return w_r, w_g


def _sel_map(t_len, n_selp):
    n_rows = t_len // CMP_STRIDE
    c0 = np.arange(n_rows)[:, None] * CMP_STRIDE
    s0 = np.arange(n_selp)[None, :] * SEL_BLOCK
    ov = np.minimum(c0 + CMP_BLOCK, s0 + SEL_BLOCK) - np.maximum(c0, s0)
    m = np.maximum(ov, 0) / CMP_BLOCK
    m[n_rows - 1:] = 0.0
    m[:, t_len // SEL_BLOCK:] = 0.0
    return jnp.asarray(m.T, dtype=BF16)


def _block_one_hot(t_len):
    blk = (np.arange(t_len) // SEL_BLOCK) % LANE
    return jnp.asarray(blk[:, None] == np.arange(LANE)[None, :], dtype=BF16)


def kernel(x, norm_w, w_in, cmp_pos, cmp_k_w1, cmp_k_w2, cmp_v_w1, cmp_v_w2, lam_q1, lam_k1, lam_q2, lam_k2,
           subln_w, w_out, final_norm_w):
    batch, t_len, _ = x.shape
    n_rows = t_len // CMP_STRIDE
    n_selp = -(-(t_len // SEL_BLOCK) // LANE) * LANE

    pos = jnp.arange(t_len, dtype=F32)
    inv = ROPE_THETA ** (-jnp.arange(0, HEAD_DIM, 2, dtype=F32) / HEAD_DIM)
    ang = pos[:, None] * inv[None, :]
    cos, sin = jnp.cos(ang), jnp.sin(ang)
    cos_f = jnp.concatenate([cos, cos], axis=1)
    sin_f = jnp.concatenate([-sin, sin], axis=1)

    w_r, w_g = _reorder_w_in(w_in[0])
    tm_in = min(1024, t_len)
    p, gl = _in_proj(x.reshape(batch * t_len, D_MODEL), norm_w[0][None, :], w_r, w_g, cos_f, sin_f,
                     batch, t_len, tm_in)

    half = CMP_STRIDE * HEAD_DIM
    w1cat = jnp.stack([jnp.concatenate([w[:half], w[half:]], axis=1) for w in (cmp_k_w1[0], cmp_v_w1[0])]).astype(BF16)
    w2 = jnp.stack([cmp_k_w2[0], cmp_v_w2[0]]).astype(BF16)
    pos2 = jnp.broadcast_to(cmp_pos[0].reshape(2, 1, half), (2, 8, half)).astype(BF16)
    pc = jnp.concatenate([p[:, SLOT_KC:SLOT_KC + NSA_GROUPS], p[:, SLOT_VC:SLOT_VC + NSA_GROUPS]], axis=1)
    kvc = _compress(pc.reshape(batch, 2 * NSA_GROUPS, n_rows, half), w1cat, pos2, w2, batch, n_rows)

    o_cmp, bias = _cmp_select(p, kvc, _sel_map(t_len, n_selp), batch, t_len)
    o_sel = _sel_attn(p, bias, _block_one_hot(t_len), batch, t_len, tq=min(2048, t_len), tk=512, rb=256)
    o_win = _win_attn(p, batch, t_len, tq=512)

    lam4 = jnp.pad(jnp.stack([lam_q1[0], lam_k1[0], lam_q2[0], lam_k2[0]]), ((0, 4), (0, 0)))
    o_b = _diff_attn(p, lam4, subln_w[0][None, :], batch, t_len, tq=min(2048, t_len), tk=1024, rb=256, rl=128)

    return _out_proj(x, o_cmp, o_sel, o_win, gl, p, o_b, w_out[0].astype(BF16), final_norm_w[None, :],
                     batch, t_len, tm=min(512, t_len))
```

```python
import functools
import math

import numpy as np
import jax
import jax.numpy as jnp
from jax import lax
from jax.experimental import pallas as pl
from jax.experimental.pallas import tpu as pltpu

F32 = jnp.float32
BF16 = jnp.bfloat16

D_MODEL = 2048
HEAD_DIM = 128
NSA_HEADS = 8
NSA_GROUPS = 2
NSA_HPG = NSA_HEADS // NSA_GROUPS
NSA_WIDTH = NSA_HEADS * HEAD_DIM
NSA_KV = NSA_GROUPS * HEAD_DIM
NSA_BRANCHES = 3
CMP_BLOCK = 32
CMP_STRIDE = 16
CMP_HIDDEN = 512
SEL_BLOCK = 64
N_SELECT = 16
WINDOW = 512
DIFF_HEADS = 4
DIFF_QK = 128
DIFF_V = 256
DIFF_WIDTH = DIFF_HEADS * DIFF_V
MIX_WIDTH = NSA_WIDTH + DIFF_WIDTH
Q_BLOCK = 128
ROPE_THETA = 10000.0
NORM_EPS = 1e-6
SUBLN_EPS = 1e-5
NEG_INF = -1e30
FORCE_SCORE = 1e9
LAMBDA_INIT = 0.8 - 0.6 * math.exp(-0.3 * 0)

LANE = 128
VMEM_LIMIT = 56 * 1024 * 1024

SLOT_QA = 0
SLOT_QB = 8
SLOT_KB = 16
SLOT_KC = 24
SLOT_KS = 26
SLOT_KW = 28
N_ROPE_SLOTS = 30
SLOT_VC = 30
SLOT_ZA = 32
SLOT_ZB = 40
SLOT_VB = 48
SLOT_VS = 56
SLOT_VW = 58
N_SLOTS = 60
SLOTS_PER_STEP = 12
N_QUERY_SLOTS = 16
QK_EXP2_SCALE = (HEAD_DIM ** -0.5) * math.log2(math.e)

_OFF = np.concatenate([[0], np.cumsum([NSA_WIDTH, NSA_KV, NSA_KV, NSA_KV, NSA_KV, NSA_KV, NSA_KV,
                                       NSA_HEADS * NSA_BRANCHES, NSA_WIDTH, 1024, 1024, DIFF_WIDTH, DIFF_WIDTH])])
(_O_QA, _O_KC, _O_VC, _O_KS, _O_VS, _O_KW, _O_VW, _O_G, _O_ZA, _O_QB, _O_KB, _O_VB, _O_ZB, _O_END) = [int(v) for v in _OFF]


def _cparams(sem):
    return pltpu.CompilerParams(dimension_semantics=sem, vmem_limit_bytes=VMEM_LIMIT)


def _in_proj_kernel(x_ref, nw_ref, w_ref, wg_ref, cos_ref, sin_ref, p_ref, g_ref, h_ref):
    j = pl.program_id(1)

    @pl.when(j == 0)
    def _():
        x = x_ref[...]
        y = x * lax.rsqrt(jnp.mean(x * x, axis=-1, keepdims=True) + NORM_EPS)
        h = (y * nw_ref[...]).astype(BF16)
        h_ref[...] = h
        g_ref[0] = jnp.dot(h, wg_ref[...], preferred_element_type=F32)

    acc = jnp.dot(h_ref[...], w_ref[...], preferred_element_type=F32)

    cos_t = cos_ref[...]
    sin_t = sin_ref[...]
    for s in range(SLOTS_PER_STEP):
        slot = j * SLOTS_PER_STEP + s
        cos = jnp.where(slot < N_ROPE_SLOTS, cos_t, 1.0)
        sin = jnp.where(slot < N_ROPE_SLOTS, sin_t, 0.0)
        a = acc[:, s * LANE:(s + 1) * LANE]
        r = a * cos + pltpu.roll(a, HEAD_DIM // 2, 1) * sin
        f = jnp.where(slot < N_QUERY_SLOTS, QK_EXP2_SCALE, 1.0)
        p_ref[0, s] = (r * f).astype(BF16)


def _in_proj(x2, norm_w, w_r, w_g, cos_f, sin_f, batch, t_len, tm):
    m = x2.shape[0]
    tiles_per_seq = t_len // tm
    n_steps = N_SLOTS // SLOTS_PER_STEP
    tn = SLOTS_PER_STEP * LANE
    return pl.pallas_call(
        _in_proj_kernel,
        grid=(m // tm, n_steps),
        in_specs=[
            pl.BlockSpec((tm, D_MODEL), lambda i, j: (i, 0)),
            pl.BlockSpec((1, D_MODEL), lambda i, j: (0, 0)),
            pl.BlockSpec((D_MODEL, tn), lambda i, j: (0, j)),
            pl.BlockSpec((D_MODEL, LANE), lambda i, j: (0, 0)),
            pl.BlockSpec((tm, LANE), lambda i, j: (i % tiles_per_seq, 0)),
            pl.BlockSpec((tm, LANE), lambda i, j: (i % tiles_per_seq, 0)),
        ],
        out_specs=[
            pl.BlockSpec((1, SLOTS_PER_STEP, tm, LANE), lambda i, j: (i // tiles_per_seq, j, i % tiles_per_seq, 0)),
            pl.BlockSpec((1, tm, LANE), lambda i, j: (i // tiles_per_seq, i % tiles_per_seq, 0)),
        ],
        out_shape=[
            jax.ShapeDtypeStruct((batch, N_SLOTS, t_len, LANE), BF16),
            jax.ShapeDtypeStruct((batch, t_len, LANE), F32),
        ],
        scratch_shapes=[pltpu.VMEM((tm, D_MODEL), BF16)],
        compiler_params=_cparams(("parallel", "arbitrary")),
        name="in_proj",
    )(x2, norm_w, w_r, w_g, cos_f, sin_f)


def _compress_kernel(r_ref, w1_ref, pos_ref, w2_ref, o_ref, *, n_rows):
    w1 = w1_ref[0]
    a = jnp.dot(r_ref[0, 0], w1, preferred_element_type=F32)
    c = (jnp.dot(pos_ref[0], w1[:, :CMP_HIDDEN], preferred_element_type=F32)
         + jnp.dot(pos_ref[1], w1[:, CMP_HIDDEN:], preferred_element_type=F32))[0:1]
    nxt = pltpu.roll(a[:, CMP_HIDDEN:], n_rows - 1, 0)
    hid = a[:, :CMP_HIDDEN] + nxt + c
    hid = hid * jax.nn.sigmoid(hid)
    out = jnp.dot(hid.astype(BF16), w2_ref[0], preferred_element_type=F32)
    row = lax.broadcasted_iota(jnp.int32, out.shape, 0)
    o_ref[0, 0, 0] = jnp.where(row < n_rows - 1, out, 0.0).astype(BF16)


def _compress(p16, w1cat, pos2, w2, batch, n_rows):
    kern = functools.partial(_compress_kernel, n_rows=n_rows)

    return pl.pallas_call(
        kern,
        grid=(batch, 2, NSA_GROUPS),
        in_specs=[
            pl.BlockSpec((1, 1, n_rows, CMP_STRIDE * HEAD_DIM), lambda b, kv, g: (b, kv * NSA_GROUPS + g, 0, 0)),
            pl.BlockSpec((1, CMP_STRIDE * HEAD_DIM, 2 * CMP_HIDDEN), lambda b, kv, g: (kv, 0, 0)),
            pl.BlockSpec((2, 8, CMP_STRIDE * HEAD_DIM), lambda b, kv, g: (0, 0, 0)),
            pl.BlockSpec((1, CMP_HIDDEN, HEAD_DIM), lambda b, kv, g: (kv, 0, 0)),
        ],
        out_specs=pl.BlockSpec((1, 1, 1, n_rows, HEAD_DIM), lambda b, kv, g: (b, kv, g, 0, 0)),
        out_shape=jax.ShapeDtypeStruct((batch, 2, NSA_GROUPS, n_rows, HEAD_DIM), BF16),
        compiler_params=_cparams(("parallel", "parallel", "parallel")),
        name="compress",
    )(p16, w1cat, pos2, w2)


_NT = (((1,), (1,)), ((), ()))
CMP_PER_SEL = SEL_BLOCK // CMP_STRIDE


def _cmp_select_body(q_ref, kc_ref, vc_ref, map_ref, o_ref, bias_ref, start, g, *, n_top, w, bw):
    n_selp = map_ref.shape[0]
    q = q_ref[0, g * NSA_HPG:(g + 1) * NSA_HPG].reshape(NSA_HPG * Q_BLOCK, HEAD_DIM)
    s = lax.dot_general(q, kc_ref[0, 0, g, :w, :], _NT, preferred_element_type=F32)
    s = s.reshape(NSA_HPG, Q_BLOCK, w)
    t_q = start + lax.broadcasted_iota(jnp.int32, (1, Q_BLOCK, w), 1)
    n_id = lax.broadcasted_iota(jnp.int32, (1, Q_BLOCK, w), 2)
    valid = n_id * CMP_STRIDE + (CMP_BLOCK - 1) <= t_q
    s = jnp.where(valid, s, NEG_INF)
    e = jnp.exp2(s - jnp.max(s, axis=-1, keepdims=True))
    den = jnp.sum(e, axis=-1, keepdims=True)
    sees_any = start + lax.broadcasted_iota(jnp.int32, (1, Q_BLOCK, 1), 1) >= CMP_BLOCK - 1
    p = e * jnp.where(sees_any, 1.0 / den, 0.0)
    o = jnp.dot(p.reshape(NSA_HPG * Q_BLOCK, w).astype(BF16), vc_ref[0, 0, g, :w, :],
                preferred_element_type=F32)
    for h in range(NSA_HPG):
        hd = g * NSA_HPG + h
        o_ref[0, :, hd * HEAD_DIM:(hd + 1) * HEAD_DIM] = o[h * Q_BLOCK:(h + 1) * Q_BLOCK].astype(BF16)

    psum = p[0] + p[1] + p[2] + p[3]
    p_hi = psum.astype(BF16)
    p_lo = (psum - p_hi.astype(F32)).astype(BF16)
    smap_t = map_ref[:bw, :w]
    imp = (lax.dot_general(smap_t, p_hi, _NT, preferred_element_type=F32)
           + lax.dot_general(smap_t, p_lo, _NT, preferred_element_type=F32))
    blk = lax.broadcasted_iota(jnp.int32, (bw, Q_BLOCK), 0)
    cur = (start + lax.broadcasted_iota(jnp.int32, (bw, Q_BLOCK), 1)) // SEL_BLOCK
    forced = (blk == 0) | (blk == cur) | (blk == cur - 1)
    blk_f = blk.astype(F32)
    work = jnp.where(forced, -jnp.inf, jnp.where(blk > cur, NEG_INF, imp))
    for _ in range(n_top - 3):
        top = jnp.max(work, axis=0, keepdims=True)
        first = jnp.min(jnp.where(work == top, blk_f, 1e9), axis=0, keepdims=True)
        work = jnp.where(blk_f == first, -jnp.inf, work)
    keep = (work == -jnp.inf) & (blk <= cur)
    bias_t = jnp.where(keep, 0.0, NEG_INF)
    if bw < n_selp:
        bias_t = jnp.concatenate([bias_t, jnp.full((n_selp - bw, Q_BLOCK), NEG_INF, F32)], axis=0)
    bias_ref[0, g] = bias_t.T.astype(BF16)


def _cmp_select_kernel(q_ref, kc_ref, vc_ref, map_ref, o_ref, bias_ref, *, n_rows, n_top, chunk):
    start = pl.program_id(1) * Q_BLOCK
    variant = ((start + Q_BLOCK - 1) // SEL_BLOCK) // (chunk // CMP_PER_SEL)

    def both_groups(w):
        for g in range(NSA_GROUPS):
            _cmp_select_body(q_ref, kc_ref, vc_ref, map_ref, o_ref, bias_ref, start, g,
                             n_top=n_top, w=w, bw=min(w // CMP_PER_SEL, map_ref.shape[0]))

    for v in range(n_rows // chunk):
        pl.when(variant == v)(functools.partial(both_groups, (v + 1) * chunk))


def _cmp_select(p, kvc, sel_map, batch, t_len):
    n_rows = t_len // CMP_STRIDE
    n_selp = sel_map.shape[0]
    n_top = min(N_SELECT, t_len // SEL_BLOCK)
    assert n_top >= 3
    chunk = min(256, n_rows)
    assert n_rows % chunk == 0 and (chunk // CMP_PER_SEL) % 8 == 0
    kern = functools.partial(_cmp_select_kernel, n_rows=n_rows, n_top=n_top, chunk=chunk)
    return pl.pallas_call(
        kern,
        grid=(batch, t_len // Q_BLOCK),
        in_specs=[
            pl.BlockSpec((1, NSA_HEADS, Q_BLOCK, HEAD_DIM), lambda b, c: (b, 0, c, 0)),
            pl.BlockSpec((1, 1, NSA_GROUPS, n_rows, HEAD_DIM), lambda b, c: (b, 0, 0, 0, 0)),
            pl.BlockSpec((1, 1, NSA_GROUPS, n_rows, HEAD_DIM), lambda b, c: (b, 1, 0, 0, 0)),
            pl.BlockSpec((n_selp, n_rows), lambda b, c: (0, 0)),
        ],
        out_specs=[
            pl.BlockSpec((1, Q_BLOCK, NSA_WIDTH), lambda b, c: (b, c, 0)),
            pl.BlockSpec((1, NSA_GROUPS, Q_BLOCK, n_selp), lambda b, c: (b, 0, c, 0)),
        ],
        out_shape=[
            jax.ShapeDtypeStruct((batch, t_len, NSA_WIDTH), BF16),
            jax.ShapeDtypeStruct((batch, NSA_GROUPS, t_len, n_selp), BF16),
        ],
        compiler_params=_cparams(("parallel", "parallel")),
        name="cmp_select",
    )(p, kvc, kvc, sel_map)


def _lane_tile(x, n):
    return x if n == 1 else jnp.concatenate([x] * n, axis=1)


def _flash_update(s, v, rows, m_ref, l_ref, acc_ref):
    n_keys = s.shape[1]
    dv = acc_ref.shape[-1]
    m_prev = m_ref[rows, :]
    m_new = jnp.maximum(m_prev, jnp.max(s, axis=-1, keepdims=True))
    alpha = jnp.exp2(m_prev - m_new)
    p = jnp.exp2(s - _lane_tile(m_new, n_keys // LANE))
    if l_ref is not None:
        psum = p[:, :LANE]
        for cc in range(1, n_keys // LANE):
            psum = psum + p[:, cc * LANE:(cc + 1) * LANE]
        l_ref[rows, :] = alpha * l_ref[rows, :] + psum
    acc_ref[rows, :] = (_lane_tile(alpha, dv // LANE) * acc_ref[rows, :]
                        + jnp.dot(p.astype(BF16), v, preferred_element_type=F32))
    m_ref[rows, :] = m_new


def _causal_triangle(rb):
    return lax.broadcasted_iota(jnp.int32, (rb, rb), 1) <= lax.broadcasted_iota(jnp.int32, (rb, rb), 0)


def _sel_attn_kernel(q_ref, bias_ref, k_ref, v_ref, hot_ref, o_ref, qa_ref, m_ref, acc_ref,
                     *, tq, tk, rb, n_halves):
    start = pl.program_id(2) * tq
    n_rows = NSA_HPG * tq
    keys_per_half = LANE * SEL_BLOCK
    for hf in range(n_halves):
        for h in range(NSA_HPG):
            qa_ref[hf, h * tq:(h + 1) * tq, :LANE] = q_ref[0, h]
            qa_ref[hf, h * tq:(h + 1) * tq, LANE:] = bias_ref[0, 0, :, hf * LANE:(hf + 1) * LANE]
    m_ref[...] = jnp.full(m_ref.shape, NEG_INF, F32)
    acc_ref[...] = jnp.zeros(acc_ref.shape, F32)

    def k_aug(k0, n):
        return jnp.concatenate([k_ref[0, 0, pl.ds(k0, n), :], hot_ref[pl.ds(k0, n), :]], axis=1)

    def v_aug(k0, n):
        return jnp.concatenate([v_ref[0, 0, pl.ds(k0, n), :], jnp.ones((n, LANE), BF16)], axis=1)

    def body(j, carry):
        k0 = pl.multiple_of(j * tk, tk)
        ka, v, hf = k_aug(k0, tk), v_aug(k0, tk), k0 // keys_per_half
        for r in range(n_rows // rb):
            rows = pl.ds(r * rb, rb)
            s = lax.dot_general(qa_ref[hf, rows, :], ka, _NT, preferred_element_type=F32)
            _flash_update(s, v, rows, m_ref, None, acc_ref)
        return carry

    lax.fori_loop(0, start // tk, body, 0)

    tri = _causal_triangle(rb)
    for d in range(tq // rb):
        k0 = pl.multiple_of(start + d * rb, rb)
        ka, v, hf = k_aug(k0, rb), v_aug(k0, rb), k0 // keys_per_half
        for r in range(n_rows // rb):
            q_off = (r * rb) % tq
            if d * rb > q_off:
                continue
            rows = pl.ds(r * rb, rb)
            s = lax.dot_general(qa_ref[hf, rows, :], ka, _NT, preferred_element_type=F32)
            if d * rb == q_off:
                s = jnp.where(tri, s, NEG_INF)
            _flash_update(s, v, rows, m_ref, None, acc_ref)

    for h in range(NSA_HPG):
        hr = slice(h * tq, (h + 1) * tq)
        o_ref[0, :, h * HEAD_DIM:(h + 1) * HEAD_DIM] = (acc_ref[hr, :HEAD_DIM] / acc_ref[hr, HEAD_DIM:]).astype(BF16)


def _resident(block_shape, index_map):
    return pl.BlockSpec(block_shape, index_map, pipeline_mode=pl.Buffered(1))


def _sel_attn(p, bias, hot, batch, t_len, tq, tk, rb):
    assert tq % tk == 0 and tk % rb == 0 and (LANE * SEL_BLOCK) % tk == 0
    n_selp = bias.shape[-1]
    n_halves = n_selp // LANE
    n_rows = NSA_HPG * tq
    kern = functools.partial(_sel_attn_kernel, tq=tq, tk=tk, rb=rb, n_halves=n_halves)
    return pl.pallas_call(
        kern,
        grid=(batch, NSA_GROUPS, t_len // tq),
        in_specs=[
            pl.BlockSpec((1, NSA_HPG, tq, HEAD_DIM), lambda b, g, c: (b, g, c, 0)),
            pl.BlockSpec((1, 1, tq, n_selp), lambda b, g, c: (b, g, c, 0)),
            _resident((1, 1, t_len, HEAD_DIM), lambda b, g, c: (b, SLOT_KS + g, 0, 0)),
            _resident((1, 1, t_len, HEAD_DIM), lambda b, g, c: (b, SLOT_VS + g, 0, 0)),
            _resident((t_len, LANE), lambda b, g, c: (0, 0)),
        ],
        out_specs=pl.BlockSpec((1, tq, NSA_HPG * HEAD_DIM), lambda b, g, c: (b, c, g)),
        out_shape=jax.ShapeDtypeStruct((batch, t_len, NSA_WIDTH), BF16),
        scratch_shapes=[
            pltpu.VMEM((n_halves, n_rows, 2 * LANE), BF16),
            pltpu.VMEM((n_rows, LANE), F32),
            pltpu.VMEM((n_rows, 2 * HEAD_DIM), F32),
        ],
        compiler_params=_cparams(("parallel", "parallel", "arbitrary")),
        name="sel_attn",
    )(p, bias, p, p, hot)


def _win_attn_kernel(q_ref, k_ref, v_ref, o_ref, *, tq, span):
    start = pl.program_id(1) * tq
    rows = NSA_HPG * Q_BLOCK
    for qc in range(tq // Q_BLOCK):
        q_rows = slice(qc * Q_BLOCK, (qc + 1) * Q_BLOCK)
        q0 = start + qc * Q_BLOCK
        k0 = pl.multiple_of(jnp.maximum(q0 + Q_BLOCK - span, 0), Q_BLOCK)
        t_q = q0 + lax.broadcasted_iota(jnp.int32, (1, Q_BLOCK, span), 1)
        k_pos = k0 + lax.broadcasted_iota(jnp.int32, (1, Q_BLOCK, span), 2)
        valid = (k_pos <= t_q) & (k_pos > t_q - WINDOW)
        for g in range(NSA_GROUPS):
            q = q_ref[0, g * NSA_HPG:(g + 1) * NSA_HPG, q_rows, :].reshape(rows, HEAD_DIM)
            s = lax.dot_general(q, k_ref[0, g, pl.ds(k0, span), :], _NT, preferred_element_type=F32)
            s = jnp.where(valid, s.reshape(NSA_HPG, Q_BLOCK, span), NEG_INF)
            e = jnp.exp2(s - jnp.max(s, axis=-1, keepdims=True))
            v = jnp.concatenate([v_ref[0, g, pl.ds(k0, span), :], jnp.ones((span, LANE), BF16)], axis=1)
            o = jnp.dot(e.reshape(rows, span).astype(BF16), v, preferred_element_type=F32)
            o = o[:, :HEAD_DIM] / o[:, HEAD_DIM:]
            for h in range(NSA_HPG):
                hd = g * NSA_HPG + h
                o_ref[0, q_rows, hd * HEAD_DIM:(hd + 1) * HEAD_DIM] = o[h * Q_BLOCK:(h + 1) * Q_BLOCK].astype(BF16)


def _win_attn(p, batch, t_len, tq):
    span = min(WINDOW + Q_BLOCK, t_len)
    kern = functools.partial(_win_attn_kernel, tq=tq, span=span)
    return pl.pallas_call(
        kern,
        grid=(batch, t_len // tq),
        in_specs=[
            pl.BlockSpec((1, NSA_HEADS, tq, HEAD_DIM), lambda b, c: (b, 0, c, 0)),
            _resident((1, NSA_GROUPS, t_len, HEAD_DIM), lambda b, c: (b, SLOT_KW // NSA_GROUPS, 0, 0)),
            _resident((1, NSA_GROUPS, t_len, HEAD_DIM), lambda b, c: (b, SLOT_VW // NSA_GROUPS, 0, 0)),
        ],
        out_specs=pl.BlockSpec((1, tq, NSA_WIDTH), lambda b, c: (b, c, 0)),
        out_shape=jax.ShapeDtypeStruct((batch, t_len, NSA_WIDTH), BF16),
        compiler_params=_cparams(("parallel", "parallel")),
        name="win_attn",
    )(p, p, p)


def _diff_attn_kernel(q1_ref, q2_ref, k1_ref, k2_ref, v_ref, lam_ref, sw_ref, o_ref,
                      m_ref, l_ref, acc_ref, *, tq, tk, rb, rl):
    start = pl.program_id(2) * tq
    m_ref[...] = jnp.full(m_ref.shape, NEG_INF, F32)
    l_ref[...] = jnp.zeros(l_ref.shape, F32)
    acc_ref[...] = jnp.zeros(acc_ref.shape, F32)
    q_refs = (q1_ref, q2_ref)
    k_refs = (k1_ref, k2_ref)

    def v_tile(k0, n):
        return jnp.concatenate([v_ref[0, 0, pl.ds(k0, n), :], v_ref[0, 1, pl.ds(k0, n), :]], axis=1)

    def body(j, carry):
        k0 = pl.multiple_of(j * tk, tk)
        v = v_tile(k0, tk)
        for mp in range(2):
            kt = k_refs[mp][0, 0, pl.ds(k0, tk), :]
            for r in range(tq // rl):
                rows = pl.ds(r * rl, rl)
                s = lax.dot_general(q_refs[mp][0, 0, rows, :], kt, _NT, preferred_element_type=F32)
                _flash_update(s, v, rows, m_ref.at[mp], l_ref.at[mp], acc_ref.at[mp])
        return carry

    lax.fori_loop(0, start // tk, body, 0)

    tri = _causal_triangle(rb)
    for d in range(tq // rb):
        k0 = pl.multiple_of(start + d * rb, rb)
        v = v_tile(k0, rb)
        for mp in range(2):
            kt = k_refs[mp][0, 0, pl.ds(k0, rb), :]
            for r in range(d, tq // rb):
                rows = pl.ds(r * rb, rb)
                s = lax.dot_general(q_refs[mp][0, 0, rows, :], kt, _NT, preferred_element_type=F32)
                if r == d:
                    s = jnp.where(tri, s, NEG_INF)
                _flash_update(s, v, rows, m_ref.at[mp], l_ref.at[mp], acc_ref.at[mp])

    lv = lam_ref[...]
    lam = (jnp.exp(jnp.sum(lv[0:1] * lv[1:2], axis=-1, keepdims=True))
           - jnp.exp(jnp.sum(lv[2:3] * lv[3:4], axis=-1, keepdims=True)) + LAMBDA_INIT)
    l1 = jnp.sum(l_ref[0], axis=-1, keepdims=True)
    l2 = jnp.sum(l_ref[1], axis=-1, keepdims=True)
    o = acc_ref[0] / l1 - lam * (acc_ref[1] / l2)
    y = o * lax.rsqrt(jnp.mean(o * o, axis=-1, keepdims=True) + SUBLN_EPS)
    o_ref[0] = (y * sw_ref[...] * (1.0 - LAMBDA_INIT)).astype(BF16)


def _diff_attn(p, lam4, subln_w, batch, t_len, tq, tk, rb, rl):
    assert tq % tk == 0 and tk % rb == 0 and tq % rl == 0
    kern = functools.partial(_diff_attn_kernel, tq=tq, tk=tk, rb=rb, rl=rl)
    return pl.pallas_call(
        kern,
        grid=(batch, DIFF_HEADS, t_len // tq),
        in_specs=[
            pl.BlockSpec((1, 1, tq, DIFF_QK), lambda b, h, i: (b, SLOT_QB + h, i, 0)),
            pl.BlockSpec((1, 1, tq, DIFF_QK), lambda b, h, i: (b, SLOT_QB + DIFF_HEADS + h, i, 0)),
            _resident((1, 1, t_len, DIFF_QK), lambda b, h, i: (b, SLOT_KB + h, 0, 0)),
            _resident((1, 1, t_len, DIFF_QK), lambda b, h, i: (b, SLOT_KB + DIFF_HEADS + h, 0, 0)),
            _resident((1, 2, t_len, LANE), lambda b, h, i: (b, SLOT_VB // 2 + h, 0, 0)),
            pl.BlockSpec((8, DIFF_QK), lambda b, h, i: (0, 0)),
            pl.BlockSpec((1, DIFF_V), lambda b, h, i: (0, 0)),
        ],
        out_specs=pl.BlockSpec((1, tq, DIFF_V), lambda b, h, i: (b, i, h)),
        out_shape=jax.ShapeDtypeStruct((batch, t_len, DIFF_WIDTH), BF16),
        scratch_shapes=[
            pltpu.VMEM((2, tq, LANE), F32),
            pltpu.VMEM((2, tq, LANE), F32),
            pltpu.VMEM((2, tq, DIFF_V), F32),
        ],
        compiler_params=_cparams(("parallel", "parallel", "arbitrary")),
        name="diff_attn",
    )(p, p, p, p, p, lam4, subln_w)


OUT_K_SLOTS = 2


def _out_proj_kernel(x_ref, oc_ref, os_ref, ow_ref, g_ref, za_ref, ob_ref, zb_ref, w_ref, fw_ref, o_ref):
    gates = jax.nn.sigmoid(g_ref[0])
    tm = gates.shape[0]

    def nsa_piece(hd):
        cols = slice(hd * HEAD_DIM, (hd + 1) * HEAD_DIM)
        g0 = jnp.broadcast_to(gates[:, 3 * hd:3 * hd + 1], (tm, HEAD_DIM))
        g1 = jnp.broadcast_to(gates[:, 3 * hd + 1:3 * hd + 2], (tm, HEAD_DIM))
        g2 = jnp.broadcast_to(gates[:, 3 * hd + 2:3 * hd + 3], (tm, HEAD_DIM))
        o = (g0 * oc_ref[0, :, cols].astype(F32) + g1 * os_ref[0, :, cols].astype(F32)
             + g2 * ow_ref[0, :, cols].astype(F32))
        z = za_ref[0, hd].astype(F32)
        return (o * (z * jax.nn.sigmoid(z))).astype(BF16)

    def diff_piece(sl):
        z = zb_ref[0, sl].astype(F32)
        return (ob_ref[0, :, sl * LANE:(sl + 1) * LANE].astype(F32) * (z * jax.nn.sigmoid(z))).astype(BF16)

    pieces = [functools.partial(nsa_piece, hd) for hd in range(NSA_HEADS)]
    pieces += [functools.partial(diff_piece, sl) for sl in range(DIFF_WIDTH // LANE)]
    y = x_ref[0]
    for kc in range(0, len(pieces), OUT_K_SLOTS):
        chunk = jnp.concatenate([f() for f in pieces[kc:kc + OUT_K_SLOTS]], axis=1)
        y = y + jnp.dot(chunk, w_ref[kc * LANE:(kc + OUT_K_SLOTS) * LANE, :], preferred_element_type=F32)
    y = y * lax.rsqrt(jnp.mean(y * y, axis=-1, keepdims=True) + NORM_EPS)
    o_ref[0] = y * fw_ref[...]


def _out_proj(x, o_cmp, o_sel, o_win, gl, p, o_b, w_out, final_w, batch, t_len, tm):
    row = lambda b, i: (b, i, 0)
    return pl.pallas_call(
        _out_proj_kernel,
        grid=(batch, t_len // tm),
        in_specs=[
            pl.BlockSpec((1, tm, D_MODEL), row),
            pl.BlockSpec((1, tm, NSA_WIDTH), row),
            pl.BlockSpec((1, tm, NSA_WIDTH), row),
            pl.BlockSpec((1, tm, NSA_WIDTH), row),
            pl.BlockSpec((1, tm, LANE), row),
            pl.BlockSpec((1, 8, tm, LANE), lambda b, i: (b, SLOT_ZA // 8, i, 0)),
            pl.BlockSpec((1, tm, DIFF_WIDTH), row),
            pl.BlockSpec((1, 8, tm, LANE), lambda b, i: (b, SLOT_ZB // 8, i, 0)),
            pl.BlockSpec((MIX_WIDTH, D_MODEL), lambda b, i: (0, 0)),
            pl.BlockSpec((1, D_MODEL), lambda b, i: (0, 0)),
        ],
        out_specs=pl.BlockSpec((1, tm, D_MODEL), row),
        out_shape=jax.ShapeDtypeStruct((batch, t_len, D_MODEL), F32),
        compiler_params=_cparams(("parallel", "parallel")),
        name="out_proj",
    )(x, o_cmp, o_sel, o_win, gl, p, o_b, p, w_out, final_w)


def _reorder_w_in(w):
    def heads(off, n):
        return [w[:, off + i * LANE: off + (i + 1) * LANE] for i in range(n)]
    cols = (heads(_O_QA, 8) + heads(_O_QB, 8) + heads(_O_KB, 8) + heads(_O_KC, 2) + heads(_O_KS, 2)
            + heads(_O_KW, 2) + heads(_O_VC, 2) + heads(_O_ZA, 8) + heads(_O_ZB, 8) + heads(_O_VB, 8)
            + heads(_O_VS, 2) + heads(_O_VW, 2))
    w_r = jnp.concatenate(cols, axis=1).astype(BF16)
    w_g = jnp.pad(w[:, _O_G:_O_G + NSA_HEADS * NSA_BRANCHES],
                  ((0, 0), (0, LANE - NSA_HEADS * NSA_BRANCHES))).astype(BF16)
    return w_r, w_g


def _sel_map(t_len, n_selp):
    n_rows = t_len // CMP_STRIDE
    c0 = np.arange(n_rows)[:, None] * CMP_STRIDE
    s0 = np.arange(n_selp)[None, :] * SEL_BLOCK
    ov = np.minimum(c0 + CMP_BLOCK, s0 + SEL_BLOCK) - np.maximum(c0, s0)
    m = np.maximum(ov, 0) / CMP_BLOCK
    m[n_rows - 1:] = 0.0
    m[:, t_len // SEL_BLOCK:] = 0.0
    return jnp.asarray(m.T, dtype=BF16)


def _block_one_hot(t_len):
    blk = (np.arange(t_len) // SEL_BLOCK) % LANE
    return jnp.asarray(blk[:, None] == np.arange(LANE)[None, :], dtype=BF16)


def kernel(x, norm_w, w_in, cmp_pos, cmp_k_w1, cmp_k_w2, cmp_v_w1, cmp_v_w2, lam_q1, lam_k1, lam_q2, lam_k2,
           subln_w, w_out, final_norm_w):
    batch, t_len, _ = x.shape
    n_rows = t_len // CMP_STRIDE
    n_selp = -(-(t_len // SEL_BLOCK) // LANE) * LANE

    pos = jnp.arange(t_len, dtype=F32)
    inv = ROPE_THETA ** (-jnp.arange(0, HEAD_DIM, 2, dtype=F32) / HEAD_DIM)
    ang = pos[:, None] * inv[None, :]
    cos, sin = jnp.cos(ang), jnp.sin(ang)
    cos_f = jnp.concatenate([cos, cos], axis=1)
    sin_f = jnp.concatenate([-sin, sin], axis=1)

    w_r, w_g = _reorder_w_in(w_in[0])
    tm_in = min(1024, t_len)
    p, gl = _in_proj(x.reshape(batch * t_len, D_MODEL), norm_w[0][None, :], w_r, w_g, cos_f, sin_f,
                     batch, t_len, tm_in)

    half = CMP_STRIDE * HEAD_DIM
    w1cat = jnp.stack([jnp.concatenate([w[:half], w[half:]], axis=1) for w in (cmp_k_w1[0], cmp_v_w1[0])]).astype(BF16)
    w2 = jnp.stack([cmp_k_w2[0], cmp_v_w2[0]]).astype(BF16)
    pos2 = jnp.broadcast_to(cmp_pos[0].reshape(2, 1, half), (2, 8, half)).astype(BF16)
    pc = jnp.concatenate([p[:, SLOT_KC:SLOT_KC + NSA_GROUPS], p[:, SLOT_VC:SLOT_VC + NSA_GROUPS]], axis=1)
    kvc = _compress(pc.reshape(batch, 2 * NSA_GROUPS, n_rows, half), w1cat, pos2, w2, batch, n_rows)

    o_cmp, bias = _cmp_select(p, kvc, _sel_map(t_len, n_selp), batch, t_len)
    o_sel = _sel_attn(p, bias, _block_one_hot(t_len), batch, t_len, tq=min(2048, t_len), tk=512, rb=256)
    o_win = _win_attn(p, batch, t_len, tq=512)

    lam4 = jnp.pad(jnp.stack([lam_q1[0], lam_k1[0], lam_q2[0], lam_k2[0]]), ((0, 4), (0, 0)))
    o_b = _diff_attn(p, lam4, subln_w[0][None, :], batch, t_len, tq=min(2048, t_len), tk=1024, rb=256, rl=128)

    return _out_proj(x, o_cmp, o_sel, o_win, gl, p, o_b, w_out[0].astype(BF16), final_norm_w[None, :],
                     batch, t_len, tm=min(512, t_len))
```

```python
import functools
import math

import numpy as np
import jax
import jax.numpy as jnp
from jax import lax
from jax.experimental import pallas as pl
from jax.experimental.pallas import tpu as pltpu

F32 = jnp.float32
BF16 = jnp.bfloat16

D_MODEL = 2048
HEAD_DIM = 128
NSA_HEADS = 8
NSA_GROUPS = 2
NSA_HPG = NSA_HEADS // NSA_GROUPS
NSA_WIDTH = NSA_HEADS * HEAD_DIM
NSA_KV = NSA_GROUPS * HEAD_DIM
NSA_BRANCHES = 3
CMP_BLOCK = 32
CMP_STRIDE = 16
CMP_HIDDEN = 512
SEL_BLOCK = 64
N_SELECT = 16
WINDOW = 512
DIFF_HEADS = 4
DIFF_QK = 128
DIFF_V = 256
DIFF_WIDTH = DIFF_HEADS * DIFF_V
MIX_WIDTH = NSA_WIDTH + DIFF_WIDTH
Q_BLOCK = 128
ROPE_THETA = 10000.0
NORM_EPS = 1e-6
SUBLN_EPS = 1e-5
NEG_INF = -1e30
FORCE_SCORE = 1e9
LAMBDA_INIT = 0.8 - 0.6 * math.exp(-0.3 * 0)

LANE = 128
VMEM_LIMIT = 56 * 1024 * 1024

SLOT_QA = 0
SLOT_QB = 8
SLOT_KB = 16
SLOT_KC = 24
SLOT_KS = 26
SLOT_KW = 28
N_ROPE_SLOTS = 30
SLOT_VC = 30
SLOT_ZA = 32
SLOT_ZB = 40
SLOT_VB = 48
SLOT_VS = 56
SLOT_VW = 58
N_SLOTS = 60
SLOTS_PER_STEP = 12
N_QUERY_SLOTS = 16
QK_EXP2_SCALE = (HEAD_DIM ** -0.5) * math.log2(math.e)

_OFF = np.concatenate([[0], np.cumsum([NSA_WIDTH, NSA_KV, NSA_KV, NSA_KV, NSA_KV, NSA_KV, NSA_KV,
                                       NSA_HEADS * NSA_BRANCHES, NSA_WIDTH, 1024, 1024, DIFF_WIDTH, DIFF_WIDTH])])
(_O_QA, _O_KC, _O_VC, _O_KS, _O_VS, _O_KW, _O_VW, _O_G, _O_ZA, _O_QB, _O_KB, _O_VB, _O_ZB, _O_END) = [int(v) for v in _OFF]


def _cparams(sem):
    return pltpu.CompilerParams(dimension_semantics=sem, vmem_limit_bytes=VMEM_LIMIT)


def _in_proj_kernel(x_ref, nw_ref, w_ref, wg_ref, cos_ref, sin_ref, p_ref, g_ref, c_ref, h_ref, stage_ref):
    j = pl.program_id(1)

    @pl.when(j == 0)
    def _():
        x = x_ref[...]
        y = x * lax.rsqrt(jnp.mean(x * x, axis=-1, keepdims=True) + NORM_EPS)
        h = (y * nw_ref[...]).astype(BF16)
        h_ref[...] = h
        g_ref[0] = jnp.dot(h, wg_ref[...], preferred_element_type=F32)

    acc = jnp.dot(h_ref[...], w_ref[...], preferred_element_type=F32)

    cos_t = cos_ref[...]
    sin_t = sin_ref[...]
    for s in range(SLOTS_PER_STEP):
        slot = j * SLOTS_PER_STEP + s
        cos = jnp.where(slot < N_ROPE_SLOTS, cos_t, 1.0)
        sin = jnp.where(slot < N_ROPE_SLOTS, sin_t, 0.0)
        a = acc[:, s * LANE:(s + 1) * LANE]
        r = a * cos + pltpu.roll(a, HEAD_DIM // 2, 1) * sin
        f = jnp.where(slot < N_QUERY_SLOTS, QK_EXP2_SCALE, 1.0)
        p_ref[0, s] = (r * f).astype(BF16)

    @pl.when(j == SLOT_KC // SLOTS_PER_STEP)
    def _():
        n_out = stage_ref.shape[0] // CMP_STRIDE
        for n, slot in enumerate((SLOT_KC, SLOT_KC + 1, SLOT_VC, SLOT_VC + 1)):
            stage_ref[...] = p_ref[0, slot % SLOTS_PER_STEP].astype(F32)
            for l in range(CMP_STRIDE):
                c_ref[0, n, :, l * LANE:(l + 1) * LANE] = stage_ref[pl.ds(l, n_out, stride=CMP_STRIDE), :].astype(BF16)


def _in_proj(x2, norm_w, w_r, w_g, cos_f, sin_f, batch, t_len, tm):
    m = x2.shape[0]
    tiles_per_seq = t_len // tm
    assert SLOT_KC // SLOTS_PER_STEP == (SLOT_VC + 1) // SLOTS_PER_STEP
    n_steps = N_SLOTS // SLOTS_PER_STEP
    tn = SLOTS_PER_STEP * LANE
    return pl.pallas_call(
        _in_proj_kernel,
        grid=(m // tm, n_steps),
        in_specs=[
            pl.BlockSpec((tm, D_MODEL), lambda i, j: (i, 0)),
            pl.BlockSpec((1, D_MODEL), lambda i, j: (0, 0)),
            pl.BlockSpec((D_MODEL, tn), lambda i, j: (0, j)),
            pl.BlockSpec((D_MODEL, LANE), lambda i, j: (0, 0)),
            pl.BlockSpec((tm, LANE), lambda i, j: (i % tiles_per_seq, 0)),
            pl.BlockSpec((tm, LANE), lambda i, j: (i % tiles_per_seq, 0)),
        ],
        out_specs=[
            pl.BlockSpec((1, SLOTS_PER_STEP, tm, LANE), lambda i, j: (i // tiles_per_seq, j, i % tiles_per_seq, 0)),
            pl.BlockSpec((1, tm, LANE), lambda i, j: (i // tiles_per_seq, i % tiles_per_seq, 0)),
            pl.BlockSpec((1, 2 * NSA_GROUPS, tm // CMP_STRIDE, CMP_STRIDE * HEAD_DIM),
                         lambda i, j: (i // tiles_per_seq, 0, i % tiles_per_seq, 0)),
        ],
        out_shape=[
            jax.ShapeDtypeStruct((batch, N_SLOTS, t_len, LANE), BF16),
            jax.ShapeDtypeStruct((batch, t_len, LANE), F32),
            jax.ShapeDtypeStruct((batch, 2 * NSA_GROUPS, t_len // CMP_STRIDE, CMP_STRIDE * HEAD_DIM), BF16),
        ],
        scratch_shapes=[pltpu.VMEM((tm, D_MODEL), BF16), pltpu.VMEM((tm, LANE), F32)],
        compiler_params=_cparams(("parallel", "arbitrary")),
        name="in_proj",
    )(x2, norm_w, w_r, w_g, cos_f, sin_f)


def _compress_kernel(r_ref, w1_ref, pos_ref, w2_ref, o_ref, *, n_rows):
    w1 = w1_ref[0]
    a = jnp.dot(r_ref[0, 0], w1, preferred_element_type=F32)
    c = (jnp.dot(pos_ref[0], w1[:, :CMP_HIDDEN], preferred_element_type=F32)
         + jnp.dot(pos_ref[1], w1[:, CMP_HIDDEN:], preferred_element_type=F32))[0:1]
    nxt = pltpu.roll(a[:, CMP_HIDDEN:], n_rows - 1, 0)
    hid = a[:, :CMP_HIDDEN] + nxt + c
    hid = hid * jax.nn.sigmoid(hid)
    out = jnp.dot(hid.astype(BF16), w2_ref[0], preferred_element_type=F32)
    row = lax.broadcasted_iota(jnp.int32, out.shape, 0)
    o_ref[0, 0, 0] = jnp.where(row < n_rows - 1, out, 0.0).astype(BF16)


def _compress(p16, w1cat, pos2, w2, batch, n_rows):
    kern = functools.partial(_compress_kernel, n_rows=n_rows)

    return pl.pallas_call(
        kern,
        grid=(batch, 2, NSA_GROUPS),
        in_specs=[
            pl.BlockSpec((1, 1, n_rows, CMP_STRIDE * HEAD_DIM), lambda b, kv, g: (b, kv * NSA_GROUPS + g, 0, 0)),
            pl.BlockSpec((1, CMP_STRIDE * HEAD_DIM, 2 * CMP_HIDDEN), lambda b, kv, g: (kv, 0, 0)),
            pl.BlockSpec((2, 8, CMP_STRIDE * HEAD_DIM), lambda b, kv, g: (0, 0, 0)),
            pl.BlockSpec((1, CMP_HIDDEN, HEAD_DIM), lambda b, kv, g: (kv, 0, 0)),
        ],
        out_specs=pl.BlockSpec((1, 1, 1, n_rows, HEAD_DIM), lambda b, kv, g: (b, kv, g, 0, 0)),
        out_shape=jax.ShapeDtypeStruct((batch, 2, NSA_GROUPS, n_rows, HEAD_DIM), BF16),
        compiler_params=_cparams(("parallel", "parallel", "parallel")),
        name="compress",
    )(p16, w1cat, pos2, w2)


_NT = (((1,), (1,)), ((), ()))
CMP_PER_SEL = SEL_BLOCK // CMP_STRIDE


def _cmp_select_body(q_ref, kc_ref, vc_ref, map_ref, o_ref, bias_ref, start, g, *, n_top, w, bw):
    n_selp = map_ref.shape[0]
    q = q_ref[0, g * NSA_HPG:(g + 1) * NSA_HPG].reshape(NSA_HPG * Q_BLOCK, HEAD_DIM)
    s = lax.dot_general(q, kc_ref[0, 0, g, :w, :], _NT, preferred_element_type=F32)
    s = s.reshape(NSA_HPG, Q_BLOCK, w)
    t_q = start + lax.broadcasted_iota(jnp.int32, (1, Q_BLOCK, w), 1)
    n_id = lax.broadcasted_iota(jnp.int32, (1, Q_BLOCK, w), 2)
    valid = n_id * CMP_STRIDE + (CMP_BLOCK - 1) <= t_q
    s = jnp.where(valid, s, NEG_INF)
    e = jnp.exp2(s - jnp.max(s, axis=-1, keepdims=True))
    den = jnp.sum(e, axis=-1, keepdims=True)
    sees_any = start + lax.broadcasted_iota(jnp.int32, (1, Q_BLOCK, 1), 1) >= CMP_BLOCK - 1
    p = e * jnp.where(sees_any, 1.0 / den, 0.0)
    o = jnp.dot(p.reshape(NSA_HPG * Q_BLOCK, w).astype(BF16), vc_ref[0, 0, g, :w, :],
                preferred_element_type=F32)
    for h in range(NSA_HPG):
        hd = g * NSA_HPG + h
        o_ref[0, :, hd * HEAD_DIM:(hd + 1) * HEAD_DIM] = o[h * Q_BLOCK:(h + 1) * Q_BLOCK].astype(BF16)

    psum = p[0] + p[1] + p[2] + p[3]
    p_hi = psum.astype(BF16)
    p_lo = (psum - p_hi.astype(F32)).astype(BF16)
    smap_t = map_ref[:bw, :w]
    imp = (lax.dot_general(smap_t, p_hi, _NT, preferred_element_type=F32)
           + lax.dot_general(smap_t, p_lo, _NT, preferred_element_type=F32))
    blk = lax.broadcasted_iota(jnp.int32, (bw, Q_BLOCK), 0)
    cur = (start + lax.broadcasted_iota(jnp.int32, (bw, Q_BLOCK), 1)) // SEL_BLOCK
    forced = (blk == 0) | (blk == cur) | (blk == cur - 1)
    blk_f = blk.astype(F32)
    work = jnp.where(forced, -jnp.inf, jnp.where(blk > cur, NEG_INF, imp))
    for _ in range(n_top - 3):
        top = jnp.max(work, axis=0, keepdims=True)
        first = jnp.min(jnp.where(work == top, blk_f, 1e9), axis=0, keepdims=True)
        work = jnp.where(blk_f == first, -jnp.inf, work)
    keep = (work == -jnp.inf) & (blk <= cur)
    bias_t = jnp.where(keep, 0.0, NEG_INF)
    if bw < n_selp:
        bias_t = jnp.concatenate([bias_t, jnp.full((n_selp - bw, Q_BLOCK), NEG_INF, F32)], axis=0)
    bias_ref[0, g] = bias_t.T.astype(BF16)


def _cmp_select_kernel(q_ref, kc_ref, vc_ref, map_ref, o_ref, bias_ref, *, n_rows, n_top, chunk):
    start = pl.program_id(1) * Q_BLOCK
    variant = ((start + Q_BLOCK - 1) // SEL_BLOCK) // (chunk // CMP_PER_SEL)

    def both_groups(w):
        for g in range(NSA_GROUPS):
            _cmp_select_body(q_ref, kc_ref, vc_ref, map_ref, o_ref, bias_ref, start, g,
                             n_top=n_top, w=w, bw=min(w // CMP_PER_SEL, map_ref.shape[0]))

    for v in range(n_rows // chunk):
        pl.when(variant == v)(functools.partial(both_groups, (v + 1) * chunk))


def _cmp_select(p, kvc, sel_map, batch, t_len):
    n_rows = t_len // CMP_STRIDE
    n_selp = sel_map.shape[0]
    n_top = min(N_SELECT, t_len // SEL_BLOCK)
    assert n_top >= 3
    chunk = min(256, n_rows)
    assert n_rows % chunk == 0 and (chunk // CMP_PER_SEL) % 8 == 0
    kern = functools.partial(_cmp_select_kernel, n_rows=n_rows, n_top=n_top, chunk=chunk)
    return pl.pallas_call(
        kern,
        grid=(batch, t_len // Q_BLOCK),
        in_specs=[
            pl.BlockSpec((1, NSA_HEADS, Q_BLOCK, HEAD_DIM), lambda b, c: (b, 0, c, 0)),
            pl.BlockSpec((1, 1, NSA_GROUPS, n_rows, HEAD_DIM), lambda b, c: (b, 0, 0, 0, 0)),
            pl.BlockSpec((1, 1, NSA_GROUPS, n_rows, HEAD_DIM), lambda b, c: (b, 1, 0, 0, 0)),
            pl.BlockSpec((n_selp, n_rows), lambda b, c: (0, 0)),
        ],
        out_specs=[
            pl.BlockSpec((1, Q_BLOCK, NSA_WIDTH), lambda b, c: (b, c, 0)),
            pl.BlockSpec((1, NSA_GROUPS, Q_BLOCK, n_selp), lambda b, c: (b, 0, c, 0)),
        ],
        out_shape=[
            jax.ShapeDtypeStruct((batch, t_len, NSA_WIDTH), BF16),
            jax.ShapeDtypeStruct((batch, NSA_GROUPS, t_len, n_selp), BF16),
        ],
        compiler_params=_cparams(("parallel", "parallel")),
        name="cmp_select",
    )(p, kvc, kvc, sel_map)


def _lane_tile(x, n):
    return x if n == 1 else jnp.concatenate([x] * n, axis=1)


def _flash_update(s, v, rows, m_ref, l_ref, acc_ref):
    n_keys = s.shape[1]
    dv = acc_ref.shape[-1]
    m_prev = m_ref[rows, :]
    m_new = jnp.maximum(m_prev, jnp.max(s, axis=-1, keepdims=True))
    alpha = jnp.exp2(m_prev - m_new)
    p = jnp.exp2(s - _lane_tile(m_new, n_keys // LANE))
    if l_ref is not None:
        psum = p[:, :LANE]
        for cc in range(1, n_keys // LANE):
            psum = psum + p[:, cc * LANE:(cc + 1) * LANE]
        l_ref[rows, :] = alpha * l_ref[rows, :] + psum
    acc_ref[rows, :] = (_lane_tile(alpha, dv // LANE) * acc_ref[rows, :]
                        + jnp.dot(p.astype(BF16), v, preferred_element_type=F32))
    m_ref[rows, :] = m_new


def _causal_triangle(rb):
    return lax.broadcasted_iota(jnp.int32, (rb, rb), 1) <= lax.broadcasted_iota(jnp.int32, (rb, rb), 0)


def _sel_attn_kernel(q_ref, bias_ref, k_ref, v_ref, hot_ref, o_ref, qa_ref, m_ref, acc_ref,
                     *, tq, tk, rb, rl, n_halves):
    start = pl.program_id(2) * tq
    n_rows = NSA_HPG * tq
    keys_per_half = LANE * SEL_BLOCK
    for hf in range(n_halves):
        for h in range(NSA_HPG):
            qa_ref[hf, h * tq:(h + 1) * tq, :LANE] = q_ref[0, h]
            qa_ref[hf, h * tq:(h + 1) * tq, LANE:] = bias_ref[0, 0, :, hf * LANE:(hf + 1) * LANE]
    m_ref[...] = jnp.full(m_ref.shape, NEG_INF, F32)
    acc_ref[...] = jnp.zeros(acc_ref.shape, F32)

    def k_aug(k0, n):
        return jnp.concatenate([k_ref[0, 0, pl.ds(k0, n), :], hot_ref[pl.ds(k0, n), :]], axis=1)

    def v_aug(k0, n):
        return jnp.concatenate([v_ref[0, 0, pl.ds(k0, n), :], jnp.ones((n, LANE), BF16)], axis=1)

    def body(j, carry):
        k0 = pl.multiple_of(j * tk, tk)
        ka, v, hf = k_aug(k0, tk), v_aug(k0, tk), k0 // keys_per_half
        for r in range(n_rows // rl):
            rows = pl.ds(r * rl, rl)
            s = lax.dot_general(qa_ref[hf, rows, :], ka, _NT, preferred_element_type=F32)
            _flash_update(s, v, rows, m_ref, None, acc_ref)
        return carry

    lax.fori_loop(0, start // tk, body, 0)

    tri = _causal_triangle(rb)
    for d in range(tq // rb):
        k0 = pl.multiple_of(start + d * rb, rb)
        ka, v, hf = k_aug(k0, rb), v_aug(k0, rb), k0 // keys_per_half
        for r in range(n_rows // rb):
            q_off = (r * rb) % tq
            if d * rb > q_off:
                continue
            rows = pl.ds(r * rb, rb)
            s = lax.dot_general(qa_ref[hf, rows, :], ka, _NT, preferred_element_type=F32)
            if d * rb == q_off:
                s = jnp.where(tri, s, NEG_INF)
            _flash_update(s, v, rows, m_ref, None, acc_ref)

    for h in range(NSA_HPG):
        hr = slice(h * tq, (h + 1) * tq)
        o_ref[0, :, h * HEAD_DIM:(h + 1) * HEAD_DIM] = (acc_ref[hr, :HEAD_DIM] / acc_ref[hr, HEAD_DIM:]).astype(BF16)


def _resident(block_shape, index_map):
    return pl.BlockSpec(block_shape, index_map, pipeline_mode=pl.Buffered(1))


def _sel_attn(p, bias, hot, batch, t_len, tq, tk, rb, rl):
    assert tq % tk == 0 and tk % rb == 0 and (LANE * SEL_BLOCK) % tk == 0
    n_selp = bias.shape[-1]
    n_halves = n_selp // LANE
    n_rows = NSA_HPG * tq
    assert n_rows % rl == 0
    kern = functools.partial(_sel_attn_kernel, tq=tq, tk=tk, rb=rb, rl=rl, n_halves=n_halves)
    return pl.pallas_call(
        kern,
        grid=(batch, NSA_GROUPS, t_len // tq),
        in_specs=[
            pl.BlockSpec((1, NSA_HPG, tq, HEAD_DIM), lambda b, g, c: (b, g, c, 0)),
            pl.BlockSpec((1, 1, tq, n_selp), lambda b, g, c: (b, g, c, 0)),
            _resident((1, 1, t_len, HEAD_DIM), lambda b, g, c: (b, SLOT_KS + g, 0, 0)),
            _resident((1, 1, t_len, HEAD_DIM), lambda b, g, c: (b, SLOT_VS + g, 0, 0)),
            _resident((t_len, LANE), lambda b, g, c: (0, 0)),
        ],
        out_specs=pl.BlockSpec((1, tq, NSA_HPG * HEAD_DIM), lambda b, g, c: (b, c, g)),
        out_shape=jax.ShapeDtypeStruct((batch, t_len, NSA_WIDTH), BF16),
        scratch_shapes=[
            pltpu.VMEM((n_halves, n_rows, 2 * LANE), BF16),
            pltpu.VMEM((n_rows, LANE), F32),
            pltpu.VMEM((n_rows, 2 * HEAD_DIM), F32),
        ],
        compiler_params=_cparams(("parallel", "parallel", "arbitrary")),
        name="sel_attn",
    )(p, bias, p, p, hot)


def _win_attn_kernel(q_ref, k_ref, v_ref, o_ref, *, tq, span):
    start = pl.program_id(1) * tq
    rows = NSA_HPG * Q_BLOCK
    for qc in range(tq // Q_BLOCK):
        q_rows = slice(qc * Q_BLOCK, (qc + 1) * Q_BLOCK)
        q0 = start + qc * Q_BLOCK
        k0 = pl.multiple_of(jnp.maximum(q0 + Q_BLOCK - span, 0), Q_BLOCK)
        t_q = q0 + lax.broadcasted_iota(jnp.int32, (1, Q_BLOCK, span), 1)
        k_pos = k0 + lax.broadcasted_iota(jnp.int32, (1, Q_BLOCK, span), 2)
        valid = (k_pos <= t_q) & (k_pos > t_q - WINDOW)
        for g in range(NSA_GROUPS):
            q = q_ref[0, g * NSA_HPG:(g + 1) * NSA_HPG, q_rows, :].reshape(rows, HEAD_DIM)
            s = lax.dot_general(q, k_ref[0, g, pl.ds(k0, span), :], _NT, preferred_element_type=F32)
            s = jnp.where(valid, s.reshape(NSA_HPG, Q_BLOCK, span), NEG_INF)
            e = jnp.exp2(s - jnp.max(s, axis=-1, keepdims=True))
            v = jnp.concatenate([v_ref[0, g, pl.ds(k0, span), :], jnp.ones((span, LANE), BF16)], axis=1)
            o = jnp.dot(e.reshape(rows, span).astype(BF16), v, preferred_element_type=F32)
            o = o[:, :HEAD_DIM] / o[:, HEAD_DIM:]
            for h in range(NSA_HPG):
                hd = g * NSA_HPG + h
                o_ref[0, q_rows, hd * HEAD_DIM:(hd + 1) * HEAD_DIM] = o[h * Q_BLOCK:(h + 1) * Q_BLOCK].astype(BF16)


def _win_attn(p, batch, t_len, tq):
    span = min(WINDOW + Q_BLOCK, t_len)
    kern = functools.partial(_win_attn_kernel, tq=tq, span=span)
    return pl.pallas_call(
        kern,
        grid=(batch, t_len // tq),
        in_specs=[
            pl.BlockSpec((1, NSA_HEADS, tq, HEAD_DIM), lambda b, c: (b, 0, c, 0)),
            _resident((1, NSA_GROUPS, t_len, HEAD_DIM), lambda b, c: (b, SLOT_KW // NSA_GROUPS, 0, 0)),
            _resident((1, NSA_GROUPS, t_len, HEAD_DIM), lambda b, c: (b, SLOT_VW // NSA_GROUPS, 0, 0)),
        ],
        out_specs=pl.BlockSpec((1, tq, NSA_WIDTH), lambda b, c: (b, c, 0)),
        out_shape=jax.ShapeDtypeStruct((batch, t_len, NSA_WIDTH), BF16),
        compiler_params=_cparams(("parallel", "parallel")),
        name="win_attn",
    )(p, p, p)


def _diff_attn_kernel(q1_ref, q2_ref, k1_ref, k2_ref, v_ref, lam_ref, sw_ref, o_ref,
                      m_ref, l_ref, acc_ref, *, tq, tk, rb, rl):
    start = pl.program_id(2) * tq
    m_ref[...] = jnp.full(m_ref.shape, NEG_INF, F32)
    l_ref[...] = jnp.zeros(l_ref.shape, F32)
    acc_ref[...] = jnp.zeros(acc_ref.shape, F32)
    q_refs = (q1_ref, q2_ref)
    k_refs = (k1_ref, k2_ref)

    def v_tile(k0, n):
        return jnp.concatenate([v_ref[0, 0, pl.ds(k0, n), :], v_ref[0, 1, pl.ds(k0, n), :]], axis=1)

    def body(j, carry):
        k0 = pl.multiple_of(j * tk, tk)
        v = v_tile(k0, tk)
        for mp in range(2):
            kt = k_refs[mp][0, 0, pl.ds(k0, tk), :]
            for r in range(tq // rl):
                rows = pl.ds(r * rl, rl)
                s = lax.dot_general(q_refs[mp][0, 0, rows, :], kt, _NT, preferred_element_type=F32)
                _flash_update(s, v, rows, m_ref.at[mp], l_ref.at[mp], acc_ref.at[mp])
        return carry

    lax.fori_loop(0, start // tk, body, 0)

    tri = _causal_triangle(rb)
    for d in range(tq // rb):
        k0 = pl.multiple_of(start + d * rb, rb)
        v = v_tile(k0, rb)
        for mp in range(2):
            kt = k_refs[mp][0, 0, pl.ds(k0, rb), :]
            for r in range(d, tq // rb):
                rows = pl.ds(r * rb, rb)
                s = lax.dot_general(q_refs[mp][0, 0, rows, :], kt, _NT, preferred_element_type=F32)
                if r == d:
                    s = jnp.where(tri, s, NEG_INF)
                _flash_update(s, v, rows, m_ref.at[mp], l_ref.at[mp], acc_ref.at[mp])

    lv = lam_ref[...]
    lam = (jnp.exp(jnp.sum(lv[0:1] * lv[1:2], axis=-1, keepdims=True))
           - jnp.exp(jnp.sum(lv[2:3] * lv[3:4], axis=-1, keepdims=True)) + LAMBDA_INIT)
    l1 = jnp.sum(l_ref[0], axis=-1, keepdims=True)
    l2 = jnp.sum(l_ref[1], axis=-1, keepdims=True)
    o = acc_ref[0] / l1 - lam * (acc_ref[1] / l2)
    y = o * lax.rsqrt(jnp.mean(o * o, axis=-1, keepdims=True) + SUBLN_EPS)
    o_ref[0] = (y * sw_ref[...] * (1.0 - LAMBDA_INIT)).astype(BF16)


def _diff_attn(p, lam4, subln_w, batch, t_len, tq, tk, rb, rl):
    assert tq % tk == 0 and tk % rb == 0 and tq % rl == 0
    kern = functools.partial(_diff_attn_kernel, tq=tq, tk=tk, rb=rb, rl=rl)
    return pl.pallas_call(
        kern,
        grid=(batch, DIFF_HEADS, t_len // tq),
        in_specs=[
            pl.BlockSpec((1, 1, tq, DIFF_QK), lambda b, h, i: (b, SLOT_QB + h, i, 0)),
            pl.BlockSpec((1, 1, tq, DIFF_QK), lambda b, h, i: (b, SLOT_QB + DIFF_HEADS + h, i, 0)),
            _resident((1, 1, t_len, DIFF_QK), lambda b, h, i: (b, SLOT_KB + h, 0, 0)),
            _resident((1, 1, t_len, DIFF_QK), lambda b, h, i: (b, SLOT_KB + DIFF_HEADS + h, 0, 0)),
            _resident((1, 2, t_len, LANE), lambda b, h, i: (b, SLOT_VB // 2 + h, 0, 0)),
            pl.BlockSpec((8, DIFF_QK), lambda b, h, i: (0, 0)),
            pl.BlockSpec((1, DIFF_V), lambda b, h, i: (0, 0)),
        ],
        out_specs=pl.BlockSpec((1, tq, DIFF_V), lambda b, h, i: (b, i, h)),
        out_shape=jax.ShapeDtypeStruct((batch, t_len, DIFF_WIDTH), BF16),
        scratch_shapes=[
            pltpu.VMEM((2, tq, LANE), F32),
            pltpu.VMEM((2, tq, LANE), F32),
            pltpu.VMEM((2, tq, DIFF_V), F32),
        ],
        compiler_params=_cparams(("parallel", "parallel", "arbitrary")),
        name="diff_attn",
    )(p, p, p, p, p, lam4, subln_w)


OUT_K_SLOTS = 2


def _out_proj_kernel(x_ref, oc_ref, os_ref, ow_ref, g_ref, za_ref, ob_ref, zb_ref, w_ref, fw_ref, o_ref):
    gates = jax.nn.sigmoid(g_ref[0])
    tm = gates.shape[0]

    def nsa_piece(hd):
        cols = slice(hd * HEAD_DIM, (hd + 1) * HEAD_DIM)
        g0 = jnp.broadcast_to(gates[:, 3 * hd:3 * hd + 1], (tm, HEAD_DIM))
        g1 = jnp.broadcast_to(gates[:, 3 * hd + 1:3 * hd + 2], (tm, HEAD_DIM))
        g2 = jnp.broadcast_to(gates[:, 3 * hd + 2:3 * hd + 3], (tm, HEAD_DIM))
        o = (g0 * oc_ref[0, :, cols].astype(F32) + g1 * os_ref[0, :, cols].astype(F32)
             + g2 * ow_ref[0, :, cols].astype(F32))
        z = za_ref[0, hd].astype(F32)
        return (o * (z * jax.nn.sigmoid(z))).astype(BF16)

    def diff_piece(sl):
        z = zb_ref[0, sl].astype(F32)
        return (ob_ref[0, :, sl * LANE:(sl + 1) * LANE].astype(F32) * (z * jax.nn.sigmoid(z))).astype(BF16)

    pieces = [functools.partial(nsa_piece, hd) for hd in range(NSA_HEADS)]
    pieces += [functools.partial(diff_piece, sl) for sl in range(DIFF_WIDTH // LANE)]
    y = x_ref[0]
    for kc in range(0, len(pieces), OUT_K_SLOTS):
        chunk = jnp.concatenate([f() for f in pieces[kc:kc + OUT_K_SLOTS]], axis=1)
        y = y + jnp.dot(chunk, w_ref[kc * LANE:(kc + OUT_K_SLOTS) * LANE, :], preferred_element_type=F32)
    y = y * lax.rsqrt(jnp.mean(y * y, axis=-1, keepdims=True) + NORM_EPS)
    o_ref[0] = y * fw_ref[...]


def _out_proj(x, o_cmp, o_sel, o_win, gl, p, o_b, w_out, final_w, batch, t_len, tm):
    row = lambda b, i: (b, i, 0)
    return pl.pallas_call(
        _out_proj_kernel,
        grid=(batch, t_len // tm),
        in_specs=[
            pl.BlockSpec((1, tm, D_MODEL), row),
            pl.BlockSpec((1, tm, NSA_WIDTH), row),
            pl.BlockSpec((1, tm, NSA_WIDTH), row),
            pl.BlockSpec((1, tm, NSA_WIDTH), row),
            pl.BlockSpec((1, tm, LANE), row),
            pl.BlockSpec((1, 8, tm, LANE), lambda b, i: (b, SLOT_ZA // 8, i, 0)),
            pl.BlockSpec((1, tm, DIFF_WIDTH), row),
            pl.BlockSpec((1, 8, tm, LANE), lambda b, i: (b, SLOT_ZB // 8, i, 0)),
            pl.BlockSpec((MIX_WIDTH, D_MODEL), lambda b, i: (0, 0)),
            pl.BlockSpec((1, D_MODEL), lambda b, i: (0, 0)),
        ],
        out_specs=pl.BlockSpec((1, tm, D_MODEL), row),
        out_shape=jax.ShapeDtypeStruct((batch, t_len, D_MODEL), F32),
        compiler_params=_cparams(("parallel", "parallel")),
        name="out_proj",
    )(x, o_cmp, o_sel, o_win, gl, p, o_b, p, w_out, final_w)


def _reorder_w_in(w):
    def section(off, n_slots):
        return w[:, off:off + n_slots * LANE]
    cols = [section(_O_QA, 8), section(_O_QB, 8), section(_O_KB, 8), section(_O_KC, 2), section(_O_KS, 2),
            section(_O_KW, 2), section(_O_VC, 2), section(_O_ZA, 8), section(_O_ZB, 8), section(_O_VB, 8),
            section(_O_VS, 2), section(_O_VW, 2)]
    w_r = jnp.concatenate(cols, axis=1).astype(BF16)
    assert w_r.shape[1] == N_SLOTS * LANE
    w_g = jnp.pad(w[:, _O_G:_O_G + NSA_HEADS * NSA_BRANCHES],
                  ((0, 0), (0, LANE - NSA_HEADS * NSA_BRANCHES))).astype(BF16)
    return w_r, w_g


def _sel_map(t_len, n_selp):
    n_rows = t_len // CMP_STRIDE
    c0 = np.arange(n_rows)[:, None] * CMP_STRIDE
    s0 = np.arange(n_selp)[None, :] * SEL_BLOCK
    ov = np.minimum(c0 + CMP_BLOCK, s0 + SEL_BLOCK) - np.maximum(c0, s0)
    m = np.maximum(ov, 0) / CMP_BLOCK
    m[n_rows - 1:] = 0.0
    m[:, t_len // SEL_BLOCK:] = 0.0
    return jnp.asarray(m.T, dtype=BF16)


def _block_one_hot(t_len):
    blk = (np.arange(t_len) // SEL_BLOCK) % LANE
    return jnp.asarray(blk[:, None] == np.arange(LANE)[None, :], dtype=BF16)


def kernel(x, norm_w, w_in, cmp_pos, cmp_k_w1, cmp_k_w2, cmp_v_w1, cmp_v_w2, lam_q1, lam_k1, lam_q2, lam_k2,
           subln_w, w_out, final_norm_w):
    batch, t_len, _ = x.shape
    n_rows = t_len // CMP_STRIDE
    n_selp = -(-(t_len // SEL_BLOCK) // LANE) * LANE

    pos = jnp.arange(t_len, dtype=F32)
    inv = ROPE_THETA ** (-jnp.arange(0, HEAD_DIM, 2, dtype=F32) / HEAD_DIM)
    ang = pos[:, None] * inv[None, :]
    cos, sin = jnp.cos(ang), jnp.sin(ang)
    cos_f = jnp.concatenate([cos, cos], axis=1)
    sin_f = jnp.concatenate([-sin, sin], axis=1)

    w_r, w_g = _reorder_w_in(w_in[0])
    tm_in = min(1024, t_len)
    p, gl, pc = _in_proj(x.reshape(batch * t_len, D_MODEL), norm_w[0][None, :], w_r, w_g, cos_f, sin_f,
                         batch, t_len, tm_in)

    half = CMP_STRIDE * HEAD_DIM
    w1cat = jnp.stack([jnp.concatenate([w[:half], w[half:]], axis=1) for w in (cmp_k_w1[0], cmp_v_w1[0])]).astype(BF16)
    w2 = jnp.stack([cmp_k_w2[0], cmp_v_w2[0]]).astype(BF16)
    pos2 = jnp.broadcast_to(cmp_pos[0].reshape(2, 1, half), (2, 8, half)).astype(BF16)
    kvc = _compress(pc, w1cat, pos2, w2, batch, n_rows)

    o_cmp, bias = _cmp_select(p, kvc, _sel_map(t_len, n_selp), batch, t_len)
    o_sel = _sel_attn(p, bias, _block_one_hot(t_len), batch, t_len, tq=min(2048, t_len), tk=512, rb=256, rl=512)
    o_win = _win_attn(p, batch, t_len, tq=512)

    lam4 = jnp.pad(jnp.stack([lam_q1[0], lam_k1[0], lam_q2[0], lam_k2[0]]), ((0, 4), (0, 0)))
    o_b = _diff_attn(p, lam4, subln_w[0][None, :], batch, t_len, tq=min(2048, t_len), tk=1024, rb=256, rl=128)

    return _out_proj(x, o_cmp, o_sel, o_win, gl, p, o_b, w_out[0].astype(BF16), final_norm_w[None, :],
                     batch, t_len, tm=min(512, t_len))
```

```python
import functools
import math
from typing import NamedTuple

import numpy as np
import jax
import jax.numpy as jnp
from jax import lax
from jax.experimental import pallas as pl
from jax.experimental.pallas import tpu as pltpu

F32 = jnp.float32
BF16 = jnp.bfloat16

D_MODEL = 2048
HEAD_DIM = 128
NSA_HEADS = 8
NSA_GROUPS = 2
NSA_HPG = NSA_HEADS // NSA_GROUPS
NSA_WIDTH = NSA_HEADS * HEAD_DIM
NSA_KV = NSA_GROUPS * HEAD_DIM
NSA_BRANCHES = 3
CMP_BLOCK = 32
CMP_STRIDE = 16
CMP_HIDDEN = 512
SEL_BLOCK = 64
N_SELECT = 16
WINDOW = 512
DIFF_HEADS = 4
DIFF_QK = 128
DIFF_V = 256
DIFF_WIDTH = DIFF_HEADS * DIFF_V
MIX_WIDTH = NSA_WIDTH + DIFF_WIDTH
Q_BLOCK = 128
ROPE_THETA = 10000.0
NORM_EPS = 1e-6
SUBLN_EPS = 1e-5
NEG_INF = -1e30
LAMBDA_INIT = 0.8 - 0.6 * math.exp(-0.3 * 0)

LANE = 128
SUBLANE = 8
VMEM_LIMIT = 56 * 1024 * 1024

SLOT_QA = 0
SLOT_QB = 8
SLOT_KB = 16
SLOT_KC = 24
SLOT_KS = 26
SLOT_KW = 28
N_ROPE_SLOTS = 30
SLOT_VC = 30
SLOT_ZA = 32
SLOT_ZB = 40
SLOT_VB = 48
SLOT_VS = 56
SLOT_VW = 58
N_SLOTS = 60
SLOTS_PER_STEP = 12
N_QUERY_SLOTS = 16
QK_EXP2_SCALE = (HEAD_DIM ** -0.5) * math.log2(math.e)

_OFF = np.concatenate([[0], np.cumsum([NSA_WIDTH, NSA_KV, NSA_KV, NSA_KV, NSA_KV, NSA_KV, NSA_KV,
                                       NSA_HEADS * NSA_BRANCHES, NSA_WIDTH, 1024, 1024, DIFF_WIDTH, DIFF_WIDTH])])
(_O_QA, _O_KC, _O_VC, _O_KS, _O_VS, _O_KW, _O_VW, _O_G, _O_ZA, _O_QB, _O_KB, _O_VB, _O_ZB, _O_END) = [int(v) for v in _OFF]


def _cparams(sem):
    return pltpu.CompilerParams(dimension_semantics=sem, vmem_limit_bytes=VMEM_LIMIT)


def _in_proj_kernel(x_ref, nw_ref, w_ref, wg_ref, cos_ref, sin_ref, p_ref, g_ref, c_ref, h_ref, stage_ref):
    j = pl.program_id(1)

    @pl.when(j == 0)
    def _():
        x = x_ref[...]
        y = x * lax.rsqrt(jnp.mean(x * x, axis=-1, keepdims=True) + NORM_EPS)
        h = (y * nw_ref[...]).astype(BF16)
        h_ref[...] = h
        g_ref[0] = jnp.dot(h, wg_ref[...], preferred_element_type=F32)

    acc = jnp.dot(h_ref[...], w_ref[...], preferred_element_type=F32)

    cos_t = cos_ref[...]
    sin_t = sin_ref[...]
    for s in range(SLOTS_PER_STEP):
        slot = j * SLOTS_PER_STEP + s
        cos = jnp.where(slot < N_ROPE_SLOTS, cos_t, 1.0)
        sin = jnp.where(slot < N_ROPE_SLOTS, sin_t, 0.0)
        a = acc[:, s * LANE:(s + 1) * LANE]
        r = a * cos + pltpu.roll(a, HEAD_DIM // 2, 1) * sin
        f = jnp.where(slot < N_QUERY_SLOTS, QK_EXP2_SCALE, 1.0)
        p_ref[0, s] = (r * f).astype(BF16)

    @pl.when(j == SLOT_KC // SLOTS_PER_STEP)
    def _():
        n_out = stage_ref.shape[0] // CMP_STRIDE
        for n, slot in enumerate((SLOT_KC, SLOT_KC + 1, SLOT_VC, SLOT_VC + 1)):
            stage_ref[...] = p_ref[0, slot % SLOTS_PER_STEP].astype(F32)
            for l in range(CMP_STRIDE):
                c_ref[0, n, :, l * LANE:(l + 1) * LANE] = stage_ref[pl.ds(l, n_out, stride=CMP_STRIDE), :].astype(BF16)


def _in_proj(x2, norm_w, w_r, w_g, cos_f, sin_f, batch, t_len, tm):
    m = x2.shape[0]
    tiles_per_seq = t_len // tm
    assert SLOT_KC // SLOTS_PER_STEP == (SLOT_VC + 1) // SLOTS_PER_STEP
    n_steps = N_SLOTS // SLOTS_PER_STEP
    tn = SLOTS_PER_STEP * LANE
    return pl.pallas_call(
        _in_proj_kernel,
        grid=(m // tm, n_steps),
        in_specs=[
            pl.BlockSpec((tm, D_MODEL), lambda i, j: (i, 0)),
            pl.BlockSpec((1, D_MODEL), lambda i, j: (0, 0)),
            pl.BlockSpec((D_MODEL, tn), lambda i, j: (0, j)),
            pl.BlockSpec((D_MODEL, LANE), lambda i, j: (0, 0)),
            pl.BlockSpec((tm, LANE), lambda i, j: (i % tiles_per_seq, 0)),
            pl.BlockSpec((tm, LANE), lambda i, j: (i % tiles_per_seq, 0)),
        ],
        out_specs=[
            pl.BlockSpec((1, SLOTS_PER_STEP, tm, LANE), lambda i, j: (i // tiles_per_seq, j, i % tiles_per_seq, 0)),
            pl.BlockSpec((1, tm, LANE), lambda i, j: (i // tiles_per_seq, i % tiles_per_seq, 0)),
            pl.BlockSpec((1, 2 * NSA_GROUPS, tm // CMP_STRIDE, CMP_STRIDE * HEAD_DIM),
                         lambda i, j: (i // tiles_per_seq, 0, i % tiles_per_seq, 0)),
        ],
        out_shape=[
            jax.ShapeDtypeStruct((batch, N_SLOTS, t_len, LANE), BF16),
            jax.ShapeDtypeStruct((batch, t_len, LANE), F32),
            jax.ShapeDtypeStruct((batch, 2 * NSA_GROUPS, t_len // CMP_STRIDE, CMP_STRIDE * HEAD_DIM), BF16),
        ],
        scratch_shapes=[pltpu.VMEM((tm, D_MODEL), BF16), pltpu.VMEM((tm, LANE), F32)],
        compiler_params=_cparams(("parallel", "arbitrary")),
        name="in_proj",
    )(x2, norm_w, w_r, w_g, cos_f, sin_f)


def _compress_kernel(r_ref, w1_ref, pos_ref, w2_ref, o_ref, *, n_rows):
    w1 = w1_ref[0]
    a = jnp.dot(r_ref[0, 0], w1, preferred_element_type=F32)
    c = (jnp.dot(pos_ref[0], w1[:, :CMP_HIDDEN], preferred_element_type=F32)
         + jnp.dot(pos_ref[1], w1[:, CMP_HIDDEN:], preferred_element_type=F32))[0:1]
    nxt = pltpu.roll(a[:, CMP_HIDDEN:], n_rows - 1, 0)
    hid = a[:, :CMP_HIDDEN] + nxt + c
    hid = hid * jax.nn.sigmoid(hid)
    out = jnp.dot(hid.astype(BF16), w2_ref[0], preferred_element_type=F32)
    row = lax.broadcasted_iota(jnp.int32, out.shape, 0)
    o_ref[0, 0, 0] = jnp.where(row < n_rows - 1, out, 0.0).astype(BF16)


def _compress(p16, w1cat, pos2, w2, batch, n_rows):
    kern = functools.partial(_compress_kernel, n_rows=n_rows)

    return pl.pallas_call(
        kern,
        grid=(batch, 2, NSA_GROUPS),
        in_specs=[
            pl.BlockSpec((1, 1, n_rows, CMP_STRIDE * HEAD_DIM), lambda b, kv, g: (b, kv * NSA_GROUPS + g, 0, 0)),
            pl.BlockSpec((1, CMP_STRIDE * HEAD_DIM, 2 * CMP_HIDDEN), lambda b, kv, g: (kv, 0, 0)),
            pl.BlockSpec((2, SUBLANE, CMP_STRIDE * HEAD_DIM), lambda b, kv, g: (0, 0, 0)),
            pl.BlockSpec((1, CMP_HIDDEN, HEAD_DIM), lambda b, kv, g: (kv, 0, 0)),
        ],
        out_specs=pl.BlockSpec((1, 1, 1, n_rows, HEAD_DIM), lambda b, kv, g: (b, kv, g, 0, 0)),
        out_shape=jax.ShapeDtypeStruct((batch, 2, NSA_GROUPS, n_rows, HEAD_DIM), BF16),
        compiler_params=_cparams(("parallel", "parallel", "parallel")),
        name="compress",
    )(p16, w1cat, pos2, w2)


_NT = (((1,), (1,)), ((), ()))
CMP_PER_SEL = SEL_BLOCK // CMP_STRIDE


def _cmp_select_body(q_ref, kc_ref, vc_ref, map_ref, o_ref, bias_ref, start, g, *, n_top, w, bw):
    n_selp = map_ref.shape[0]
    q = q_ref[0, g * NSA_HPG:(g + 1) * NSA_HPG].reshape(NSA_HPG * Q_BLOCK, HEAD_DIM)
    s = lax.dot_general(q, kc_ref[0, 0, g, :w, :], _NT, preferred_element_type=F32)
    s = s.reshape(NSA_HPG, Q_BLOCK, w)
    t_q = start + lax.broadcasted_iota(jnp.int32, (1, Q_BLOCK, w), 1)
    n_id = lax.broadcasted_iota(jnp.int32, (1, Q_BLOCK, w), 2)
    valid = n_id * CMP_STRIDE + (CMP_BLOCK - 1) <= t_q
    s = jnp.where(valid, s, NEG_INF)
    e = jnp.exp2(s - jnp.max(s, axis=-1, keepdims=True))
    den = jnp.sum(e, axis=-1, keepdims=True)
    sees_any = start + lax.broadcasted_iota(jnp.int32, (1, Q_BLOCK, 1), 1) >= CMP_BLOCK - 1
    p = e * jnp.where(sees_any, 1.0 / den, 0.0)
    o = jnp.dot(p.reshape(NSA_HPG * Q_BLOCK, w).astype(BF16), vc_ref[0, 0, g, :w, :],
                preferred_element_type=F32)
    for h in range(NSA_HPG):
        hd = g * NSA_HPG + h
        o_ref[0, :, hd * HEAD_DIM:(hd + 1) * HEAD_DIM] = o[h * Q_BLOCK:(h + 1) * Q_BLOCK].astype(BF16)

    psum = p[0] + p[1] + p[2] + p[3]
    p_hi = psum.astype(BF16)
    p_lo = (psum - p_hi.astype(F32)).astype(BF16)
    smap_t = map_ref[:bw, :w]
    imp = (lax.dot_general(smap_t, p_hi, _NT, preferred_element_type=F32)
           + lax.dot_general(smap_t, p_lo, _NT, preferred_element_type=F32))
    blk = lax.broadcasted_iota(jnp.int32, (bw, Q_BLOCK), 0)
    cur = (start + lax.broadcasted_iota(jnp.int32, (bw, Q_BLOCK), 1)) // SEL_BLOCK
    forced = (blk == 0) | (blk == cur) | (blk == cur - 1)
    blk_f = blk.astype(F32)
    work = jnp.where(forced, -jnp.inf, jnp.where(blk > cur, NEG_INF, imp))
    for _ in range(n_top - 3):
        top = jnp.max(work, axis=0, keepdims=True)
        first = jnp.min(jnp.where(work == top, blk_f, 1e9), axis=0, keepdims=True)
        work = jnp.where(blk_f == first, -jnp.inf, work)
    keep = (work == -jnp.inf) & (blk <= cur)
    bias_t = jnp.where(keep, 0.0, NEG_INF)
    if bw < n_selp:
        bias_t = jnp.concatenate([bias_t, jnp.full((n_selp - bw, Q_BLOCK), NEG_INF, F32)], axis=0)
    bias_ref[0, g] = bias_t.T.astype(BF16)


def _cmp_select_kernel(q_ref, kc_ref, vc_ref, map_ref, o_ref, bias_ref, *, n_rows, n_top, chunk):
    start = pl.program_id(1) * Q_BLOCK
    variant = ((start + Q_BLOCK - 1) // SEL_BLOCK) // (chunk // CMP_PER_SEL)

    def both_groups(w):
        for g in range(NSA_GROUPS):
            _cmp_select_body(q_ref, kc_ref, vc_ref, map_ref, o_ref, bias_ref, start, g,
                             n_top=n_top, w=w, bw=min(w // CMP_PER_SEL, map_ref.shape[0]))

    for v in range(n_rows // chunk):
        pl.when(variant == v)(functools.partial(both_groups, (v + 1) * chunk))


def _cmp_select(p, kvc, sel_map, batch, t_len):
    n_rows = t_len // CMP_STRIDE
    n_selp = sel_map.shape[0]
    n_top = min(N_SELECT, t_len // SEL_BLOCK)
    assert n_top >= 3
    chunk = min(256, n_rows)
    assert n_rows % chunk == 0 and (chunk // CMP_PER_SEL) % 8 == 0
    kern = functools.partial(_cmp_select_kernel, n_rows=n_rows, n_top=n_top, chunk=chunk)
    return pl.pallas_call(
        kern,
        grid=(batch, t_len // Q_BLOCK),
        in_specs=[
            pl.BlockSpec((1, NSA_HEADS, Q_BLOCK, HEAD_DIM), lambda b, c: (b, 0, c, 0)),
            pl.BlockSpec((1, 1, NSA_GROUPS, n_rows, HEAD_DIM), lambda b, c: (b, 0, 0, 0, 0)),
            pl.BlockSpec((1, 1, NSA_GROUPS, n_rows, HEAD_DIM), lambda b, c: (b, 1, 0, 0, 0)),
            pl.BlockSpec((n_selp, n_rows), lambda b, c: (0, 0)),
        ],
        out_specs=[
            pl.BlockSpec((1, Q_BLOCK, NSA_WIDTH), lambda b, c: (b, c, 0)),
            pl.BlockSpec((1, NSA_GROUPS, Q_BLOCK, n_selp), lambda b, c: (b, 0, c, 0)),
        ],
        out_shape=[
            jax.ShapeDtypeStruct((batch, t_len, NSA_WIDTH), BF16),
            jax.ShapeDtypeStruct((batch, NSA_GROUPS, t_len, n_selp), BF16),
        ],
        compiler_params=_cparams(("parallel", "parallel")),
        name="cmp_select",
    )(p, kvc, kvc, sel_map)


def _lane_tile(x, n):
    return x if n == 1 else jnp.concatenate([x] * n, axis=1)


def _flash_update(s, v, rows, m_ref, l_ref, acc_ref):
    n_keys = s.shape[1]
    dv = acc_ref.shape[-1]
    m_prev = m_ref[rows, :]
    m_new = jnp.maximum(m_prev, jnp.max(s, axis=-1, keepdims=True))
    alpha = jnp.exp2(m_prev - m_new)
    p = jnp.exp2(s - _lane_tile(m_new, n_keys // LANE))
    if l_ref is not None:
        psum = p[:, :LANE]
        for cc in range(1, n_keys // LANE):
            psum = psum + p[:, cc * LANE:(cc + 1) * LANE]
        l_ref[rows, :] = alpha * l_ref[rows, :] + psum
    acc_ref[rows, :] = (_lane_tile(alpha, dv // LANE) * acc_ref[rows, :]
                        + jnp.dot(p.astype(BF16), v, preferred_element_type=F32))
    m_ref[rows, :] = m_new


def _causal_triangle(rb):
    return lax.broadcasted_iota(jnp.int32, (rb, rb), 1) <= lax.broadcasted_iota(jnp.int32, (rb, rb), 0)


def _sel_attn_kernel(q_ref, bias_ref, k_ref, v_ref, hot_ref, o_ref, qa_ref, m_ref, acc_ref,
                     *, tq, tk, rb, rl, n_halves):
    start = pl.program_id(2) * tq
    n_rows = NSA_HPG * tq
    keys_per_half = LANE * SEL_BLOCK
    for hf in range(n_halves):
        for h in range(NSA_HPG):
            qa_ref[hf, h * tq:(h + 1) * tq, :LANE] = q_ref[0, h]
            qa_ref[hf, h * tq:(h + 1) * tq, LANE:] = bias_ref[0, 0, :, hf * LANE:(hf + 1) * LANE]
    m_ref[...] = jnp.full(m_ref.shape, NEG_INF, F32)
    acc_ref[...] = jnp.zeros(acc_ref.shape, F32)

    def k_aug(k0, n):
        return jnp.concatenate([k_ref[0, 0, pl.ds(k0, n), :], hot_ref[pl.ds(k0, n), :]], axis=1)

    def v_aug(k0, n):
        return jnp.concatenate([v_ref[0, 0, pl.ds(k0, n), :], jnp.ones((n, LANE), BF16)], axis=1)

    def body(j, carry):
        k0 = pl.multiple_of(j * tk, tk)
        ka, v, hf = k_aug(k0, tk), v_aug(k0, tk), k0 // keys_per_half
        for r in range(n_rows // rl):
            rows = pl.ds(r * rl, rl)
            s = lax.dot_general(qa_ref[hf, rows, :], ka, _NT, preferred_element_type=F32)
            _flash_update(s, v, rows, m_ref, None, acc_ref)
        return carry

    lax.fori_loop(0, start // tk, body, 0)

    tri = _causal_triangle(rb)
    for d in range(tq // rb):
        k0 = pl.multiple_of(start + d * rb, rb)
        ka, v, hf = k_aug(k0, rb), v_aug(k0, rb), k0 // keys_per_half
        for r in range(n_rows // rb):
            q_off = (r * rb) % tq
            if d * rb > q_off:
                continue
            rows = pl.ds(r * rb, rb)
            s = lax.dot_general(qa_ref[hf, rows, :], ka, _NT, preferred_element_type=F32)
            if d * rb == q_off:
                s = jnp.where(tri, s, NEG_INF)
            _flash_update(s, v, rows, m_ref, None, acc_ref)

    for h in range(NSA_HPG):
        hr = slice(h * tq, (h + 1) * tq)
        o_ref[0, :, h * HEAD_DIM:(h + 1) * HEAD_DIM] = (acc_ref[hr, :HEAD_DIM] / acc_ref[hr, HEAD_DIM:]).astype(BF16)


def _resident(block_shape, index_map):
    return pl.BlockSpec(block_shape, index_map, pipeline_mode=pl.Buffered(1))


def _sel_attn(p, bias, hot, batch, t_len, tq, tk, rb, rl):
    assert tq % tk == 0 and tk % rb == 0 and (LANE * SEL_BLOCK) % tk == 0
    n_selp = bias.shape[-1]
    n_halves = n_selp // LANE
    n_rows = NSA_HPG * tq
    assert n_rows % rl == 0
    kern = functools.partial(_sel_attn_kernel, tq=tq, tk=tk, rb=rb, rl=rl, n_halves=n_halves)
    return pl.pallas_call(
        kern,
        grid=(batch, NSA_GROUPS, t_len // tq),
        in_specs=[
            pl.BlockSpec((1, NSA_HPG, tq, HEAD_DIM), lambda b, g, c: (b, g, c, 0)),
            pl.BlockSpec((1, 1, tq, n_selp), lambda b, g, c: (b, g, c, 0)),
            _resident((1, 1, t_len, HEAD_DIM), lambda b, g, c: (b, SLOT_KS + g, 0, 0)),
            _resident((1, 1, t_len, HEAD_DIM), lambda b, g, c: (b, SLOT_VS + g, 0, 0)),
            _resident((t_len, LANE), lambda b, g, c: (0, 0)),
        ],
        out_specs=pl.BlockSpec((1, tq, NSA_HPG * HEAD_DIM), lambda b, g, c: (b, c, g)),
        out_shape=jax.ShapeDtypeStruct((batch, t_len, NSA_WIDTH), BF16),
        scratch_shapes=[
            pltpu.VMEM((n_halves, n_rows, 2 * LANE), BF16),
            pltpu.VMEM((n_rows, LANE), F32),
            pltpu.VMEM((n_rows, 2 * HEAD_DIM), F32),
        ],
        compiler_params=_cparams(("parallel", "parallel", "arbitrary")),
        name="sel_attn",
    )(p, bias, p, p, hot)


def _win_attn_kernel(q_ref, k_ref, v_ref, o_ref, *, tq, span):
    start = pl.program_id(1) * tq
    rows = NSA_HPG * Q_BLOCK
    for qc in range(tq // Q_BLOCK):
        q_rows = slice(qc * Q_BLOCK, (qc + 1) * Q_BLOCK)
        q0 = start + qc * Q_BLOCK
        k0 = pl.multiple_of(jnp.maximum(q0 + Q_BLOCK - span, 0), Q_BLOCK)
        t_q = q0 + lax.broadcasted_iota(jnp.int32, (1, Q_BLOCK, span), 1)
        k_pos = k0 + lax.broadcasted_iota(jnp.int32, (1, Q_BLOCK, span), 2)
        valid = (k_pos <= t_q) & (k_pos > t_q - WINDOW)
        for g in range(NSA_GROUPS):
            q = q_ref[0, g * NSA_HPG:(g + 1) * NSA_HPG, q_rows, :].reshape(rows, HEAD_DIM)
            s = lax.dot_general(q, k_ref[0, g, pl.ds(k0, span), :], _NT, preferred_element_type=F32)
            s = jnp.where(valid, s.reshape(NSA_HPG, Q_BLOCK, span), NEG_INF)
            e = jnp.exp2(s - jnp.max(s, axis=-1, keepdims=True))
            v = jnp.concatenate([v_ref[0, g, pl.ds(k0, span), :], jnp.ones((span, LANE), BF16)], axis=1)
            o = jnp.dot(e.reshape(rows, span).astype(BF16), v, preferred_element_type=F32)
            o = o[:, :HEAD_DIM] / o[:, HEAD_DIM:]
            for h in range(NSA_HPG):
                hd = g * NSA_HPG + h
                o_ref[0, q_rows, hd * HEAD_DIM:(hd + 1) * HEAD_DIM] = o[h * Q_BLOCK:(h + 1) * Q_BLOCK].astype(BF16)


def _win_attn(p, batch, t_len, tq):
    span = min(WINDOW + Q_BLOCK, t_len)
    kern = functools.partial(_win_attn_kernel, tq=tq, span=span)
    return pl.pallas_call(
        kern,
        grid=(batch, t_len // tq),
        in_specs=[
            pl.BlockSpec((1, NSA_HEADS, tq, HEAD_DIM), lambda b, c: (b, 0, c, 0)),
            _resident((1, NSA_GROUPS, t_len, HEAD_DIM), lambda b, c: (b, SLOT_KW // NSA_GROUPS, 0, 0)),
            _resident((1, NSA_GROUPS, t_len, HEAD_DIM), lambda b, c: (b, SLOT_VW // NSA_GROUPS, 0, 0)),
        ],
        out_specs=pl.BlockSpec((1, tq, NSA_WIDTH), lambda b, c: (b, c, 0)),
        out_shape=jax.ShapeDtypeStruct((batch, t_len, NSA_WIDTH), BF16),
        compiler_params=_cparams(("parallel", "parallel")),
        name="win_attn",
    )(p, p, p)


def _diff_attn_kernel(q1_ref, q2_ref, k1_ref, k2_ref, v_ref, lam_ref, sw_ref, o_ref,
                      m_ref, l_ref, acc_ref, *, tq, tk, rb, rl):
    start = pl.program_id(2) * tq
    m_ref[...] = jnp.full(m_ref.shape, NEG_INF, F32)
    l_ref[...] = jnp.zeros(l_ref.shape, F32)
    acc_ref[...] = jnp.zeros(acc_ref.shape, F32)
    q_refs = (q1_ref, q2_ref)
    k_refs = (k1_ref, k2_ref)

    def v_tile(k0, n):
        return jnp.concatenate([v_ref[0, 0, pl.ds(k0, n), :], v_ref[0, 1, pl.ds(k0, n), :]], axis=1)

    def body(j, carry):
        k0 = pl.multiple_of(j * tk, tk)
        v = v_tile(k0, tk)
        for mp in range(2):
            kt = k_refs[mp][0, 0, pl.ds(k0, tk), :]
            for r in range(tq // rl):
                rows = pl.ds(r * rl, rl)
                s = lax.dot_general(q_refs[mp][0, 0, rows, :], kt, _NT, preferred_element_type=F32)
                _flash_update(s, v, rows, m_ref.at[mp], l_ref.at[mp], acc_ref.at[mp])
        return carry

    lax.fori_loop(0, start // tk, body, 0)

    tri = _causal_triangle(rb)
    for d in range(tq // rb):
        k0 = pl.multiple_of(start + d * rb, rb)
        v = v_tile(k0, rb)
        for mp in range(2):
            kt = k_refs[mp][0, 0, pl.ds(k0, rb), :]
            for r in range(d, tq // rb):
                rows = pl.ds(r * rb, rb)
                s = lax.dot_general(q_refs[mp][0, 0, rows, :], kt, _NT, preferred_element_type=F32)
                if r == d:
                    s = jnp.where(tri, s, NEG_INF)
                _flash_update(s, v, rows, m_ref.at[mp], l_ref.at[mp], acc_ref.at[mp])

    lv = lam_ref[...]
    lam = (jnp.exp(jnp.sum(lv[0:1] * lv[1:2], axis=-1, keepdims=True))
           - jnp.exp(jnp.sum(lv[2:3] * lv[3:4], axis=-1, keepdims=True)) + LAMBDA_INIT)
    l1 = jnp.sum(l_ref[0], axis=-1, keepdims=True)
    l2 = jnp.sum(l_ref[1], axis=-1, keepdims=True)
    o = acc_ref[0] / l1 - lam * (acc_ref[1] / l2)
    y = o * lax.rsqrt(jnp.mean(o * o, axis=-1, keepdims=True) + SUBLN_EPS)
    o_ref[0] = (y * sw_ref[...] * (1.0 - LAMBDA_INIT)).astype(BF16)


def _diff_attn(p, lam4, subln_w, batch, t_len, tq, tk, rb, rl):
    assert tq % tk == 0 and tk % rb == 0 and tq % rl == 0
    kern = functools.partial(_diff_attn_kernel, tq=tq, tk=tk, rb=rb, rl=rl)
    return pl.pallas_call(
        kern,
        grid=(batch, DIFF_HEADS, t_len // tq),
        in_specs=[
            pl.BlockSpec((1, 1, tq, DIFF_QK), lambda b, h, i: (b, SLOT_QB + h, i, 0)),
            pl.BlockSpec((1, 1, tq, DIFF_QK), lambda b, h, i: (b, SLOT_QB + DIFF_HEADS + h, i, 0)),
            _resident((1, 1, t_len, DIFF_QK), lambda b, h, i: (b, SLOT_KB + h, 0, 0)),
            _resident((1, 1, t_len, DIFF_QK), lambda b, h, i: (b, SLOT_KB + DIFF_HEADS + h, 0, 0)),
            _resident((1, 2, t_len, LANE), lambda b, h, i: (b, SLOT_VB // 2 + h, 0, 0)),
            pl.BlockSpec((SUBLANE, DIFF_QK), lambda b, h, i: (0, 0)),
            pl.BlockSpec((1, DIFF_V), lambda b, h, i: (0, 0)),
        ],
        out_specs=pl.BlockSpec((1, tq, DIFF_V), lambda b, h, i: (b, i, h)),
        out_shape=jax.ShapeDtypeStruct((batch, t_len, DIFF_WIDTH), BF16),
        scratch_shapes=[
            pltpu.VMEM((2, tq, LANE), F32),
            pltpu.VMEM((2, tq, LANE), F32),
            pltpu.VMEM((2, tq, DIFF_V), F32),
        ],
        compiler_params=_cparams(("parallel", "parallel", "arbitrary")),
        name="diff_attn",
    )(p, p, p, p, p, lam4, subln_w)


OUT_K_SLOTS = 2


def _out_proj_kernel(x_ref, oc_ref, os_ref, ow_ref, g_ref, za_ref, ob_ref, zb_ref, w_ref, fw_ref, o_ref):
    gates = jax.nn.sigmoid(g_ref[0])
    tm = gates.shape[0]

    def nsa_piece(hd):
        cols = slice(hd * HEAD_DIM, (hd + 1) * HEAD_DIM)
        g0 = jnp.broadcast_to(gates[:, 3 * hd:3 * hd + 1], (tm, HEAD_DIM))
        g1 = jnp.broadcast_to(gates[:, 3 * hd + 1:3 * hd + 2], (tm, HEAD_DIM))
        g2 = jnp.broadcast_to(gates[:, 3 * hd + 2:3 * hd + 3], (tm, HEAD_DIM))
        o = (g0 * oc_ref[0, :, cols].astype(F32) + g1 * os_ref[0, :, cols].astype(F32)
             + g2 * ow_ref[0, :, cols].astype(F32))
        z = za_ref[0, hd].astype(F32)
        return (o * (z * jax.nn.sigmoid(z))).astype(BF16)

    def diff_piece(sl):
        z = zb_ref[0, sl].astype(F32)
        return (ob_ref[0, :, sl * LANE:(sl + 1) * LANE].astype(F32) * (z * jax.nn.sigmoid(z))).astype(BF16)

    pieces = [functools.partial(nsa_piece, hd) for hd in range(NSA_HEADS)]
    pieces += [functools.partial(diff_piece, sl) for sl in range(DIFF_WIDTH // LANE)]
    y = x_ref[0]
    for kc in range(0, len(pieces), OUT_K_SLOTS):
        chunk = jnp.concatenate([f() for f in pieces[kc:kc + OUT_K_SLOTS]], axis=1)
        y = y + jnp.dot(chunk, w_ref[kc * LANE:(kc + OUT_K_SLOTS) * LANE, :], preferred_element_type=F32)
    y = y * lax.rsqrt(jnp.mean(y * y, axis=-1, keepdims=True) + NORM_EPS)
    o_ref[0] = y * fw_ref[...]


def _out_proj(x, o_cmp, o_sel, o_win, gl, p, o_b, w_out, final_w, batch, t_len, tm):
    row = lambda b, i: (b, i, 0)
    return pl.pallas_call(
        _out_proj_kernel,
        grid=(batch, t_len // tm),
        in_specs=[
            pl.BlockSpec((1, tm, D_MODEL), row),
            pl.BlockSpec((1, tm, NSA_WIDTH), row),
            pl.BlockSpec((1, tm, NSA_WIDTH), row),
            pl.BlockSpec((1, tm, NSA_WIDTH), row),
            pl.BlockSpec((1, tm, LANE), row),
            pl.BlockSpec((1, NSA_HEADS, tm, LANE), lambda b, i: (b, SLOT_ZA // NSA_HEADS, i, 0)),
            pl.BlockSpec((1, tm, DIFF_WIDTH), row),
            pl.BlockSpec((1, DIFF_WIDTH // LANE, tm, LANE), lambda b, i: (b, SLOT_ZB // (DIFF_WIDTH // LANE), i, 0)),
            _resident((MIX_WIDTH, D_MODEL), lambda b, i: (0, 0)),
            pl.BlockSpec((1, D_MODEL), lambda b, i: (0, 0)),
        ],
        out_specs=pl.BlockSpec((1, tm, D_MODEL), row),
        out_shape=jax.ShapeDtypeStruct((batch, t_len, D_MODEL), F32),
        compiler_params=_cparams(("parallel", "parallel")),
        name="out_proj",
    )(x, o_cmp, o_sel, o_win, gl, p, o_b, p, w_out, final_w)


def _reorder_w_in(w):
    def section(off, n_slots):
        return w[:, off:off + n_slots * LANE]
    cols = [section(_O_QA, 8), section(_O_QB, 8), section(_O_KB, 8), section(_O_KC, 2), section(_O_KS, 2),
            section(_O_KW, 2), section(_O_VC, 2), section(_O_ZA, 8), section(_O_ZB, 8), section(_O_VB, 8),
            section(_O_VS, 2), section(_O_VW, 2)]
    w_r = jnp.concatenate(cols, axis=1).astype(BF16)
    assert w_r.shape[1] == N_SLOTS * LANE
    w_g = jnp.pad(w[:, _O_G:_O_G + NSA_HEADS * NSA_BRANCHES],
                  ((0, 0), (0, LANE - NSA_HEADS * NSA_BRANCHES))).astype(BF16)
    return w_r, w_g


def _sel_map(t_len, n_selp):
    n_rows = t_len // CMP_STRIDE
    c0 = np.arange(n_rows)[:, None] * CMP_STRIDE
    s0 = np.arange(n_selp)[None, :] * SEL_BLOCK
    ov = np.minimum(c0 + CMP_BLOCK, s0 + SEL_BLOCK) - np.maximum(c0, s0)
    m = np.maximum(ov, 0) / CMP_BLOCK
    m[n_rows - 1:] = 0.0
    m[:, t_len // SEL_BLOCK:] = 0.0
    return jnp.asarray(m.T, dtype=BF16)


def _block_one_hot(t_len):
    blk = (np.arange(t_len) // SEL_BLOCK) % LANE
    return jnp.asarray(blk[:, None] == np.arange(LANE)[None, :], dtype=BF16)


def _rope_tables(t_len):
    inv = ROPE_THETA ** (-np.arange(0, HEAD_DIM, 2, dtype=np.float64) / HEAD_DIM)
    ang = np.arange(t_len, dtype=np.float64)[:, None] * inv[None, :]
    cos, sin = np.cos(ang), np.sin(ang)
    return (jnp.asarray(np.concatenate([cos, cos], axis=1), dtype=F32),
            jnp.asarray(np.concatenate([-sin, sin], axis=1), dtype=F32))


class _Tiles(NamedTuple):
    in_tm: int
    out_tm: int
    win_tq: int
    attn_tq: int
    diag_rb: int
    sel_tk: int
    sel_rl: int
    diff_tk: int
    diff_rl: int


def _tiles(t_len):
    return _Tiles(in_tm=min(1024, t_len), out_tm=min(512, t_len), win_tq=min(512, t_len),
                  attn_tq=min(2048, t_len), diag_rb=256, sel_tk=512, sel_rl=512, diff_tk=1024, diff_rl=128)


def kernel(x, norm_w, w_in, cmp_pos, cmp_k_w1, cmp_k_w2, cmp_v_w1, cmp_v_w2, lam_q1, lam_k1, lam_q2, lam_k2,
           subln_w, w_out, final_norm_w):
    batch, t_len, _ = x.shape
    n_rows = t_len // CMP_STRIDE
    n_selp = -(-(t_len // SEL_BLOCK) // LANE) * LANE

    tiles = _tiles(t_len)
    cos_f, sin_f = _rope_tables(t_len)
    w_r, w_g = _reorder_w_in(w_in[0])
    p, gl, pc = _in_proj(x.reshape(batch * t_len, D_MODEL), norm_w[0][None, :], w_r, w_g, cos_f, sin_f,
                         batch, t_len, tiles.in_tm)

    half = CMP_STRIDE * HEAD_DIM
    w1cat = jnp.stack([jnp.concatenate([w[:half], w[half:]], axis=1) for w in (cmp_k_w1[0], cmp_v_w1[0])]).astype(BF16)
    w2 = jnp.stack([cmp_k_w2[0], cmp_v_w2[0]]).astype(BF16)
    pos2 = jnp.broadcast_to(cmp_pos[0].reshape(2, 1, half), (2, SUBLANE, half)).astype(BF16)
    kvc = _compress(pc, w1cat, pos2, w2, batch, n_rows)

    o_cmp, bias = _cmp_select(p, kvc, _sel_map(t_len, n_selp), batch, t_len)
    o_sel = _sel_attn(p, bias, _block_one_hot(t_len), batch, t_len,
                      tq=tiles.attn_tq, tk=tiles.sel_tk, rb=tiles.diag_rb, rl=tiles.sel_rl)
    o_win = _win_attn(p, batch, t_len, tq=tiles.win_tq)

    lam4 = jnp.pad(jnp.stack([lam_q1[0], lam_k1[0], lam_q2[0], lam_k2[0]]), ((0, SUBLANE - 4), (0, 0)))
    o_b = _diff_attn(p, lam4, subln_w[0][None, :], batch, t_len,
                     tq=tiles.attn_tq, tk=tiles.diff_tk, rb=tiles.diag_rb, rl=tiles.diff_rl)

    return _out_proj(x, o_cmp, o_sel, o_win, gl, p, o_b, w_out[0].astype(BF16), final_norm_w[None, :],
                     batch, t_len, tm=tiles.out_tm)
```

```python
import functools
import math
from typing import NamedTuple

import numpy as np
import jax
import jax.numpy as jnp
from jax import lax
from jax.experimental import pallas as pl
from jax.experimental.pallas import tpu as pltpu

F32 = jnp.float32
BF16 = jnp.bfloat16

D_MODEL = 2048
HEAD_DIM = 128
NSA_HEADS = 8
NSA_GROUPS = 2
NSA_HPG = NSA_HEADS // NSA_GROUPS
NSA_WIDTH = NSA_HEADS * HEAD_DIM
NSA_KV = NSA_GROUPS * HEAD_DIM
NSA_BRANCHES = 3
CMP_BLOCK = 32
CMP_STRIDE = 16
CMP_HIDDEN = 512
SEL_BLOCK = 64
N_SELECT = 16
WINDOW = 512
DIFF_HEADS = 4
DIFF_QK = 128
DIFF_V = 256
DIFF_WIDTH = DIFF_HEADS * DIFF_V
MIX_WIDTH = NSA_WIDTH + DIFF_WIDTH
Q_BLOCK = 128
ROPE_THETA = 10000.0
NORM_EPS = 1e-6
SUBLN_EPS = 1e-5
NEG_INF = -1e30
LAMBDA_INIT = 0.8 - 0.6 * math.exp(-0.3 * 0)

LANE = 128
SUBLANE = 8
VMEM_LIMIT = 56 * 1024 * 1024

SLOT_QA = 0
SLOT_QB = 8
SLOT_KB = 16
SLOT_KC = 24
SLOT_KS = 26
SLOT_KW = 28
N_ROPE_SLOTS = 30
SLOT_VC = 30
SLOT_ZA = 32
SLOT_ZB = 40
SLOT_VB = 48
SLOT_VS = 56
SLOT_VW = 58
N_SLOTS = 60
SLOTS_PER_STEP = 12
N_QUERY_SLOTS = 16
QK_EXP2_SCALE = (HEAD_DIM ** -0.5) * math.log2(math.e)

_OFF = np.concatenate([[0], np.cumsum([NSA_WIDTH, NSA_KV, NSA_KV, NSA_KV, NSA_KV, NSA_KV, NSA_KV,
                                       NSA_HEADS * NSA_BRANCHES, NSA_WIDTH, 1024, 1024, DIFF_WIDTH, DIFF_WIDTH])])
(_O_QA, _O_KC, _O_VC, _O_KS, _O_VS, _O_KW, _O_VW, _O_G, _O_ZA, _O_QB, _O_KB, _O_VB, _O_ZB, _O_END) = [int(v) for v in _OFF]


def _cparams(sem):
    return pltpu.CompilerParams(dimension_semantics=sem, vmem_limit_bytes=VMEM_LIMIT)


def _in_proj_kernel(x_ref, nw_ref, w_ref, wg_ref, cos_ref, sin_ref, p_ref, g_ref, c_ref, h_ref, stage_ref):
    j = pl.program_id(1)

    @pl.when(j == 0)
    def _():
        x = x_ref[...]
        y = x * lax.rsqrt(jnp.mean(x * x, axis=-1, keepdims=True) + NORM_EPS)
        h = (y * nw_ref[...]).astype(BF16)
        h_ref[...] = h
        g_ref[0] = jnp.dot(h, wg_ref[...], preferred_element_type=F32)

    acc = jnp.dot(h_ref[...], w_ref[...], preferred_element_type=F32)

    cos_t = cos_ref[...]
    sin_t = sin_ref[...]
    for s in range(SLOTS_PER_STEP):
        slot = j * SLOTS_PER_STEP + s
        cos = jnp.where(slot < N_ROPE_SLOTS, cos_t, 1.0)
        sin = jnp.where(slot < N_ROPE_SLOTS, sin_t, 0.0)
        a = acc[:, s * LANE:(s + 1) * LANE]
        r = a * cos + pltpu.roll(a, HEAD_DIM // 2, 1) * sin
        f = jnp.where(slot < N_QUERY_SLOTS, QK_EXP2_SCALE, 1.0)
        p_ref[0, s] = (r * f).astype(BF16)

    @pl.when(j == SLOT_KC // SLOTS_PER_STEP)
    def _():
        n_out = stage_ref.shape[0] // CMP_STRIDE
        for n, slot in enumerate((SLOT_KC, SLOT_KC + 1, SLOT_VC, SLOT_VC + 1)):
            stage_ref[...] = p_ref[0, slot % SLOTS_PER_STEP].astype(F32)
            for l in range(CMP_STRIDE):
                c_ref[0, n, :, l * LANE:(l + 1) * LANE] = stage_ref[pl.ds(l, n_out, stride=CMP_STRIDE), :].astype(BF16)


def _in_proj(x2, norm_w, w_r, w_g, cos_f, sin_f, batch, t_len, tm):
    m = x2.shape[0]
    tiles_per_seq = t_len // tm
    assert SLOT_KC // SLOTS_PER_STEP == (SLOT_VC + 1) // SLOTS_PER_STEP
    n_steps = N_SLOTS // SLOTS_PER_STEP
    tn = SLOTS_PER_STEP * LANE
    return pl.pallas_call(
        _in_proj_kernel,
        grid=(m // tm, n_steps),
        in_specs=[
            pl.BlockSpec((tm, D_MODEL), lambda i, j: (i, 0)),
            pl.BlockSpec((1, D_MODEL), lambda i, j: (0, 0)),
            pl.BlockSpec((D_MODEL, tn), lambda i, j: (0, j)),
            pl.BlockSpec((D_MODEL, LANE), lambda i, j: (0, 0)),
            pl.BlockSpec((tm, LANE), lambda i, j: (i % tiles_per_seq, 0)),
            pl.BlockSpec((tm, LANE), lambda i, j: (i % tiles_per_seq, 0)),
        ],
        out_specs=[
            pl.BlockSpec((1, SLOTS_PER_STEP, tm, LANE), lambda i, j: (i // tiles_per_seq, j, i % tiles_per_seq, 0)),
            pl.BlockSpec((1, tm, LANE), lambda i, j: (i // tiles_per_seq, i % tiles_per_seq, 0)),
            pl.BlockSpec((1, 2 * NSA_GROUPS, tm // CMP_STRIDE, CMP_STRIDE * HEAD_DIM),
                         lambda i, j: (i // tiles_per_seq, 0, i % tiles_per_seq, 0)),
        ],
        out_shape=[
            jax.ShapeDtypeStruct((batch, N_SLOTS, t_len, LANE), BF16),
            jax.ShapeDtypeStruct((batch, t_len, LANE), F32),
            jax.ShapeDtypeStruct((batch, 2 * NSA_GROUPS, t_len // CMP_STRIDE, CMP_STRIDE * HEAD_DIM), BF16),
        ],
        scratch_shapes=[pltpu.VMEM((tm, D_MODEL), BF16), pltpu.VMEM((tm, LANE), F32)],
        compiler_params=_cparams(("parallel", "arbitrary")),
        name="in_proj",
    )(x2, norm_w, w_r, w_g, cos_f, sin_f)


def _compress_kernel(r_ref, w1_ref, pos_ref, w2_ref, o_ref, *, n_rows):
    w1 = w1_ref[0]
    a = jnp.dot(r_ref[0, 0], w1, preferred_element_type=F32)
    c = (jnp.dot(pos_ref[0], w1[:, :CMP_HIDDEN], preferred_element_type=F32)
         + jnp.dot(pos_ref[1], w1[:, CMP_HIDDEN:], preferred_element_type=F32))[0:1]
    nxt = pltpu.roll(a[:, CMP_HIDDEN:], n_rows - 1, 0)
    hid = a[:, :CMP_HIDDEN] + nxt + c
    hid = hid * jax.nn.sigmoid(hid)
    out = jnp.dot(hid.astype(BF16), w2_ref[0], preferred_element_type=F32)
    row = lax.broadcasted_iota(jnp.int32, out.shape, 0)
    o_ref[0, 0, 0] = jnp.where(row < n_rows - 1, out, 0.0).astype(BF16)


def _compress(p16, w1cat, pos2, w2, batch, n_rows):
    kern = functools.partial(_compress_kernel, n_rows=n_rows)

    return pl.pallas_call(
        kern,
        grid=(batch, 2, NSA_GROUPS),
        in_specs=[
            pl.BlockSpec((1, 1, n_rows, CMP_STRIDE * HEAD_DIM), lambda b, kv, g: (b, kv * NSA_GROUPS + g, 0, 0)),
            pl.BlockSpec((1, CMP_STRIDE * HEAD_DIM, 2 * CMP_HIDDEN), lambda b, kv, g: (kv, 0, 0)),
            pl.BlockSpec((2, SUBLANE, CMP_STRIDE * HEAD_DIM), lambda b, kv, g: (0, 0, 0)),
            pl.BlockSpec((1, CMP_HIDDEN, HEAD_DIM), lambda b, kv, g: (kv, 0, 0)),
        ],
        out_specs=pl.BlockSpec((1, 1, 1, n_rows, HEAD_DIM), lambda b, kv, g: (b, kv, g, 0, 0)),
        out_shape=jax.ShapeDtypeStruct((batch, 2, NSA_GROUPS, n_rows, HEAD_DIM), BF16),
        compiler_params=_cparams(("parallel", "parallel", "parallel")),
        name="compress",
    )(p16, w1cat, pos2, w2)


_NT = (((1,), (1,)), ((), ()))
CMP_PER_SEL = SEL_BLOCK // CMP_STRIDE


def _cmp_select_body(q_ref, kc_ref, vc_ref, map_ref, o_ref, bias_ref, start, g, q_rows, *, n_top, w, bw):
    n_selp = map_ref.shape[0]
    q = q_ref[0, g * NSA_HPG:(g + 1) * NSA_HPG, q_rows, :].reshape(NSA_HPG * Q_BLOCK, HEAD_DIM)
    s = lax.dot_general(q, kc_ref[0, 0, g, :w, :], _NT, preferred_element_type=F32)
    s = s.reshape(NSA_HPG, Q_BLOCK, w)
    t_q = start + lax.broadcasted_iota(jnp.int32, (1, Q_BLOCK, w), 1)
    n_id = lax.broadcasted_iota(jnp.int32, (1, Q_BLOCK, w), 2)
    valid = n_id * CMP_STRIDE + (CMP_BLOCK - 1) <= t_q
    s = jnp.where(valid, s, NEG_INF)
    e = jnp.exp2(s - jnp.max(s, axis=-1, keepdims=True))
    den = jnp.sum(e, axis=-1, keepdims=True)
    sees_any = start + lax.broadcasted_iota(jnp.int32, (1, Q_BLOCK, 1), 1) >= CMP_BLOCK - 1
    p = e * jnp.where(sees_any, 1.0 / den, 0.0)
    o = jnp.dot(p.reshape(NSA_HPG * Q_BLOCK, w).astype(BF16), vc_ref[0, 0, g, :w, :],
                preferred_element_type=F32)
    for h in range(NSA_HPG):
        hd = g * NSA_HPG + h
        o_ref[0, q_rows, hd * HEAD_DIM:(hd + 1) * HEAD_DIM] = o[h * Q_BLOCK:(h + 1) * Q_BLOCK].astype(BF16)

    psum = p[0] + p[1] + p[2] + p[3]
    p_hi = psum.astype(BF16)
    p_lo = (psum - p_hi.astype(F32)).astype(BF16)
    smap_t = map_ref[:bw, :w]
    imp = (lax.dot_general(smap_t, p_hi, _NT, preferred_element_type=F32)
           + lax.dot_general(smap_t, p_lo, _NT, preferred_element_type=F32))
    blk = lax.broadcasted_iota(jnp.int32, (bw, Q_BLOCK), 0)
    cur = (start + lax.broadcasted_iota(jnp.int32, (bw, Q_BLOCK), 1)) // SEL_BLOCK
    forced = (blk == 0) | (blk == cur) | (blk == cur - 1)
    blk_f = blk.astype(F32)
    work = jnp.where(forced, -jnp.inf, jnp.where(blk > cur, NEG_INF, imp))
    for _ in range(n_top - 3):
        top = jnp.max(work, axis=0, keepdims=True)
        first = jnp.min(jnp.where(work == top, blk_f, 1e9), axis=0, keepdims=True)
        work = jnp.where(blk_f == first, -jnp.inf, work)
    keep = (work == -jnp.inf) & (blk <= cur)
    bias_t = jnp.where(keep, 0.0, NEG_INF)
    if bw < n_selp:
        bias_t = jnp.concatenate([bias_t, jnp.full((n_selp - bw, Q_BLOCK), NEG_INF, F32)], axis=0)
    bias_ref[0, g, q_rows, :] = bias_t.T.astype(BF16)


def _cmp_select_kernel(q_ref, kc_ref, vc_ref, map_ref, o_ref, bias_ref, *, n_top, chunk, tq):
    start = pl.program_id(1) * tq
    variant = ((start + tq - 1) // SEL_BLOCK) // (chunk // CMP_PER_SEL)

    def all_chains(w):
        for qc in range(tq // Q_BLOCK):
            for g in range(NSA_GROUPS):
                _cmp_select_body(q_ref, kc_ref, vc_ref, map_ref, o_ref, bias_ref, start + qc * Q_BLOCK, g,
                                 slice(qc * Q_BLOCK, (qc + 1) * Q_BLOCK),
                                 n_top=n_top, w=w, bw=min(w // CMP_PER_SEL, map_ref.shape[0]))

    for v in range(map_ref.shape[1] // chunk):
        pl.when(variant == v)(functools.partial(all_chains, (v + 1) * chunk))


def _cmp_select(p, kvc, sel_map, batch, t_len, tq):
    n_rows = t_len // CMP_STRIDE
    n_selp = sel_map.shape[0]
    n_top = min(N_SELECT, t_len // SEL_BLOCK)
    assert n_top >= 3
    chunk = min(256, n_rows)
    assert n_rows % chunk == 0 and (chunk // CMP_PER_SEL) % 8 == 0
    kern = functools.partial(_cmp_select_kernel, n_top=n_top, chunk=chunk, tq=tq)
    return pl.pallas_call(
        kern,
        grid=(batch, t_len // tq),
        in_specs=[
            pl.BlockSpec((1, NSA_HEADS, tq, HEAD_DIM), lambda b, c: (b, 0, c, 0)),
            pl.BlockSpec((1, 1, NSA_GROUPS, n_rows, HEAD_DIM), lambda b, c: (b, 0, 0, 0, 0)),
            pl.BlockSpec((1, 1, NSA_GROUPS, n_rows, HEAD_DIM), lambda b, c: (b, 1, 0, 0, 0)),
            pl.BlockSpec((n_selp, n_rows), lambda b, c: (0, 0)),
        ],
        out_specs=[
            pl.BlockSpec((1, tq, NSA_WIDTH), lambda b, c: (b, c, 0)),
            pl.BlockSpec((1, NSA_GROUPS, tq, n_selp), lambda b, c: (b, 0, c, 0)),
        ],
        out_shape=[
            jax.ShapeDtypeStruct((batch, t_len, NSA_WIDTH), BF16),
            jax.ShapeDtypeStruct((batch, NSA_GROUPS, t_len, n_selp), BF16),
        ],
        compiler_params=_cparams(("parallel", "parallel")),
        name="cmp_select",
    )(p, kvc, kvc, sel_map)


def _lane_tile(x, n):
    return x if n == 1 else jnp.concatenate([x] * n, axis=1)


def _flash_update(s, v, rows, m_ref, l_ref, acc_ref):
    n_keys = s.shape[1]
    dv = acc_ref.shape[-1]
    m_prev = m_ref[rows, :]
    m_new = jnp.maximum(m_prev, jnp.max(s, axis=-1, keepdims=True))
    alpha = jnp.exp2(m_prev - m_new)
    p = jnp.exp2(s - _lane_tile(m_new, n_keys // LANE))
    if l_ref is not None:
        psum = p[:, :LANE]
        for cc in range(1, n_keys // LANE):
            psum = psum + p[:, cc * LANE:(cc + 1) * LANE]
        l_ref[rows, :] = alpha * l_ref[rows, :] + psum
    acc_ref[rows, :] = (_lane_tile(alpha, dv // LANE) * acc_ref[rows, :]
                        + jnp.dot(p.astype(BF16), v, preferred_element_type=F32))
    m_ref[rows, :] = m_new


def _causal_triangle(rb):
    return lax.broadcasted_iota(jnp.int32, (rb, rb), 1) <= lax.broadcasted_iota(jnp.int32, (rb, rb), 0)


def _sel_attn_kernel(q_ref, bias_ref, k_ref, v_ref, hot_ref, o_ref, qa_ref, m_ref, acc_ref,
                     *, tq, tk, rb, rl, n_halves):
    start = pl.program_id(2) * tq
    n_rows = NSA_HPG * tq
    keys_per_half = LANE * SEL_BLOCK
    for hf in range(n_halves):
        for h in range(NSA_HPG):
            qa_ref[hf, h * tq:(h + 1) * tq, :LANE] = q_ref[0, h]
            qa_ref[hf, h * tq:(h + 1) * tq, LANE:] = bias_ref[0, 0, :, hf * LANE:(hf + 1) * LANE]
    m_ref[...] = jnp.full(m_ref.shape, NEG_INF, F32)
    acc_ref[...] = jnp.zeros(acc_ref.shape, F32)

    def k_aug(k0, n):
        return jnp.concatenate([k_ref[0, 0, pl.ds(k0, n), :], hot_ref[pl.ds(k0, n), :]], axis=1)

    def v_aug(k0, n):
        return jnp.concatenate([v_ref[0, 0, pl.ds(k0, n), :], jnp.ones((n, LANE), BF16)], axis=1)

    def body(j, carry):
        k0 = pl.multiple_of(j * tk, tk)
        ka, v, hf = k_aug(k0, tk), v_aug(k0, tk), k0 // keys_per_half
        for r in range(n_rows // rl):
            rows = pl.ds(r * rl, rl)
            s = lax.dot_general(qa_ref[hf, rows, :], ka, _NT, preferred_element_type=F32)
            _flash_update(s, v, rows, m_ref, None, acc_ref)
        return carry

    lax.fori_loop(0, start // tk, body, 0)

    tri = _causal_triangle(rb)
    for d in range(tq // rb):
        k0 = pl.multiple_of(start + d * rb, rb)
        ka, v, hf = k_aug(k0, rb), v_aug(k0, rb), k0 // keys_per_half
        for r in range(n_rows // rb):
            q_off = (r * rb) % tq
            if d * rb > q_off:
                continue
            rows = pl.ds(r * rb, rb)
            s = lax.dot_general(qa_ref[hf, rows, :], ka, _NT, preferred_element_type=F32)
            if d * rb == q_off:
                s = jnp.where(tri, s, NEG_INF)
            _flash_update(s, v, rows, m_ref, None, acc_ref)

    for h in range(NSA_HPG):
        hr = slice(h * tq, (h + 1) * tq)
        o_ref[0, :, h * HEAD_DIM:(h + 1) * HEAD_DIM] = (acc_ref[hr, :HEAD_DIM] / acc_ref[hr, HEAD_DIM:]).astype(BF16)


def _resident(block_shape, index_map):
    return pl.BlockSpec(block_shape, index_map, pipeline_mode=pl.Buffered(1))


def _sel_attn(p, bias, hot, batch, t_len, tq, tk, rb, rl):
    assert tq % tk == 0 and tk % rb == 0 and (LANE * SEL_BLOCK) % tk == 0
    n_selp = bias.shape[-1]
    n_halves = n_selp // LANE
    n_rows = NSA_HPG * tq
    assert n_rows % rl == 0
    kern = functools.partial(_sel_attn_kernel, tq=tq, tk=tk, rb=rb, rl=rl, n_halves=n_halves)
    return pl.pallas_call(
        kern,
        grid=(batch, NSA_GROUPS, t_len // tq),
        in_specs=[
            pl.BlockSpec((1, NSA_HPG, tq, HEAD_DIM), lambda b, g, c: (b, g, c, 0)),
            pl.BlockSpec((1, 1, tq, n_selp), lambda b, g, c: (b, g, c, 0)),
            _resident((1, 1, t_len, HEAD_DIM), lambda b, g, c: (b, SLOT_KS + g, 0, 0)),
            _resident((1, 1, t_len, HEAD_DIM), lambda b, g, c: (b, SLOT_VS + g, 0, 0)),
            _resident((t_len, LANE), lambda b, g, c: (0, 0)),
        ],
        out_specs=pl.BlockSpec((1, tq, NSA_HPG * HEAD_DIM), lambda b, g, c: (b, c, g)),
        out_shape=jax.ShapeDtypeStruct((batch, t_len, NSA_WIDTH), BF16),
        scratch_shapes=[
            pltpu.VMEM((n_halves, n_rows, 2 * LANE), BF16),
            pltpu.VMEM((n_rows, LANE), F32),
            pltpu.VMEM((n_rows, 2 * HEAD_DIM), F32),
        ],
        compiler_params=_cparams(("parallel", "parallel", "arbitrary")),
        name="sel_attn",
    )(p, bias, p, p, hot)


def _win_attn_kernel(q_ref, k_ref, v_ref, o_ref, *, tq, span):
    start = pl.program_id(1) * tq
    rows = NSA_HPG * Q_BLOCK
    for qc in range(tq // Q_BLOCK):
        q_rows = slice(qc * Q_BLOCK, (qc + 1) * Q_BLOCK)
        q0 = start + qc * Q_BLOCK
        k0 = pl.multiple_of(jnp.maximum(q0 + Q_BLOCK - span, 0), Q_BLOCK)
        t_q = q0 + lax.broadcasted_iota(jnp.int32, (1, Q_BLOCK, span), 1)
        k_pos = k0 + lax.broadcasted_iota(jnp.int32, (1, Q_BLOCK, span), 2)
        valid = (k_pos <= t_q) & (k_pos > t_q - WINDOW)
        for g in range(NSA_GROUPS):
            q = q_ref[0, g * NSA_HPG:(g + 1) * NSA_HPG, q_rows, :].reshape(rows, HEAD_DIM)
            s = lax.dot_general(q, k_ref[0, g, pl.ds(k0, span), :], _NT, preferred_element_type=F32)
            s = jnp.where(valid, s.reshape(NSA_HPG, Q_BLOCK, span), NEG_INF)
            e = jnp.exp2(s - jnp.max(s, axis=-1, keepdims=True))
            v = jnp.concatenate([v_ref[0, g, pl.ds(k0, span), :], jnp.ones((span, LANE), BF16)], axis=1)
            o = jnp.dot(e.reshape(rows, span).astype(BF16), v, preferred_element_type=F32)
            o = o[:, :HEAD_DIM] / o[:, HEAD_DIM:]
            for h in range(NSA_HPG):
                hd = g * NSA_HPG + h
                o_ref[0, q_rows, hd * HEAD_DIM:(hd + 1) * HEAD_DIM] = o[h * Q_BLOCK:(h + 1) * Q_BLOCK].astype(BF16)


def _win_attn(p, batch, t_len, tq):
    span = min(WINDOW + Q_BLOCK, t_len)
    kern = functools.partial(_win_attn_kernel, tq=tq, span=span)
    return pl.pallas_call(
        kern,
        grid=(batch, t_len // tq),
        in_specs=[
            pl.BlockSpec((1, NSA_HEADS, tq, HEAD_DIM), lambda b, c: (b, 0, c, 0)),
            _resident((1, NSA_GROUPS, t_len, HEAD_DIM), lambda b, c: (b, SLOT_KW // NSA_GROUPS, 0, 0)),
            _resident((1, NSA_GROUPS, t_len, HEAD_DIM), lambda b, c: (b, SLOT_VW // NSA_GROUPS, 0, 0)),
        ],
        out_specs=pl.BlockSpec((1, tq, NSA_WIDTH), lambda b, c: (b, c, 0)),
        out_shape=jax.ShapeDtypeStruct((batch, t_len, NSA_WIDTH), BF16),
        compiler_params=_cparams(("parallel", "parallel")),
        name="win_attn",
    )(p, p, p)


def _diff_attn_kernel(q1_ref, q2_ref, k1_ref, k2_ref, v_ref, lam_ref, sw_ref, o_ref,
                      m_ref, l_ref, acc_ref, *, tq, tk, rb, rl):
    start = pl.program_id(2) * tq
    m_ref[...] = jnp.full(m_ref.shape, NEG_INF, F32)
    l_ref[...] = jnp.zeros(l_ref.shape, F32)
    acc_ref[...] = jnp.zeros(acc_ref.shape, F32)
    q_refs = (q1_ref, q2_ref)
    k_refs = (k1_ref, k2_ref)

    def v_tile(k0, n):
        return jnp.concatenate([v_ref[0, 0, pl.ds(k0, n), :], v_ref[0, 1, pl.ds(k0, n), :]], axis=1)

    def body(j, carry):
        k0 = pl.multiple_of(j * tk, tk)
        v = v_tile(k0, tk)
        for mp in range(2):
            kt = k_refs[mp][0, 0, pl.ds(k0, tk), :]
            for r in range(tq // rl):
                rows = pl.ds(r * rl, rl)
                s = lax.dot_general(q_refs[mp][0, 0, rows, :], kt, _NT, preferred_element_type=F32)
                _flash_update(s, v, rows, m_ref.at[mp], l_ref.at[mp], acc_ref.at[mp])
        return carry

    lax.fori_loop(0, start // tk, body, 0)

    tri = _causal_triangle(rb)
    for d in range(tq // rb):
        k0 = pl.multiple_of(start + d * rb, rb)
        v = v_tile(k0, rb)
        for mp in range(2):
            kt = k_refs[mp][0, 0, pl.ds(k0, rb), :]
            for r in range(d, tq // rb):
                rows = pl.ds(r * rb, rb)
                s = lax.dot_general(q_refs[mp][0, 0, rows, :], kt, _NT, preferred_element_type=F32)
                if r == d:
                    s = jnp.where(tri, s, NEG_INF)
                _flash_update(s, v, rows, m_ref.at[mp], l_ref.at[mp], acc_ref.at[mp])

    lv = lam_ref[...]
    lam = (jnp.exp(jnp.sum(lv[0:1] * lv[1:2], axis=-1, keepdims=True))
           - jnp.exp(jnp.sum(lv[2:3] * lv[3:4], axis=-1, keepdims=True)) + LAMBDA_INIT)
    l1 = jnp.sum(l_ref[0], axis=-1, keepdims=True)
    l2 = jnp.sum(l_ref[1], axis=-1, keepdims=True)
    o = acc_ref[0] / l1 - lam * (acc_ref[1] / l2)
    y = o * lax.rsqrt(jnp.mean(o * o, axis=-1, keepdims=True) + SUBLN_EPS)
    o_ref[0] = (y * sw_ref[...] * (1.0 - LAMBDA_INIT)).astype(BF16)


def _diff_attn(p, lam4, subln_w, batch, t_len, tq, tk, rb, rl):
    assert tq % tk == 0 and tk % rb == 0 and tq % rl == 0
    kern = functools.partial(_diff_attn_kernel, tq=tq, tk=tk, rb=rb, rl=rl)
    return pl.pallas_call(
        kern,
        grid=(batch, DIFF_HEADS, t_len // tq),
        in_specs=[
            pl.BlockSpec((1, 1, tq, DIFF_QK), lambda b, h, i: (b, SLOT_QB + h, i, 0)),
            pl.BlockSpec((1, 1, tq, DIFF_QK), lambda b, h, i: (b, SLOT_QB + DIFF_HEADS + h, i, 0)),
            _resident((1, 1, t_len, DIFF_QK), lambda b, h, i: (b, SLOT_KB + h, 0, 0)),
            _resident((1, 1, t_len, DIFF_QK), lambda b, h, i: (b, SLOT_KB + DIFF_HEADS + h, 0, 0)),
            _resident((1, 2, t_len, LANE), lambda b, h, i: (b, SLOT_VB // 2 + h, 0, 0)),
            pl.BlockSpec((SUBLANE, DIFF_QK), lambda b, h, i: (0, 0)),
            pl.BlockSpec((1, DIFF_V), lambda b, h, i: (0, 0)),
        ],
        out_specs=pl.BlockSpec((1, tq, DIFF_V), lambda b, h, i: (b, i, h)),
        out_shape=jax.ShapeDtypeStruct((batch, t_len, DIFF_WIDTH), BF16),
        scratch_shapes=[
            pltpu.VMEM((2, tq, LANE), F32),
            pltpu.VMEM((2, tq, LANE), F32),
            pltpu.VMEM((2, tq, DIFF_V), F32),
        ],
        compiler_params=_cparams(("parallel", "parallel", "arbitrary")),
        name="diff_attn",
    )(p, p, p, p, p, lam4, subln_w)


OUT_K_SLOTS = 2


def _out_proj_kernel(x_ref, oc_ref, os_ref, ow_ref, g_ref, za_ref, ob_ref, zb_ref, w_ref, fw_ref, o_ref):
    gates = jax.nn.sigmoid(g_ref[0])
    tm = gates.shape[0]

    def nsa_piece(hd):
        cols = slice(hd * HEAD_DIM, (hd + 1) * HEAD_DIM)
        g0 = jnp.broadcast_to(gates[:, 3 * hd:3 * hd + 1], (tm, HEAD_DIM))
        g1 = jnp.broadcast_to(gates[:, 3 * hd + 1:3 * hd + 2], (tm, HEAD_DIM))
        g2 = jnp.broadcast_to(gates[:, 3 * hd + 2:3 * hd + 3], (tm, HEAD_DIM))
        o = (g0 * oc_ref[0, :, cols].astype(F32) + g1 * os_ref[0, :, cols].astype(F32)
             + g2 * ow_ref[0, :, cols].astype(F32))
        z = za_ref[0, hd].astype(F32)
        return (o * (z * jax.nn.sigmoid(z))).astype(BF16)

    def diff_piece(sl):
        z = zb_ref[0, sl].astype(F32)
        return (ob_ref[0, :, sl * LANE:(sl + 1) * LANE].astype(F32) * (z * jax.nn.sigmoid(z))).astype(BF16)

    pieces = [functools.partial(nsa_piece, hd) for hd in range(NSA_HEADS)]
    pieces += [functools.partial(diff_piece, sl) for sl in range(DIFF_WIDTH // LANE)]
    y = x_ref[0]
    for kc in range(0, len(pieces), OUT_K_SLOTS):
        chunk = jnp.concatenate([f() for f in pieces[kc:kc + OUT_K_SLOTS]], axis=1)
        y = y + jnp.dot(chunk, w_ref[kc * LANE:(kc + OUT_K_SLOTS) * LANE, :], preferred_element_type=F32)
    y = y * lax.rsqrt(jnp.mean(y * y, axis=-1, keepdims=True) + NORM_EPS)
    o_ref[0] = y * fw_ref[...]


def _out_proj(x, o_cmp, o_sel, o_win, gl, p, o_b, w_out, final_w, batch, t_len, tm):
    row = lambda b, i: (b, i, 0)
    return pl.pallas_call(
        _out_proj_kernel,
        grid=(batch, t_len // tm),
        in_specs=[
            pl.BlockSpec((1, tm, D_MODEL), row),
            pl.BlockSpec((1, tm, NSA_WIDTH), row),
            pl.BlockSpec((1, tm, NSA_WIDTH), row),
            pl.BlockSpec((1, tm, NSA_WIDTH), row),
            pl.BlockSpec((1, tm, LANE), row),
            pl.BlockSpec((1, NSA_HEADS, tm, LANE), lambda b, i: (b, SLOT_ZA // NSA_HEADS, i, 0)),
            pl.BlockSpec((1, tm, DIFF_WIDTH), row),
            pl.BlockSpec((1, DIFF_WIDTH // LANE, tm, LANE), lambda b, i: (b, SLOT_ZB // (DIFF_WIDTH // LANE), i, 0)),
            _resident((MIX_WIDTH, D_MODEL), lambda b, i: (0, 0)),
            pl.BlockSpec((1, D_MODEL), lambda b, i: (0, 0)),
        ],
        out_specs=pl.BlockSpec((1, tm, D_MODEL), row),
        out_shape=jax.ShapeDtypeStruct((batch, t_len, D_MODEL), F32),
        compiler_params=_cparams(("parallel", "parallel")),
        name="out_proj",
    )(x, o_cmp, o_sel, o_win, gl, p, o_b, p, w_out, final_w)


def _reorder_w_in(w):
    def section(off, n_slots):
        return w[:, off:off + n_slots * LANE]
    cols = [section(_O_QA, 8), section(_O_QB, 8), section(_O_KB, 8), section(_O_KC, 2), section(_O_KS, 2),
            section(_O_KW, 2), section(_O_VC, 2), section(_O_ZA, 8), section(_O_ZB, 8), section(_O_VB, 8),
            section(_O_VS, 2), section(_O_VW, 2)]
    w_r = jnp.concatenate(cols, axis=1).astype(BF16)
    assert w_r.shape[1] == N_SLOTS * LANE
    w_g = jnp.pad(w[:, _O_G:_O_G + NSA_HEADS * NSA_BRANCHES],
                  ((0, 0), (0, LANE - NSA_HEADS * NSA_BRANCHES))).astype(BF16)
    return w_r, w_g


def _sel_map(t_len, n_selp):
    n_rows = t_len // CMP_STRIDE
    c0 = np.arange(n_rows)[:, None] * CMP_STRIDE
    s0 = np.arange(n_selp)[None, :] * SEL_BLOCK
    ov = np.minimum(c0 + CMP_BLOCK, s0 + SEL_BLOCK) - np.maximum(c0, s0)
    m = np.maximum(ov, 0) / CMP_BLOCK
    m[n_rows - 1:] = 0.0
    m[:, t_len // SEL_BLOCK:] = 0.0
    return jnp.asarray(m.T, dtype=BF16)


def _block_one_hot(t_len):
    blk = (np.arange(t_len) // SEL_BLOCK) % LANE
    return jnp.asarray(blk[:, None] == np.arange(LANE)[None, :], dtype=BF16)


def _rope_tables(t_len):
    inv = ROPE_THETA ** (-np.arange(0, HEAD_DIM, 2, dtype=np.float64) / HEAD_DIM)
    ang = np.arange(t_len, dtype=np.float64)[:, None] * inv[None, :]
    cos, sin = np.cos(ang), np.sin(ang)
    return (jnp.asarray(np.concatenate([cos, cos], axis=1), dtype=F32),
            jnp.asarray(np.concatenate([-sin, sin], axis=1), dtype=F32))


class _Tiles(NamedTuple):
    in_tm: int
    out_tm: int
    cmp_tq: int
    win_tq: int
    attn_tq: int
    diag_rb: int
    sel_tk: int
    sel_rl: int
    diff_tk: int
    diff_rl: int


def _tiles(t_len):
    return _Tiles(in_tm=min(1024, t_len), out_tm=min(512, t_len), cmp_tq=min(256, t_len), win_tq=min(512, t_len),
                  attn_tq=min(2048, t_len), diag_rb=256, sel_tk=512, sel_rl=512, diff_tk=1024, diff_rl=128)


def kernel(x, norm_w, w_in, cmp_pos, cmp_k_w1, cmp_k_w2, cmp_v_w1, cmp_v_w2, lam_q1, lam_k1, lam_q2, lam_k2,
           subln_w, w_out, final_norm_w):
    batch, t_len, _ = x.shape
    n_rows = t_len // CMP_STRIDE
    n_selp = -(-(t_len // SEL_BLOCK) // LANE) * LANE

    tiles = _tiles(t_len)
    cos_f, sin_f = _rope_tables(t_len)
    w_r, w_g = _reorder_w_in(w_in[0])
    p, gl, pc = _in_proj(x.reshape(batch * t_len, D_MODEL), norm_w[0][None, :], w_r, w_g, cos_f, sin_f,
                         batch, t_len, tiles.in_tm)

    half = CMP_STRIDE * HEAD_DIM
    w1cat = jnp.stack([jnp.concatenate([w[:half], w[half:]], axis=1) for w in (cmp_k_w1[0], cmp_v_w1[0])]).astype(BF16)
    w2 = jnp.stack([cmp_k_w2[0], cmp_v_w2[0]]).astype(BF16)
    pos2 = jnp.broadcast_to(cmp_pos[0].reshape(2, 1, half), (2, SUBLANE, half)).astype(BF16)
    kvc = _compress(pc, w1cat, pos2, w2, batch, n_rows)

    o_cmp, bias = _cmp_select(p, kvc, _sel_map(t_len, n_selp), batch, t_len, tq=tiles.cmp_tq)
    o_sel = _sel_attn(p, bias, _block_one_hot(t_len), batch, t_len,
                      tq=tiles.attn_tq, tk=tiles.sel_tk, rb=tiles.diag_rb, rl=tiles.sel_rl)
    o_win = _win_attn(p, batch, t_len, tq=tiles.win_tq)

    lam4 = jnp.pad(jnp.stack([lam_q1[0], lam_k1[0], lam_q2[0], lam_k2[0]]), ((0, SUBLANE - 4), (0, 0)))
    o_b = _diff_attn(p, lam4, subln_w[0][None, :], batch, t_len,
                     tq=tiles.attn_tq, tk=tiles.diff_tk, rb=tiles.diag_rb, rl=tiles.diff_rl)

    return _out_proj(x, o_cmp, o_sel, o_win, gl, p, o_b, w_out[0].astype(BF16), final_norm_w[None, :],
                     batch, t_len, tm=tiles.out_tm)
```

```python
import functools
import math
from typing import NamedTuple

import numpy as np
import jax
import jax.numpy as jnp
from jax import lax
from jax.experimental import pallas as pl
from jax.experimental.pallas import tpu as pltpu

F32 = jnp.float32
BF16 = jnp.bfloat16

D_MODEL = 2048
HEAD_DIM = 128
NSA_HEADS = 8
NSA_GROUPS = 2
NSA_HPG = NSA_HEADS // NSA_GROUPS
NSA_WIDTH = NSA_HEADS * HEAD_DIM
NSA_KV = NSA_GROUPS * HEAD_DIM
NSA_BRANCHES = 3
CMP_BLOCK = 32
CMP_STRIDE = 16
CMP_HIDDEN = 512
SEL_BLOCK = 64
N_SELECT = 16
WINDOW = 512
DIFF_HEADS = 4
DIFF_QK = 128
DIFF_V = 256
DIFF_WIDTH = DIFF_HEADS * DIFF_V
MIX_WIDTH = NSA_WIDTH + DIFF_WIDTH
Q_BLOCK = 128
ROPE_THETA = 10000.0
NORM_EPS = 1e-6
SUBLN_EPS = 1e-5
NEG_INF = -1e30
LAMBDA_INIT = 0.8 - 0.6 * math.exp(-0.3 * 0)

LANE = 128
SUBLANE = 8
VMEM_LIMIT = 56 * 1024 * 1024

SLOT_QA = 0
SLOT_QB = 8
SLOT_KB = 16
SLOT_KC = 24
SLOT_KS = 26
SLOT_KW = 28
N_ROPE_SLOTS = 30
SLOT_VC = 30
SLOT_ZA = 32
SLOT_ZB = 40
SLOT_VB = 48
SLOT_VS = 56
SLOT_VW = 58
N_SLOTS = 60
SLOTS_PER_STEP = 12
N_QUERY_SLOTS = 16
QK_EXP2_SCALE = (HEAD_DIM ** -0.5) * math.log2(math.e)

_OFF = np.concatenate([[0], np.cumsum([NSA_WIDTH, NSA_KV, NSA_KV, NSA_KV, NSA_KV, NSA_KV, NSA_KV,
                                       NSA_HEADS * NSA_BRANCHES, NSA_WIDTH, 1024, 1024, DIFF_WIDTH, DIFF_WIDTH])])
(_O_QA, _O_KC, _O_VC, _O_KS, _O_VS, _O_KW, _O_VW, _O_G, _O_ZA, _O_QB, _O_KB, _O_VB, _O_ZB, _O_END) = [int(v) for v in _OFF]


def _cparams(sem):
    return pltpu.CompilerParams(dimension_semantics=sem, vmem_limit_bytes=VMEM_LIMIT)


def _in_proj_kernel(x_ref, nw_ref, w_ref, wg_ref, cos_ref, sin_ref, p_ref, g_ref, c_ref, h_ref, stage_ref):
    j = pl.program_id(1)

    @pl.when(j == 0)
    def _():
        x = x_ref[...]
        y = x * lax.rsqrt(jnp.mean(x * x, axis=-1, keepdims=True) + NORM_EPS)
        h = (y * nw_ref[...]).astype(BF16)
        h_ref[...] = h
        g_ref[0] = jnp.dot(h, wg_ref[...], preferred_element_type=F32)

    acc = jnp.dot(h_ref[...], w_ref[...], preferred_element_type=F32)

    cos_t = cos_ref[...]
    sin_t = sin_ref[...]
    for s in range(SLOTS_PER_STEP):
        slot = j * SLOTS_PER_STEP + s
        cos = jnp.where(slot < N_ROPE_SLOTS, cos_t, 1.0)
        sin = jnp.where(slot < N_ROPE_SLOTS, sin_t, 0.0)
        a = acc[:, s * LANE:(s + 1) * LANE]
        r = a * cos + pltpu.roll(a, HEAD_DIM // 2, 1) * sin
        f = jnp.where(slot < N_QUERY_SLOTS, QK_EXP2_SCALE, 1.0)
        p_ref[0, s] = (r * f).astype(BF16)

    @pl.when(j == SLOT_KC // SLOTS_PER_STEP)
    def _():
        n_out = stage_ref.shape[0] // CMP_STRIDE
        for n, slot in enumerate((SLOT_KC, SLOT_KC + 1, SLOT_VC, SLOT_VC + 1)):
            stage_ref[...] = p_ref[0, slot % SLOTS_PER_STEP].astype(F32)
            for l in range(CMP_STRIDE):
                c_ref[0, n, :, l * LANE:(l + 1) * LANE] = stage_ref[pl.ds(l, n_out, stride=CMP_STRIDE), :].astype(BF16)


def _in_proj(x2, norm_w, w_r, w_g, cos_f, sin_f, batch, t_len, tm):
    m = x2.shape[0]
    tiles_per_seq = t_len // tm
    assert SLOT_KC // SLOTS_PER_STEP == (SLOT_VC + 1) // SLOTS_PER_STEP
    n_steps = N_SLOTS // SLOTS_PER_STEP
    tn = SLOTS_PER_STEP * LANE
    return pl.pallas_call(
        _in_proj_kernel,
        grid=(m // tm, n_steps),
        in_specs=[
            pl.BlockSpec((tm, D_MODEL), lambda i, j: (i, 0)),
            pl.BlockSpec((1, D_MODEL), lambda i, j: (0, 0)),
            pl.BlockSpec((D_MODEL, tn), lambda i, j: (0, j)),
            pl.BlockSpec((D_MODEL, LANE), lambda i, j: (0, 0)),
            pl.BlockSpec((tm, LANE), lambda i, j: (i % tiles_per_seq, 0)),
            pl.BlockSpec((tm, LANE), lambda i, j: (i % tiles_per_seq, 0)),
        ],
        out_specs=[
            pl.BlockSpec((1, SLOTS_PER_STEP, tm, LANE), lambda i, j: (i // tiles_per_seq, j, i % tiles_per_seq, 0)),
            pl.BlockSpec((1, tm, LANE), lambda i, j: (i // tiles_per_seq, i % tiles_per_seq, 0)),
            pl.BlockSpec((1, 2 * NSA_GROUPS, tm // CMP_STRIDE, CMP_STRIDE * HEAD_DIM),
                         lambda i, j: (i // tiles_per_seq, 0, i % tiles_per_seq, 0)),
        ],
        out_shape=[
            jax.ShapeDtypeStruct((batch, N_SLOTS, t_len, LANE), BF16),
            jax.ShapeDtypeStruct((batch, t_len, LANE), F32),
            jax.ShapeDtypeStruct((batch, 2 * NSA_GROUPS, t_len // CMP_STRIDE, CMP_STRIDE * HEAD_DIM), BF16),
        ],
        scratch_shapes=[pltpu.VMEM((tm, D_MODEL), BF16), pltpu.VMEM((tm, LANE), F32)],
        compiler_params=_cparams(("parallel", "arbitrary")),
        name="in_proj",
    )(x2, norm_w, w_r, w_g, cos_f, sin_f)


def _compress_kernel(r_ref, w1_ref, pos_ref, w2_ref, o_ref, *, n_rows):
    w1 = w1_ref[0]
    a = jnp.dot(r_ref[0, 0], w1, preferred_element_type=F32)
    c = (jnp.dot(pos_ref[0], w1[:, :CMP_HIDDEN], preferred_element_type=F32)
         + jnp.dot(pos_ref[1], w1[:, CMP_HIDDEN:], preferred_element_type=F32))[0:1]
    nxt = pltpu.roll(a[:, CMP_HIDDEN:], n_rows - 1, 0)
    hid = a[:, :CMP_HIDDEN] + nxt + c
    hid = hid * jax.nn.sigmoid(hid)
    out = jnp.dot(hid.astype(BF16), w2_ref[0], preferred_element_type=F32)
    row = lax.broadcasted_iota(jnp.int32, out.shape, 0)
    o_ref[0, 0, 0] = jnp.where(row < n_rows - 1, out, 0.0).astype(BF16)


def _compress(p16, w1cat, pos2, w2, batch, n_rows):
    kern = functools.partial(_compress_kernel, n_rows=n_rows)

    return pl.pallas_call(
        kern,
        grid=(batch, 2, NSA_GROUPS),
        in_specs=[
            pl.BlockSpec((1, 1, n_rows, CMP_STRIDE * HEAD_DIM), lambda b, kv, g: (b, kv * NSA_GROUPS + g, 0, 0)),
            pl.BlockSpec((1, CMP_STRIDE * HEAD_DIM, 2 * CMP_HIDDEN), lambda b, kv, g: (kv, 0, 0)),
            pl.BlockSpec((2, SUBLANE, CMP_STRIDE * HEAD_DIM), lambda b, kv, g: (0, 0, 0)),
            pl.BlockSpec((1, CMP_HIDDEN, HEAD_DIM), lambda b, kv, g: (kv, 0, 0)),
        ],
        out_specs=pl.BlockSpec((1, 1, 1, n_rows, HEAD_DIM), lambda b, kv, g: (b, kv, g, 0, 0)),
        out_shape=jax.ShapeDtypeStruct((batch, 2, NSA_GROUPS, n_rows, HEAD_DIM), BF16),
        compiler_params=_cparams(("parallel", "parallel", "parallel")),
        name="compress",
    )(p16, w1cat, pos2, w2)


_NT = (((1,), (1,)), ((), ()))
CMP_PER_SEL = SEL_BLOCK // CMP_STRIDE


def _cmp_select_body(q_ref, kc_ref, vc_ref, map_ref, o_ref, bias_ref, start, g, q_rows, *, n_top, w, bw):
    n_selp = map_ref.shape[0]
    q = q_ref[0, g * NSA_HPG:(g + 1) * NSA_HPG, q_rows, :].reshape(NSA_HPG * Q_BLOCK, HEAD_DIM)
    s = lax.dot_general(q, kc_ref[0, 0, g, :w, :], _NT, preferred_element_type=F32)
    s = s.reshape(NSA_HPG, Q_BLOCK, w)
    t_q = start + lax.broadcasted_iota(jnp.int32, (1, Q_BLOCK, w), 1)
    n_id = lax.broadcasted_iota(jnp.int32, (1, Q_BLOCK, w), 2)
    valid = n_id * CMP_STRIDE + (CMP_BLOCK - 1) <= t_q
    s = jnp.where(valid, s, NEG_INF)
    e = jnp.exp2(s - jnp.max(s, axis=-1, keepdims=True))
    den = jnp.sum(e, axis=-1, keepdims=True)
    sees_any = start + lax.broadcasted_iota(jnp.int32, (1, Q_BLOCK, 1), 1) >= CMP_BLOCK - 1
    p = e * jnp.where(sees_any, 1.0 / den, 0.0)
    o = jnp.dot(p.reshape(NSA_HPG * Q_BLOCK, w).astype(BF16), vc_ref[0, 0, g, :w, :],
                preferred_element_type=F32)
    for h in range(NSA_HPG):
        hd = g * NSA_HPG + h
        o_ref[0, q_rows, hd * HEAD_DIM:(hd + 1) * HEAD_DIM] = o[h * Q_BLOCK:(h + 1) * Q_BLOCK].astype(BF16)

    psum = p[0] + p[1] + p[2] + p[3]
    p_hi = psum.astype(BF16)
    p_lo = (psum - p_hi.astype(F32)).astype(BF16)
    smap_t = map_ref[:bw, :w]
    imp = (lax.dot_general(smap_t, p_hi, _NT, preferred_element_type=F32)
           + lax.dot_general(smap_t, p_lo, _NT, preferred_element_type=F32))
    blk = lax.broadcasted_iota(jnp.int32, (bw, Q_BLOCK), 0)
    cur = (start + lax.broadcasted_iota(jnp.int32, (bw, Q_BLOCK), 1)) // SEL_BLOCK
    forced = (blk == 0) | (blk == cur) | (blk == cur - 1)
    blk_f = blk.astype(F32)
    work = jnp.where(forced, -jnp.inf, jnp.where(blk > cur, NEG_INF, imp))
    for _ in range(n_top - 3):
        top = jnp.max(work, axis=0, keepdims=True)
        first = jnp.min(jnp.where(work == top, blk_f, 1e9), axis=0, keepdims=True)
        work = jnp.where(blk_f == first, -jnp.inf, work)
    keep = (work == -jnp.inf) & (blk <= cur)
    bias_t = jnp.where(keep, 0.0, NEG_INF)
    if bw < n_selp:
        bias_t = jnp.concatenate([bias_t, jnp.full((n_selp - bw, Q_BLOCK), NEG_INF, F32)], axis=0)
    bias_ref[0, g, q_rows, :] = bias_t.T.astype(BF16)


def _cmp_select_kernel(q_ref, kc_ref, vc_ref, map_ref, o_ref, bias_ref, *, n_top, chunk, tq):
    start = pl.program_id(1) * tq
    variant = ((start + tq - 1) // SEL_BLOCK) // (chunk // CMP_PER_SEL)

    def all_chains(w):
        for qc in range(tq // Q_BLOCK):
            for g in range(NSA_GROUPS):
                _cmp_select_body(q_ref, kc_ref, vc_ref, map_ref, o_ref, bias_ref, start + qc * Q_BLOCK, g,
                                 slice(qc * Q_BLOCK, (qc + 1) * Q_BLOCK),
                                 n_top=n_top, w=w, bw=min(w // CMP_PER_SEL, map_ref.shape[0]))

    for v in range(map_ref.shape[1] // chunk):
        pl.when(variant == v)(functools.partial(all_chains, (v + 1) * chunk))


def _cmp_select(p, kvc, sel_map, batch, t_len, tq):
    n_rows = t_len // CMP_STRIDE
    n_selp = sel_map.shape[0]
    n_top = min(N_SELECT, t_len // SEL_BLOCK)
    assert n_top >= 3
    chunk = min(256, n_rows)
    assert n_rows % chunk == 0 and (chunk // CMP_PER_SEL) % 8 == 0
    kern = functools.partial(_cmp_select_kernel, n_top=n_top, chunk=chunk, tq=tq)
    return pl.pallas_call(
        kern,
        grid=(batch, t_len // tq),
        in_specs=[
            pl.BlockSpec((1, NSA_HEADS, tq, HEAD_DIM), lambda b, c: (b, 0, c, 0)),
            pl.BlockSpec((1, 1, NSA_GROUPS, n_rows, HEAD_DIM), lambda b, c: (b, 0, 0, 0, 0)),
            pl.BlockSpec((1, 1, NSA_GROUPS, n_rows, HEAD_DIM), lambda b, c: (b, 1, 0, 0, 0)),
            pl.BlockSpec((n_selp, n_rows), lambda b, c: (0, 0)),
        ],
        out_specs=[
            pl.BlockSpec((1, tq, NSA_WIDTH), lambda b, c: (b, c, 0)),
            pl.BlockSpec((1, NSA_GROUPS, tq, n_selp), lambda b, c: (b, 0, c, 0)),
        ],
        out_shape=[
            jax.ShapeDtypeStruct((batch, t_len, NSA_WIDTH), BF16),
            jax.ShapeDtypeStruct((batch, NSA_GROUPS, t_len, n_selp), BF16),
        ],
        compiler_params=_cparams(("parallel", "parallel")),
        name="cmp_select",
    )(p, kvc, kvc, sel_map)


def _lane_tile(x, n):
    return x if n == 1 else jnp.concatenate([x] * n, axis=1)


def _flash_update(s, v, rows, m_ref, l_ref, acc_ref):
    n_keys = s.shape[1]
    dv = acc_ref.shape[-1]
    m_prev = m_ref[rows, :]
    m_new = jnp.maximum(m_prev, jnp.max(s, axis=-1, keepdims=True))
    alpha = jnp.exp2(m_prev - m_new)
    p = jnp.exp2(s - _lane_tile(m_new, n_keys // LANE))
    if l_ref is not None:
        psum = p[:, :LANE]
        for cc in range(1, n_keys // LANE):
            psum = psum + p[:, cc * LANE:(cc + 1) * LANE]
        l_ref[rows, :] = alpha * l_ref[rows, :] + psum
    acc_ref[rows, :] = (_lane_tile(alpha, dv // LANE) * acc_ref[rows, :]
                        + jnp.dot(p.astype(BF16), v, preferred_element_type=F32))
    m_ref[rows, :] = m_new


def _causal_triangle(rb):
    return lax.broadcasted_iota(jnp.int32, (rb, rb), 1) <= lax.broadcasted_iota(jnp.int32, (rb, rb), 0)


def _sel_attn_kernel(q_ref, bias_ref, k_ref, v_ref, hot_ref, o_ref, qa_ref, m_ref, acc_ref,
                     *, tq, tk, rb, rl, n_halves):
    start = pl.program_id(2) * tq
    n_rows = NSA_HPG * tq
    keys_per_half = LANE * SEL_BLOCK
    for hf in range(n_halves):
        for h in range(NSA_HPG):
            qa_ref[hf, h * tq:(h + 1) * tq, :LANE] = q_ref[0, h]
            qa_ref[hf, h * tq:(h + 1) * tq, LANE:] = bias_ref[0, 0, :, hf * LANE:(hf + 1) * LANE]
    m_ref[...] = jnp.full(m_ref.shape, NEG_INF, F32)
    acc_ref[...] = jnp.zeros(acc_ref.shape, F32)

    def k_aug(k0, n):
        return jnp.concatenate([k_ref[0, 0, pl.ds(k0, n), :], hot_ref[pl.ds(k0, n), :]], axis=1)

    def v_aug(k0, n):
        return jnp.concatenate([v_ref[0, 0, pl.ds(k0, n), :], jnp.ones((n, LANE), BF16)], axis=1)

    def body(j, carry):
        k0 = pl.multiple_of(j * tk, tk)
        ka, v, hf = k_aug(k0, tk), v_aug(k0, tk), k0 // keys_per_half
        for r in range(n_rows // rl):
            rows = pl.ds(r * rl, rl)
            s = lax.dot_general(qa_ref[hf, rows, :], ka, _NT, preferred_element_type=F32)
            _flash_update(s, v, rows, m_ref, None, acc_ref)
        return carry

    lax.fori_loop(0, start // tk, body, 0)

    tri = _causal_triangle(rb)
    for d in range(tq // rb):
        k0 = pl.multiple_of(start + d * rb, rb)
        ka, v, hf = k_aug(k0, rb), v_aug(k0, rb), k0 // keys_per_half
        for r in range(n_rows // rb):
            q_off = (r * rb) % tq
            if d * rb > q_off:
                continue
            rows = pl.ds(r * rb, rb)
            s = lax.dot_general(qa_ref[hf, rows, :], ka, _NT, preferred_element_type=F32)
            if d * rb == q_off:
                s = jnp.where(tri, s, NEG_INF)
            _flash_update(s, v, rows, m_ref, None, acc_ref)

    for h in range(NSA_HPG):
        hr = slice(h * tq, (h + 1) * tq)
        o_ref[0, :, h * HEAD_DIM:(h + 1) * HEAD_DIM] = (acc_ref[hr, :HEAD_DIM] / acc_ref[hr, HEAD_DIM:]).astype(BF16)


def _resident(block_shape, index_map):
    return pl.BlockSpec(block_shape, index_map, pipeline_mode=pl.Buffered(1))


def _sel_attn(p, bias, hot, batch, t_len, tq, tk, rb, rl):
    assert tq % tk == 0 and tk % rb == 0 and (LANE * SEL_BLOCK) % tk == 0
    n_selp = bias.shape[-1]
    n_halves = n_selp // LANE
    n_rows = NSA_HPG * tq
    assert n_rows % rl == 0
    kern = functools.partial(_sel_attn_kernel, tq=tq, tk=tk, rb=rb, rl=rl, n_halves=n_halves)
    return pl.pallas_call(
        kern,
        grid=(batch, NSA_GROUPS, t_len // tq),
        in_specs=[
            pl.BlockSpec((1, NSA_HPG, tq, HEAD_DIM), lambda b, g, c: (b, g, c, 0)),
            pl.BlockSpec((1, 1, tq, n_selp), lambda b, g, c: (b, g, c, 0)),
            _resident((1, 1, t_len, HEAD_DIM), lambda b, g, c: (b, SLOT_KS + g, 0, 0)),
            _resident((1, 1, t_len, HEAD_DIM), lambda b, g, c: (b, SLOT_VS + g, 0, 0)),
            _resident((t_len, LANE), lambda b, g, c: (0, 0)),
        ],
        out_specs=pl.BlockSpec((1, tq, NSA_HPG * HEAD_DIM), lambda b, g, c: (b, c, g)),
        out_shape=jax.ShapeDtypeStruct((batch, t_len, NSA_WIDTH), BF16),
        scratch_shapes=[
            pltpu.VMEM((n_halves, n_rows, 2 * LANE), BF16),
            pltpu.VMEM((n_rows, LANE), F32),
            pltpu.VMEM((n_rows, 2 * HEAD_DIM), F32),
        ],
        compiler_params=_cparams(("parallel", "parallel", "arbitrary")),
        name="sel_attn",
    )(p, bias, p, p, hot)


def _win_attn_kernel(q_ref, k_ref, v_ref, o_ref, *, tq, span):
    start = pl.program_id(1) * tq
    rows = NSA_HPG * Q_BLOCK
    for qc in range(tq // Q_BLOCK):
        q_rows = slice(qc * Q_BLOCK, (qc + 1) * Q_BLOCK)
        q0 = start + qc * Q_BLOCK
        k0 = pl.multiple_of(jnp.maximum(q0 + Q_BLOCK - span, 0), Q_BLOCK)
        t_q = q0 + lax.broadcasted_iota(jnp.int32, (1, Q_BLOCK, span), 1)
        k_pos = k0 + lax.broadcasted_iota(jnp.int32, (1, Q_BLOCK, span), 2)
        valid = (k_pos <= t_q) & (k_pos > t_q - WINDOW)
        for g in range(NSA_GROUPS):
            q = q_ref[0, g * NSA_HPG:(g + 1) * NSA_HPG, q_rows, :].reshape(rows, HEAD_DIM)
            s = lax.dot_general(q, k_ref[0, g, pl.ds(k0, span), :], _NT, preferred_element_type=F32)
            s = jnp.where(valid, s.reshape(NSA_HPG, Q_BLOCK, span), NEG_INF)
            e = jnp.exp2(s - jnp.max(s, axis=-1, keepdims=True))
            v = jnp.concatenate([v_ref[0, g, pl.ds(k0, span), :], jnp.ones((span, LANE), BF16)], axis=1)
            o = jnp.dot(e.reshape(rows, span).astype(BF16), v, preferred_element_type=F32)
            o = o[:, :HEAD_DIM] / o[:, HEAD_DIM:]
            for h in range(NSA_HPG):
                hd = g * NSA_HPG + h
                o_ref[0, q_rows, hd * HEAD_DIM:(hd + 1) * HEAD_DIM] = o[h * Q_BLOCK:(h + 1) * Q_BLOCK].astype(BF16)


def _win_attn(p, batch, t_len, tq):
    span = min(WINDOW + Q_BLOCK, t_len)
    kern = functools.partial(_win_attn_kernel, tq=tq, span=span)
    return pl.pallas_call(
        kern,
        grid=(batch, t_len // tq),
        in_specs=[
            pl.BlockSpec((1, NSA_HEADS, tq, HEAD_DIM), lambda b, c: (b, 0, c, 0)),
            _resident((1, NSA_GROUPS, t_len, HEAD_DIM), lambda b, c: (b, SLOT_KW // NSA_GROUPS, 0, 0)),
            _resident((1, NSA_GROUPS, t_len, HEAD_DIM), lambda b, c: (b, SLOT_VW // NSA_GROUPS, 0, 0)),
        ],
        out_specs=pl.BlockSpec((1, tq, NSA_WIDTH), lambda b, c: (b, c, 0)),
        out_shape=jax.ShapeDtypeStruct((batch, t_len, NSA_WIDTH), BF16),
        compiler_params=_cparams(("parallel", "parallel")),
        name="win_attn",
    )(p, p, p)


def _diff_attn_kernel(q1_ref, q2_ref, k1_ref, k2_ref, v_ref, lam_ref, sw_ref, o_ref,
                      m_ref, l_ref, acc_ref, *, tq, tk, rb, rl):
    start = pl.program_id(2) * tq
    m_ref[...] = jnp.full(m_ref.shape, NEG_INF, F32)
    l_ref[...] = jnp.zeros(l_ref.shape, F32)
    acc_ref[...] = jnp.zeros(acc_ref.shape, F32)
    q_refs = (q1_ref, q2_ref)
    k_refs = (k1_ref, k2_ref)

    def v_tile(k0, n):
        return jnp.concatenate([v_ref[0, 0, pl.ds(k0, n), :], v_ref[0, 1, pl.ds(k0, n), :]], axis=1)

    def body(j, carry):
        k0 = pl.multiple_of(j * tk, tk)
        v = v_tile(k0, tk)
        for mp in range(2):
            kt = k_refs[mp][0, 0, pl.ds(k0, tk), :]
            for r in range(tq // rl):
                rows = pl.ds(r * rl, rl)
                s = lax.dot_general(q_refs[mp][0, 0, rows, :], kt, _NT, preferred_element_type=F32)
                _flash_update(s, v, rows, m_ref.at[mp], l_ref.at[mp], acc_ref.at[mp])
        return carry

    lax.fori_loop(0, start // tk, body, 0)

    tri = _causal_triangle(rb)
    for d in range(tq // rb):
        k0 = pl.multiple_of(start + d * rb, rb)
        v = v_tile(k0, rb)
        for mp in range(2):
            kt = k_refs[mp][0, 0, pl.ds(k0, rb), :]
            for r in range(d, tq // rb):
                rows = pl.ds(r * rb, rb)
                s = lax.dot_general(q_refs[mp][0, 0, rows, :], kt, _NT, preferred_element_type=F32)
                if r == d:
                    s = jnp.where(tri, s, NEG_INF)
                _flash_update(s, v, rows, m_ref.at[mp], l_ref.at[mp], acc_ref.at[mp])

    lv = lam_ref[...]
    lam = (jnp.exp(jnp.sum(lv[0:1] * lv[1:2], axis=-1, keepdims=True))
           - jnp.exp(jnp.sum(lv[2:3] * lv[3:4], axis=-1, keepdims=True)) + LAMBDA_INIT)
    l1 = jnp.sum(l_ref[0], axis=-1, keepdims=True)
    l2 = jnp.sum(l_ref[1], axis=-1, keepdims=True)
    o = acc_ref[0] / l1 - lam * (acc_ref[1] / l2)
    y = o * lax.rsqrt(jnp.mean(o * o, axis=-1, keepdims=True) + SUBLN_EPS)
    o_ref[0] = (y * sw_ref[...] * (1.0 - LAMBDA_INIT)).astype(BF16)


def _diff_attn(p, lam4, subln_w, batch, t_len, tq, tk, rb, rl):
    assert tq % tk == 0 and tk % rb == 0 and tq % rl == 0
    kern = functools.partial(_diff_attn_kernel, tq=tq, tk=tk, rb=rb, rl=rl)
    return pl.pallas_call(
        kern,
        grid=(batch, DIFF_HEADS, t_len // tq),
        in_specs=[
            pl.BlockSpec((1, 1, tq, DIFF_QK), lambda b, h, i: (b, SLOT_QB + h, i, 0)),
            pl.BlockSpec((1, 1, tq, DIFF_QK), lambda b, h, i: (b, SLOT_QB + DIFF_HEADS + h, i, 0)),
            _resident((1, 1, t_len, DIFF_QK), lambda b, h, i: (b, SLOT_KB + h, 0, 0)),
            _resident((1, 1, t_len, DIFF_QK), lambda b, h, i: (b, SLOT_KB + DIFF_HEADS + h, 0, 0)),
            _resident((1, 2, t_len, LANE), lambda b, h, i: (b, SLOT_VB // 2 + h, 0, 0)),
            pl.BlockSpec((SUBLANE, DIFF_QK), lambda b, h, i: (0, 0)),
            pl.BlockSpec((1, DIFF_V), lambda b, h, i: (0, 0)),
        ],
        out_specs=pl.BlockSpec((1, tq, DIFF_V), lambda b, h, i: (b, i, h)),
        out_shape=jax.ShapeDtypeStruct((batch, t_len, DIFF_WIDTH), BF16),
        scratch_shapes=[
            pltpu.VMEM((2, tq, LANE), F32),
            pltpu.VMEM((2, tq, LANE), F32),
            pltpu.VMEM((2, tq, DIFF_V), F32),
        ],
        compiler_params=_cparams(("parallel", "parallel", "arbitrary")),
        name="diff_attn",
    )(p, p, p, p, p, lam4, subln_w)


OUT_K_SLOTS = 2


def _out_proj_kernel(x_ref, oc_ref, os_ref, ow_ref, g_ref, za_ref, ob_ref, zb_ref, w_ref, fw_ref, o_ref):
    gates = jax.nn.sigmoid(g_ref[0])
    tm = gates.shape[0]

    def nsa_piece(hd):
        cols = slice(hd * HEAD_DIM, (hd + 1) * HEAD_DIM)
        g0 = jnp.broadcast_to(gates[:, 3 * hd:3 * hd + 1], (tm, HEAD_DIM))
        g1 = jnp.broadcast_to(gates[:, 3 * hd + 1:3 * hd + 2], (tm, HEAD_DIM))
        g2 = jnp.broadcast_to(gates[:, 3 * hd + 2:3 * hd + 3], (tm, HEAD_DIM))
        o = (g0 * oc_ref[0, :, cols].astype(F32) + g1 * os_ref[0, :, cols].astype(F32)
             + g2 * ow_ref[0, :, cols].astype(F32))
        z = za_ref[0, hd].astype(F32)
        return (o * (z * jax.nn.sigmoid(z))).astype(BF16)

    def diff_piece(sl):
        z = zb_ref[0, sl].astype(F32)
        return (ob_ref[0, :, sl * LANE:(sl + 1) * LANE].astype(F32) * (z * jax.nn.sigmoid(z))).astype(BF16)

    pieces = [functools.partial(nsa_piece, hd) for hd in range(NSA_HEADS)]
    pieces += [functools.partial(diff_piece, sl) for sl in range(DIFF_WIDTH // LANE)]
    y = x_ref[0]
    for kc in range(0, len(pieces), OUT_K_SLOTS):
        chunk = jnp.concatenate([f() for f in pieces[kc:kc + OUT_K_SLOTS]], axis=1)
        y = y + jnp.dot(chunk, w_ref[kc * LANE:(kc + OUT_K_SLOTS) * LANE, :], preferred_element_type=F32)
    y = y * lax.rsqrt(jnp.mean(y * y, axis=-1, keepdims=True) + NORM_EPS)
    o_ref[0] = y * fw_ref[...]


def _out_proj(x, o_cmp, o_sel, o_win, gl, p, o_b, w_out, final_w, batch, t_len, tm):
    row = lambda b, i: (b, i, 0)
    return pl.pallas_call(
        _out_proj_kernel,
        grid=(batch, t_len // tm),
        in_specs=[
            pl.BlockSpec((1, tm, D_MODEL), row),
            pl.BlockSpec((1, tm, NSA_WIDTH), row),
            pl.BlockSpec((1, tm, NSA_WIDTH), row),
            pl.BlockSpec((1, tm, NSA_WIDTH), row),
            pl.BlockSpec((1, tm, LANE), row),
            pl.BlockSpec((1, NSA_HEADS, tm, LANE), lambda b, i: (b, SLOT_ZA // NSA_HEADS, i, 0)),
            pl.BlockSpec((1, tm, DIFF_WIDTH), row),
            pl.BlockSpec((1, DIFF_WIDTH // LANE, tm, LANE), lambda b, i: (b, SLOT_ZB // (DIFF_WIDTH // LANE), i, 0)),
            _resident((MIX_WIDTH, D_MODEL), lambda b, i: (0, 0)),
            pl.BlockSpec((1, D_MODEL), lambda b, i: (0, 0)),
        ],
        out_specs=pl.BlockSpec((1, tm, D_MODEL), row),
        out_shape=jax.ShapeDtypeStruct((batch, t_len, D_MODEL), F32),
        compiler_params=_cparams(("parallel", "parallel")),
        name="out_proj",
    )(x, o_cmp, o_sel, o_win, gl, p, o_b, p, w_out, final_w)


def _reorder_w_in(w):
    def section(off, n_slots):
        return w[:, off:off + n_slots * LANE]
    cols = [section(_O_QA, 8), section(_O_QB, 8), section(_O_KB, 8), section(_O_KC, 2), section(_O_KS, 2),
            section(_O_KW, 2), section(_O_VC, 2), section(_O_ZA, 8), section(_O_ZB, 8), section(_O_VB, 8),
            section(_O_VS, 2), section(_O_VW, 2)]
    w_r = jnp.concatenate(cols, axis=1).astype(BF16)
    assert w_r.shape[1] == N_SLOTS * LANE
    w_g = jnp.pad(w[:, _O_G:_O_G + NSA_HEADS * NSA_BRANCHES],
                  ((0, 0), (0, LANE - NSA_HEADS * NSA_BRANCHES))).astype(BF16)
    return w_r, w_g


def _sel_map(t_len, n_selp):
    n_rows = t_len // CMP_STRIDE
    c0 = np.arange(n_rows)[:, None] * CMP_STRIDE
    s0 = np.arange(n_selp)[None, :] * SEL_BLOCK
    ov = np.minimum(c0 + CMP_BLOCK, s0 + SEL_BLOCK) - np.maximum(c0, s0)
    m = np.maximum(ov, 0) / CMP_BLOCK
    m[n_rows - 1:] = 0.0
    m[:, t_len // SEL_BLOCK:] = 0.0
    return jnp.asarray(m.T, dtype=BF16)


def _block_one_hot(t_len):
    blk = (np.arange(t_len) // SEL_BLOCK) % LANE
    return jnp.asarray(blk[:, None] == np.arange(LANE)[None, :], dtype=BF16)


def _rope_tables(t_len):
    inv = ROPE_THETA ** (-np.arange(0, HEAD_DIM, 2, dtype=np.float64) / HEAD_DIM)
    ang = np.arange(t_len, dtype=np.float64)[:, None] * inv[None, :]
    cos, sin = np.cos(ang), np.sin(ang)
    return (jnp.asarray(np.concatenate([cos, cos], axis=1), dtype=F32),
            jnp.asarray(np.concatenate([-sin, sin], axis=1), dtype=F32))


class _Tiles(NamedTuple):
    in_tm: int
    out_tm: int
    cmp_tq: int
    win_tq: int
    attn_tq: int
    diag_rb: int
    sel_tk: int
    sel_rl: int
    diff_tk: int
    diff_rl: int


def _tiles(t_len):
    return _Tiles(in_tm=min(1024, t_len), out_tm=min(512, t_len), cmp_tq=min(512, t_len), win_tq=min(2048, t_len),
                  attn_tq=min(2048, t_len), diag_rb=256, sel_tk=512, sel_rl=512, diff_tk=1024, diff_rl=128)


def kernel(x, norm_w, w_in, cmp_pos, cmp_k_w1, cmp_k_w2, cmp_v_w1, cmp_v_w2, lam_q1, lam_k1, lam_q2, lam_k2,
           subln_w, w_out, final_norm_w):
    batch, t_len, _ = x.shape
    n_rows = t_len // CMP_STRIDE
    n_selp = -(-(t_len // SEL_BLOCK) // LANE) * LANE

    tiles = _tiles(t_len)
    cos_f, sin_f = _rope_tables(t_len)
    w_r, w_g = _reorder_w_in(w_in[0])
    p, gl, pc = _in_proj(x.reshape(batch * t_len, D_MODEL), norm_w[0][None, :], w_r, w_g, cos_f, sin_f,
                         batch, t_len, tiles.in_tm)

    half = CMP_STRIDE * HEAD_DIM
    w1cat = jnp.stack([jnp.concatenate([w[:half], w[half:]], axis=1) for w in (cmp_k_w1[0], cmp_v_w1[0])]).astype(BF16)
    w2 = jnp.stack([cmp_k_w2[0], cmp_v_w2[0]]).astype(BF16)
    pos2 = jnp.broadcast_to(cmp_pos[0].reshape(2, 1, half), (2, SUBLANE, half)).astype(BF16)
    kvc = _compress(pc, w1cat, pos2, w2, batch, n_rows)

    o_cmp, bias = _cmp_select(p, kvc, _sel_map(t_len, n_selp), batch, t_len, tq=tiles.cmp_tq)
    o_sel = _sel_attn(p, bias, _block_one_hot(t_len), batch, t_len,
                      tq=tiles.attn_tq, tk=tiles.sel_tk, rb=tiles.diag_rb, rl=tiles.sel_rl)
    o_win = _win_attn(p, batch, t_len, tq=tiles.win_tq)

    lam4 = jnp.pad(jnp.stack([lam_q1[0], lam_k1[0], lam_q2[0], lam_k2[0]]), ((0, SUBLANE - 4), (0, 0)))
    o_b = _diff_attn(p, lam4, subln_w[0][None, :], batch, t_len,
                     tq=tiles.attn_tq, tk=tiles.diff_tk, rb=tiles.diag_rb, rl=tiles.diff_rl)

    return _out_proj(x, o_cmp, o_sel, o_win, gl, p, o_b, w_out[0].astype(BF16), final_norm_w[None, :],
                     batch, t_len, tm=tiles.out_tm)
```

```python
import functools
import math
from typing import NamedTuple

import numpy as np
import jax
import jax.numpy as jnp
from jax import lax
from jax.experimental import pallas as pl
from jax.experimental.pallas import tpu as pltpu

F32 = jnp.float32
BF16 = jnp.bfloat16

D_MODEL = 2048
HEAD_DIM = 128
NSA_HEADS = 8
NSA_GROUPS = 2
NSA_HPG = NSA_HEADS // NSA_GROUPS
NSA_WIDTH = NSA_HEADS * HEAD_DIM
NSA_KV = NSA_GROUPS * HEAD_DIM
NSA_BRANCHES = 3
CMP_BLOCK = 32
CMP_STRIDE = 16
CMP_HIDDEN = 512
SEL_BLOCK = 64
N_SELECT = 16
WINDOW = 512
DIFF_HEADS = 4
DIFF_QK = 128
DIFF_V = 256
DIFF_WIDTH = DIFF_HEADS * DIFF_V
MIX_WIDTH = NSA_WIDTH + DIFF_WIDTH
Q_BLOCK = 128
ROPE_THETA = 10000.0
NORM_EPS = 1e-6
SUBLN_EPS = 1e-5
NEG_INF = -1e30
LAMBDA_INIT = 0.8 - 0.6 * math.exp(-0.3 * 0)

LANE = 128
SUBLANE = 8
VMEM_LIMIT = 56 * 1024 * 1024

SLOT_QA = 0
SLOT_QB = 8
SLOT_KB = 16
SLOT_KC = 24
SLOT_KS = 26
SLOT_KW = 28
N_ROPE_SLOTS = 30
SLOT_VC = 30
SLOT_ZA = 32
SLOT_ZB = 40
SLOT_VB = 48
SLOT_VS = 56
SLOT_VW = 58
N_SLOTS = 60
SLOTS_PER_STEP = 12
N_QUERY_SLOTS = 16
QK_EXP2_SCALE = (HEAD_DIM ** -0.5) * math.log2(math.e)

_OFF = np.concatenate([[0], np.cumsum([NSA_WIDTH, NSA_KV, NSA_KV, NSA_KV, NSA_KV, NSA_KV, NSA_KV,
                                       NSA_HEADS * NSA_BRANCHES, NSA_WIDTH, 1024, 1024, DIFF_WIDTH, DIFF_WIDTH])])
(_O_QA, _O_KC, _O_VC, _O_KS, _O_VS, _O_KW, _O_VW, _O_G, _O_ZA, _O_QB, _O_KB, _O_VB, _O_ZB, _O_END) = [int(v) for v in _OFF]


def _cparams(sem):
    return pltpu.CompilerParams(dimension_semantics=sem, vmem_limit_bytes=VMEM_LIMIT)


def _in_proj_kernel(x_ref, nw_ref, w_ref, wg_ref, cos_ref, sin_ref, p_ref, g_ref, c_ref, h_ref, stage_ref):
    j = pl.program_id(1)

    @pl.when(j == 0)
    def _():
        x = x_ref[...]
        y = x * lax.rsqrt(jnp.mean(x * x, axis=-1, keepdims=True) + NORM_EPS)
        h = (y * nw_ref[...]).astype(BF16)
        h_ref[...] = h
        g_ref[0] = jnp.dot(h, wg_ref[...], preferred_element_type=F32)

    acc = jnp.dot(h_ref[...], w_ref[...], preferred_element_type=F32)

    cos_t = cos_ref[...]
    sin_t = sin_ref[...]
    for s in range(SLOTS_PER_STEP):
        slot = j * SLOTS_PER_STEP + s
        cos = jnp.where(slot < N_ROPE_SLOTS, cos_t, 1.0)
        sin = jnp.where(slot < N_ROPE_SLOTS, sin_t, 0.0)
        a = acc[:, s * LANE:(s + 1) * LANE]
        r = a * cos + pltpu.roll(a, HEAD_DIM // 2, 1) * sin
        f = jnp.where(slot < N_QUERY_SLOTS, QK_EXP2_SCALE, 1.0)
        p_ref[0, s] = (r * f).astype(BF16)

    @pl.when(j == SLOT_KC // SLOTS_PER_STEP)
    def _():
        n_out = stage_ref.shape[0] // CMP_STRIDE
        for n, slot in enumerate((SLOT_KC, SLOT_KC + 1, SLOT_VC, SLOT_VC + 1)):
            stage_ref[...] = p_ref[0, slot % SLOTS_PER_STEP].astype(F32)
            for l in range(CMP_STRIDE):
                c_ref[0, n, :, l * LANE:(l + 1) * LANE] = stage_ref[pl.ds(l, n_out, stride=CMP_STRIDE), :].astype(BF16)


def _in_proj(x2, norm_w, w_r, w_g, cos_f, sin_f, batch, t_len, tm):
    m = x2.shape[0]
    tiles_per_seq = t_len // tm
    assert SLOT_KC // SLOTS_PER_STEP == (SLOT_VC + 1) // SLOTS_PER_STEP
    n_steps = N_SLOTS // SLOTS_PER_STEP
    tn = SLOTS_PER_STEP * LANE
    return pl.pallas_call(
        _in_proj_kernel,
        grid=(m // tm, n_steps),
        in_specs=[
            pl.BlockSpec((tm, D_MODEL), lambda i, j: (i, 0)),
            pl.BlockSpec((1, D_MODEL), lambda i, j: (0, 0)),
            pl.BlockSpec((D_MODEL, tn), lambda i, j: (0, j)),
            pl.BlockSpec((D_MODEL, LANE), lambda i, j: (0, 0)),
            pl.BlockSpec((tm, LANE), lambda i, j: (i % tiles_per_seq, 0)),
            pl.BlockSpec((tm, LANE), lambda i, j: (i % tiles_per_seq, 0)),
        ],
        out_specs=[
            pl.BlockSpec((1, SLOTS_PER_STEP, tm, LANE), lambda i, j: (i // tiles_per_seq, j, i % tiles_per_seq, 0)),
            pl.BlockSpec((1, tm, LANE), lambda i, j: (i // tiles_per_seq, i % tiles_per_seq, 0)),
            pl.BlockSpec((1, 2 * NSA_GROUPS, tm // CMP_STRIDE, CMP_STRIDE * HEAD_DIM),
                         lambda i, j: (i // tiles_per_seq, 0, i % tiles_per_seq, 0)),
        ],
        out_shape=[
            jax.ShapeDtypeStruct((batch, N_SLOTS, t_len, LANE), BF16),
            jax.ShapeDtypeStruct((batch, t_len, LANE), F32),
            jax.ShapeDtypeStruct((batch, 2 * NSA_GROUPS, t_len // CMP_STRIDE, CMP_STRIDE * HEAD_DIM), BF16),
        ],
        scratch_shapes=[pltpu.VMEM((tm, D_MODEL), BF16), pltpu.VMEM((tm, LANE), F32)],
        compiler_params=_cparams(("parallel", "arbitrary")),
        name="in_proj",
    )(x2, norm_w, w_r, w_g, cos_f, sin_f)


def _compress_kernel(r_ref, w1_ref, pos_ref, w2_ref, o_ref, *, n_rows):
    w1 = w1_ref[0]
    a = jnp.dot(r_ref[0, 0], w1, preferred_element_type=F32)
    c = (jnp.dot(pos_ref[0], w1[:, :CMP_HIDDEN], preferred_element_type=F32)
         + jnp.dot(pos_ref[1], w1[:, CMP_HIDDEN:], preferred_element_type=F32))[0:1]
    nxt = pltpu.roll(a[:, CMP_HIDDEN:], n_rows - 1, 0)
    hid = a[:, :CMP_HIDDEN] + nxt + c
    hid = hid * jax.nn.sigmoid(hid)
    out = jnp.dot(hid.astype(BF16), w2_ref[0], preferred_element_type=F32)
    row = lax.broadcasted_iota(jnp.int32, out.shape, 0)
    o_ref[0, 0, 0] = jnp.where(row < n_rows - 1, out, 0.0).astype(BF16)


def _compress(p16, w1cat, pos2, w2, batch, n_rows):
    kern = functools.partial(_compress_kernel, n_rows=n_rows)

    return pl.pallas_call(
        kern,
        grid=(batch, 2, NSA_GROUPS),
        in_specs=[
            pl.BlockSpec((1, 1, n_rows, CMP_STRIDE * HEAD_DIM), lambda b, kv, g: (b, kv * NSA_GROUPS + g, 0, 0)),
            pl.BlockSpec((1, CMP_STRIDE * HEAD_DIM, 2 * CMP_HIDDEN), lambda b, kv, g: (kv, 0, 0)),
            pl.BlockSpec((2, SUBLANE, CMP_STRIDE * HEAD_DIM), lambda b, kv, g: (0, 0, 0)),
            pl.BlockSpec((1, CMP_HIDDEN, HEAD_DIM), lambda b, kv, g: (kv, 0, 0)),
        ],
        out_specs=pl.BlockSpec((1, 1, 1, n_rows, HEAD_DIM), lambda b, kv, g: (b, kv, g, 0, 0)),
        out_shape=jax.ShapeDtypeStruct((batch, 2, NSA_GROUPS, n_rows, HEAD_DIM), BF16),
        compiler_params=_cparams(("parallel", "parallel", "parallel")),
        name="compress",
    )(p16, w1cat, pos2, w2)


_NT = (((1,), (1,)), ((), ()))
CMP_PER_SEL = SEL_BLOCK // CMP_STRIDE


def _cmp_select_body(q_ref, kc_ref, vc_ref, map_ref, o_ref, bias_ref, start, g, q_rows, *, n_top, w, bw):
    n_selp = map_ref.shape[0]
    q = q_ref[0, g * NSA_HPG:(g + 1) * NSA_HPG, q_rows, :].reshape(NSA_HPG * Q_BLOCK, HEAD_DIM)
    s = lax.dot_general(q, kc_ref[0, 0, g, :w, :], _NT, preferred_element_type=F32)
    s = s.reshape(NSA_HPG, Q_BLOCK, w)
    t_q = start + lax.broadcasted_iota(jnp.int32, (1, Q_BLOCK, w), 1)
    n_id = lax.broadcasted_iota(jnp.int32, (1, Q_BLOCK, w), 2)
    valid = n_id * CMP_STRIDE + (CMP_BLOCK - 1) <= t_q
    s = jnp.where(valid, s, NEG_INF)
    e = jnp.exp2(s - jnp.max(s, axis=-1, keepdims=True))
    den = jnp.sum(e, axis=-1, keepdims=True)
    sees_any = start + lax.broadcasted_iota(jnp.int32, (1, Q_BLOCK, 1), 1) >= CMP_BLOCK - 1
    p = e * jnp.where(sees_any, 1.0 / den, 0.0)
    o = jnp.dot(p.reshape(NSA_HPG * Q_BLOCK, w).astype(BF16), vc_ref[0, 0, g, :w, :],
                preferred_element_type=F32)
    for h in range(NSA_HPG):
        hd = g * NSA_HPG + h
        o_ref[0, q_rows, hd * HEAD_DIM:(hd + 1) * HEAD_DIM] = o[h * Q_BLOCK:(h + 1) * Q_BLOCK].astype(BF16)

    psum = p[0] + p[1] + p[2] + p[3]
    p_hi = psum.astype(BF16)
    p_lo = (psum - p_hi.astype(F32)).astype(BF16)
    smap_t = map_ref[:bw, :w]
    imp = (lax.dot_general(smap_t, p_hi, _NT, preferred_element_type=F32)
           + lax.dot_general(smap_t, p_lo, _NT, preferred_element_type=F32))
    blk = lax.broadcasted_iota(jnp.int32, (bw, Q_BLOCK), 0)
    cur = (start + lax.broadcasted_iota(jnp.int32, (bw, Q_BLOCK), 1)) // SEL_BLOCK
    forced = (blk == 0) | (blk == cur) | (blk == cur - 1)
    blk_f = blk.astype(F32)
    work = jnp.where(forced, -jnp.inf, jnp.where(blk > cur, NEG_INF, imp))
    for _ in range(n_top - 3):
        top = jnp.max(work, axis=0, keepdims=True)
        first = jnp.min(jnp.where(work == top, blk_f, 1e9), axis=0, keepdims=True)
        work = jnp.where(blk_f == first, -jnp.inf, work)
    keep = (work == -jnp.inf) & (blk <= cur)
    bias_t = jnp.where(keep, 0.0, NEG_INF)
    if bw < n_selp:
        bias_t = jnp.concatenate([bias_t, jnp.full((n_selp - bw, Q_BLOCK), NEG_INF, F32)], axis=0)
    bias_ref[0, g, q_rows, :] = bias_t.T.astype(BF16)


def _cmp_select_kernel(q_ref, kc_ref, vc_ref, map_ref, o_ref, bias_ref, *, n_top, chunk, tq):
    start = pl.program_id(1) * tq
    variant = ((start + tq - 1) // SEL_BLOCK) // (chunk // CMP_PER_SEL)

    def all_chains(w):
        for qc in range(tq // Q_BLOCK):
            for g in range(NSA_GROUPS):
                _cmp_select_body(q_ref, kc_ref, vc_ref, map_ref, o_ref, bias_ref, start + qc * Q_BLOCK, g,
                                 slice(qc * Q_BLOCK, (qc + 1) * Q_BLOCK),
                                 n_top=n_top, w=w, bw=min(w // CMP_PER_SEL, map_ref.shape[0]))

    for v in range(map_ref.shape[1] // chunk):
        pl.when(variant == v)(functools.partial(all_chains, (v + 1) * chunk))


def _cmp_select(p, kvc, sel_map, batch, t_len, tq):
    n_rows = t_len // CMP_STRIDE
    n_selp = sel_map.shape[0]
    n_top = min(N_SELECT, t_len // SEL_BLOCK)
    assert n_top >= 3
    chunk = min(256, n_rows)
    assert n_rows % chunk == 0 and (chunk // CMP_PER_SEL) % 8 == 0
    kern = functools.partial(_cmp_select_kernel, n_top=n_top, chunk=chunk, tq=tq)
    return pl.pallas_call(
        kern,
        grid=(batch, t_len // tq),
        in_specs=[
            pl.BlockSpec((1, NSA_HEADS, tq, HEAD_DIM), lambda b, c: (b, 0, c, 0)),
            pl.BlockSpec((1, 1, NSA_GROUPS, n_rows, HEAD_DIM), lambda b, c: (b, 0, 0, 0, 0)),
            pl.BlockSpec((1, 1, NSA_GROUPS, n_rows, HEAD_DIM), lambda b, c: (b, 1, 0, 0, 0)),
            pl.BlockSpec((n_selp, n_rows), lambda b, c: (0, 0)),
        ],
        out_specs=[
            pl.BlockSpec((1, tq, NSA_WIDTH), lambda b, c: (b, c, 0)),
            pl.BlockSpec((1, NSA_GROUPS, tq, n_selp), lambda b, c: (b, 0, c, 0)),
        ],
        out_shape=[
            jax.ShapeDtypeStruct((batch, t_len, NSA_WIDTH), BF16),
            jax.ShapeDtypeStruct((batch, NSA_GROUPS, t_len, n_selp), BF16),
        ],
        compiler_params=_cparams(("parallel", "parallel")),
        name="cmp_select",
    )(p, kvc, kvc, sel_map)


def _lane_tile(x, n):
    return x if n == 1 else jnp.concatenate([x] * n, axis=1)


def _flash_update(s, v, rows, m_ref, l_ref, acc_ref):
    n_keys = s.shape[1]
    dv = acc_ref.shape[-1]
    m_prev = m_ref[rows, :]
    m_new = jnp.maximum(m_prev, jnp.max(s, axis=-1, keepdims=True))
    alpha = jnp.exp2(m_prev - m_new)
    p = jnp.exp2(s - _lane_tile(m_new, n_keys // LANE))
    if l_ref is not None:
        psum = p[:, :LANE]
        for cc in range(1, n_keys // LANE):
            psum = psum + p[:, cc * LANE:(cc + 1) * LANE]
        l_ref[rows, :] = alpha * l_ref[rows, :] + psum
    acc_ref[rows, :] = (_lane_tile(alpha, dv // LANE) * acc_ref[rows, :]
                        + jnp.dot(p.astype(BF16), v, preferred_element_type=F32))
    m_ref[rows, :] = m_new


def _causal_triangle(rb):
    return lax.broadcasted_iota(jnp.int32, (rb, rb), 1) <= lax.broadcasted_iota(jnp.int32, (rb, rb), 0)


def _sel_attn_kernel(q_ref, bias_ref, k_ref, v_ref, hot_ref, o_ref, qa_ref, m_ref, acc_ref,
                     *, tq, tk, rb, rl, n_halves):
    start = pl.program_id(2) * tq
    n_rows = NSA_HPG * tq
    keys_per_half = LANE * SEL_BLOCK
    for hf in range(n_halves):
        for h in range(NSA_HPG):
            qa_ref[hf, h * tq:(h + 1) * tq, :LANE] = q_ref[0, h]
            qa_ref[hf, h * tq:(h + 1) * tq, LANE:] = bias_ref[0, 0, :, hf * LANE:(hf + 1) * LANE]
    m_ref[...] = jnp.full(m_ref.shape, NEG_INF, F32)
    acc_ref[...] = jnp.zeros(acc_ref.shape, F32)

    def k_aug(k0, n):
        return jnp.concatenate([k_ref[0, 0, pl.ds(k0, n), :], hot_ref[pl.ds(k0, n), :]], axis=1)

    def v_aug(k0, n):
        return jnp.concatenate([v_ref[0, 0, pl.ds(k0, n), :], jnp.ones((n, LANE), BF16)], axis=1)

    def body(j, carry):
        k0 = pl.multiple_of(j * tk, tk)
        ka, v, hf = k_aug(k0, tk), v_aug(k0, tk), k0 // keys_per_half
        for r in range(n_rows // rl):
            rows = pl.ds(r * rl, rl)
            s = lax.dot_general(qa_ref[hf, rows, :], ka, _NT, preferred_element_type=F32)
            _flash_update(s, v, rows, m_ref, None, acc_ref)
        return carry

    lax.fori_loop(0, start // tk, body, 0)

    tri = _causal_triangle(rb)
    for d in range(tq // rb):
        k0 = pl.multiple_of(start + d * rb, rb)
        ka, v, hf = k_aug(k0, rb), v_aug(k0, rb), k0 // keys_per_half
        for r in range(n_rows // rb):
            q_off = (r * rb) % tq
            if d * rb > q_off:
                continue
            rows = pl.ds(r * rb, rb)
            s = lax.dot_general(qa_ref[hf, rows, :], ka, _NT, preferred_element_type=F32)
            if d * rb == q_off:
                s = jnp.where(tri, s, NEG_INF)
            _flash_update(s, v, rows, m_ref, None, acc_ref)

    for h in range(NSA_HPG):
        hr = slice(h * tq, (h + 1) * tq)
        o_ref[0, :, h * HEAD_DIM:(h + 1) * HEAD_DIM] = (acc_ref[hr, :HEAD_DIM] / acc_ref[hr, HEAD_DIM:]).astype(BF16)


def _resident(block_shape, index_map):
    return pl.BlockSpec(block_shape, index_map, pipeline_mode=pl.Buffered(1))


def _sel_attn(p, bias, hot, batch, t_len, tq, tk, rb, rl):
    assert tq % tk == 0 and tk % rb == 0 and (LANE * SEL_BLOCK) % tk == 0
    n_selp = bias.shape[-1]
    n_halves = n_selp // LANE
    n_rows = NSA_HPG * tq
    assert n_rows % rl == 0
    kern = functools.partial(_sel_attn_kernel, tq=tq, tk=tk, rb=rb, rl=rl, n_halves=n_halves)
    return pl.pallas_call(
        kern,
        grid=(batch, NSA_GROUPS, t_len // tq),
        in_specs=[
            pl.BlockSpec((1, NSA_HPG, tq, HEAD_DIM), lambda b, g, c: (b, g, c, 0)),
            pl.BlockSpec((1, 1, tq, n_selp), lambda b, g, c: (b, g, c, 0)),
            _resident((1, 1, t_len, HEAD_DIM), lambda b, g, c: (b, SLOT_KS + g, 0, 0)),
            _resident((1, 1, t_len, HEAD_DIM), lambda b, g, c: (b, SLOT_VS + g, 0, 0)),
            _resident((t_len, LANE), lambda b, g, c: (0, 0)),
        ],
        out_specs=pl.BlockSpec((1, tq, NSA_HPG * HEAD_DIM), lambda b, g, c: (b, c, g)),
        out_shape=jax.ShapeDtypeStruct((batch, t_len, NSA_WIDTH), BF16),
        scratch_shapes=[
            pltpu.VMEM((n_halves, n_rows, 2 * LANE), BF16),
            pltpu.VMEM((n_rows, LANE), F32),
            pltpu.VMEM((n_rows, 2 * HEAD_DIM), F32),
        ],
        compiler_params=_cparams(("parallel", "parallel", "arbitrary")),
        name="sel_attn",
    )(p, bias, p, p, hot)


def _win_attn_kernel(q_ref, k_ref, v_ref, o_ref, *, tq, span):
    start = pl.program_id(1) * tq
    rows = NSA_HPG * Q_BLOCK
    for qc in range(tq // Q_BLOCK):
        q_rows = slice(qc * Q_BLOCK, (qc + 1) * Q_BLOCK)
        q0 = start + qc * Q_BLOCK
        k0 = pl.multiple_of(jnp.maximum(q0 + Q_BLOCK - span, 0), Q_BLOCK)
        t_q = q0 + lax.broadcasted_iota(jnp.int32, (1, Q_BLOCK, span), 1)
        k_pos = k0 + lax.broadcasted_iota(jnp.int32, (1, Q_BLOCK, span), 2)
        valid = (k_pos <= t_q) & (k_pos > t_q - WINDOW)
        for g in range(NSA_GROUPS):
            q = q_ref[0, g * NSA_HPG:(g + 1) * NSA_HPG, q_rows, :].reshape(rows, HEAD_DIM)
            s = lax.dot_general(q, k_ref[0, g, pl.ds(k0, span), :], _NT, preferred_element_type=F32)
            s = jnp.where(valid, s.reshape(NSA_HPG, Q_BLOCK, span), NEG_INF)
            e = jnp.exp2(s - jnp.max(s, axis=-1, keepdims=True))
            v = jnp.concatenate([v_ref[0, g, pl.ds(k0, span), :], jnp.ones((span, LANE), BF16)], axis=1)
            o = jnp.dot(e.reshape(rows, span).astype(BF16), v, preferred_element_type=F32)
            o = o[:, :HEAD_DIM] / o[:, HEAD_DIM:]
            for h in range(NSA_HPG):
                hd = g * NSA_HPG + h
                o_ref[0, q_rows, hd * HEAD_DIM:(hd + 1) * HEAD_DIM] = o[h * Q_BLOCK:(h + 1) * Q_BLOCK].astype(BF16)


def _win_attn(p, batch, t_len, tq):
    span = min(WINDOW + Q_BLOCK, t_len)
    kern = functools.partial(_win_attn_kernel, tq=tq, span=span)
    return pl.pallas_call(
        kern,
        grid=(batch, t_len // tq),
        in_specs=[
            pl.BlockSpec((1, NSA_HEADS, tq, HEAD_DIM), lambda b, c: (b, 0, c, 0)),
            _resident((1, NSA_GROUPS, t_len, HEAD_DIM), lambda b, c: (b, SLOT_KW // NSA_GROUPS, 0, 0)),
            _resident((1, NSA_GROUPS, t_len, HEAD_DIM), lambda b, c: (b, SLOT_VW // NSA_GROUPS, 0, 0)),
        ],
        out_specs=pl.BlockSpec((1, tq, NSA_WIDTH), lambda b, c: (b, c, 0)),
        out_shape=jax.ShapeDtypeStruct((batch, t_len, NSA_WIDTH), BF16),
        compiler_params=_cparams(("parallel", "parallel")),
        name="win_attn",
    )(p, p, p)


def _diff_attn_kernel(q1_ref, q2_ref, k1_ref, k2_ref, v_ref, lam_ref, sw_ref, o_ref,
                      m_ref, l_ref, acc_ref, *, tq, tk, rb, rl):
    start = pl.program_id(2) * tq
    m_ref[...] = jnp.full(m_ref.shape, NEG_INF, F32)
    l_ref[...] = jnp.zeros(l_ref.shape, F32)
    acc_ref[...] = jnp.zeros(acc_ref.shape, F32)
    q_refs = (q1_ref, q2_ref)
    k_refs = (k1_ref, k2_ref)

    def v_tile(k0, n):
        return jnp.concatenate([v_ref[0, 0, pl.ds(k0, n), :], v_ref[0, 1, pl.ds(k0, n), :]], axis=1)

    def body(j, carry):
        k0 = pl.multiple_of(j * tk, tk)
        v = v_tile(k0, tk)
        for mp in range(2):
            kt = k_refs[mp][0, 0, pl.ds(k0, tk), :]
            for r in range(tq // rl):
                rows = pl.ds(r * rl, rl)
                s = lax.dot_general(q_refs[mp][0, 0, rows, :], kt, _NT, preferred_element_type=F32)
                _flash_update(s, v, rows, m_ref.at[mp], l_ref.at[mp], acc_ref.at[mp])
        return carry

    lax.fori_loop(0, start // tk, body, 0)

    tri = _causal_triangle(rb)
    for d in range(tq // rb):
        k0 = pl.multiple_of(start + d * rb, rb)
        v = v_tile(k0, rb)
        for mp in range(2):
            kt = k_refs[mp][0, 0, pl.ds(k0, rb), :]
            for r in range(d, tq // rb):
                rows = pl.ds(r * rb, rb)
                s = lax.dot_general(q_refs[mp][0, 0, rows, :], kt, _NT, preferred_element_type=F32)
                if r == d:
                    s = jnp.where(tri, s, NEG_INF)
                _flash_update(s, v, rows, m_ref.at[mp], l_ref.at[mp], acc_ref.at[mp])

    lv = lam_ref[...]
    lam = (jnp.exp(jnp.sum(lv[0:1] * lv[1:2], axis=-1, keepdims=True))
           - jnp.exp(jnp.sum(lv[2:3] * lv[3:4], axis=-1, keepdims=True)) + LAMBDA_INIT)
    l1 = jnp.sum(l_ref[0], axis=-1, keepdims=True)
    l2 = jnp.sum(l_ref[1], axis=-1, keepdims=True)
    o = acc_ref[0] / l1 - lam * (acc_ref[1] / l2)
    y = o * lax.rsqrt(jnp.mean(o * o, axis=-1, keepdims=True) + SUBLN_EPS)
    o_ref[0] = (y * sw_ref[...] * (1.0 - LAMBDA_INIT)).astype(BF16)


def _diff_attn(p, lam4, subln_w, batch, t_len, tq, tk, rb, rl):
    assert tq % tk == 0 and tk % rb == 0 and tq % rl == 0
    kern = functools.partial(_diff_attn_kernel, tq=tq, tk=tk, rb=rb, rl=rl)
    return pl.pallas_call(
        kern,
        grid=(batch, DIFF_HEADS, t_len // tq),
        in_specs=[
            pl.BlockSpec((1, 1, tq, DIFF_QK), lambda b, h, i: (b, SLOT_QB + h, i, 0)),
            pl.BlockSpec((1, 1, tq, DIFF_QK), lambda b, h, i: (b, SLOT_QB + DIFF_HEADS + h, i, 0)),
            _resident((1, 1, t_len, DIFF_QK), lambda b, h, i: (b, SLOT_KB + h, 0, 0)),
            _resident((1, 1, t_len, DIFF_QK), lambda b, h, i: (b, SLOT_KB + DIFF_HEADS + h, 0, 0)),
            _resident((1, 2, t_len, LANE), lambda b, h, i: (b, SLOT_VB // 2 + h, 0, 0)),
            pl.BlockSpec((SUBLANE, DIFF_QK), lambda b, h, i: (0, 0)),
            pl.BlockSpec((1, DIFF_V), lambda b, h, i: (0, 0)),
        ],
        out_specs=pl.BlockSpec((1, tq, DIFF_V), lambda b, h, i: (b, i, h)),
        out_shape=jax.ShapeDtypeStruct((batch, t_len, DIFF_WIDTH), BF16),
        scratch_shapes=[
            pltpu.VMEM((2, tq, LANE), F32),
            pltpu.VMEM((2, tq, LANE), F32),
            pltpu.VMEM((2, tq, DIFF_V), F32),
        ],
        compiler_params=_cparams(("parallel", "parallel", "arbitrary")),
        name="diff_attn",
    )(p, p, p, p, p, lam4, subln_w)


OUT_K_SLOTS = 2


def _out_proj_kernel(x_ref, oc_ref, os_ref, ow_ref, g_ref, za_ref, ob_ref, zb_ref, w_ref, fw_ref, o_ref):
    gates = jax.nn.sigmoid(g_ref[0])
    tm = gates.shape[0]

    def nsa_piece(hd):
        cols = slice(hd * HEAD_DIM, (hd + 1) * HEAD_DIM)
        g0 = jnp.broadcast_to(gates[:, 3 * hd:3 * hd + 1], (tm, HEAD_DIM))
        g1 = jnp.broadcast_to(gates[:, 3 * hd + 1:3 * hd + 2], (tm, HEAD_DIM))
        g2 = jnp.broadcast_to(gates[:, 3 * hd + 2:3 * hd + 3], (tm, HEAD_DIM))
        o = (g0 * oc_ref[0, :, cols].astype(F32) + g1 * os_ref[0, :, cols].astype(F32)
             + g2 * ow_ref[0, :, cols].astype(F32))
        z = za_ref[0, hd].astype(F32)
        return (o * (z * jax.nn.sigmoid(z))).astype(BF16)

    def diff_piece(sl):
        z = zb_ref[0, sl].astype(F32)
        return (ob_ref[0, :, sl * LANE:(sl + 1) * LANE].astype(F32) * (z * jax.nn.sigmoid(z))).astype(BF16)

    pieces = [functools.partial(nsa_piece, hd) for hd in range(NSA_HEADS)]
    pieces += [functools.partial(diff_piece, sl) for sl in range(DIFF_WIDTH // LANE)]
    y = x_ref[0]
    for kc in range(0, len(pieces), OUT_K_SLOTS):
        chunk = jnp.concatenate([f() for f in pieces[kc:kc + OUT_K_SLOTS]], axis=1)
        y = y + jnp.dot(chunk, w_ref[kc * LANE:(kc + OUT_K_SLOTS) * LANE, :], preferred_element_type=F32)
    y = y * lax.rsqrt(jnp.mean(y * y, axis=-1, keepdims=True) + NORM_EPS)
    o_ref[0] = y * fw_ref[...]


def _out_proj(x, o_cmp, o_sel, o_win, gl, p, o_b, w_out, final_w, batch, t_len, tm):
    row = lambda b, i: (b, i, 0)
    return pl.pallas_call(
        _out_proj_kernel,
        grid=(batch, t_len // tm),
        in_specs=[
            pl.BlockSpec((1, tm, D_MODEL), row),
            pl.BlockSpec((1, tm, NSA_WIDTH), row),
            pl.BlockSpec((1, tm, NSA_WIDTH), row),
            pl.BlockSpec((1, tm, NSA_WIDTH), row),
            pl.BlockSpec((1, tm, LANE), row),
            pl.BlockSpec((1, NSA_HEADS, tm, LANE), lambda b, i: (b, SLOT_ZA // NSA_HEADS, i, 0)),
            pl.BlockSpec((1, tm, DIFF_WIDTH), row),
            pl.BlockSpec((1, DIFF_WIDTH // LANE, tm, LANE), lambda b, i: (b, SLOT_ZB // (DIFF_WIDTH // LANE), i, 0)),
            _resident((MIX_WIDTH, D_MODEL), lambda b, i: (0, 0)),
            pl.BlockSpec((1, D_MODEL), lambda b, i: (0, 0)),
        ],
        out_specs=pl.BlockSpec((1, tm, D_MODEL), row),
        out_shape=jax.ShapeDtypeStruct((batch, t_len, D_MODEL), F32),
        compiler_params=_cparams(("parallel", "parallel")),
        name="out_proj",
    )(x, o_cmp, o_sel, o_win, gl, p, o_b, p, w_out, final_w)


def _reorder_w_in(w):
    def section(off, n_slots):
        return w[:, off:off + n_slots * LANE]
    cols = [section(_O_QA, 8), section(_O_QB, 8), section(_O_KB, 8), section(_O_KC, 2), section(_O_KS, 2),
            section(_O_KW, 2), section(_O_VC, 2), section(_O_ZA, 8), section(_O_ZB, 8), section(_O_VB, 8),
            section(_O_VS, 2), section(_O_VW, 2)]
    w_r = jnp.concatenate(cols, axis=1).astype(BF16)
    assert w_r.shape[1] == N_SLOTS * LANE
    w_g = jnp.pad(w[:, _O_G:_O_G + NSA_HEADS * NSA_BRANCHES],
                  ((0, 0), (0, LANE - NSA_HEADS * NSA_BRANCHES))).astype(BF16)
    return w_r, w_g


def _sel_map(t_len, n_selp):
    n_rows = t_len // CMP_STRIDE
    c0 = np.arange(n_rows)[:, None] * CMP_STRIDE
    s0 = np.arange(n_selp)[None, :] * SEL_BLOCK
    ov = np.minimum(c0 + CMP_BLOCK, s0 + SEL_BLOCK) - np.maximum(c0, s0)
    m = np.maximum(ov, 0) / CMP_BLOCK
    m[n_rows - 1:] = 0.0
    m[:, t_len // SEL_BLOCK:] = 0.0
    return jnp.asarray(m.T, dtype=BF16)


def _block_one_hot(t_len):
    blk = (np.arange(t_len) // SEL_BLOCK) % LANE
    return jnp.asarray(blk[:, None] == np.arange(LANE)[None, :], dtype=BF16)


def _rope_tables(t_len):
    inv = ROPE_THETA ** (-np.arange(0, HEAD_DIM, 2, dtype=np.float64) / HEAD_DIM)
    ang = np.arange(t_len, dtype=np.float64)[:, None] * inv[None, :]
    cos, sin = np.cos(ang), np.sin(ang)
    return (jnp.asarray(np.concatenate([cos, cos], axis=1), dtype=F32),
            jnp.asarray(np.concatenate([-sin, sin], axis=1), dtype=F32))


class _Tiles(NamedTuple):
    in_tm: int
    out_tm: int
    cmp_tq: int
    win_tq: int
    attn_tq: int
    diag_rb: int
    sel_tk: int
    sel_rl: int
    diff_tk: int
    diff_rl: int


def _tiles(t_len):
    return _Tiles(in_tm=min(1024, t_len), out_tm=min(512, t_len), cmp_tq=min(512, t_len), win_tq=min(2048, t_len),
                  attn_tq=min(2048, t_len), diag_rb=256, sel_tk=512, sel_rl=512, diff_tk=min(2048, t_len), diff_rl=128)


def kernel(x, norm_w, w_in, cmp_pos, cmp_k_w1, cmp_k_w2, cmp_v_w1, cmp_v_w2, lam_q1, lam_k1, lam_q2, lam_k2,
           subln_w, w_out, final_norm_w):
    batch, t_len, _ = x.shape
    n_rows = t_len // CMP_STRIDE
    n_selp = -(-(t_len // SEL_BLOCK) // LANE) * LANE

    tiles = _tiles(t_len)
    cos_f, sin_f = _rope_tables(t_len)
    w_r, w_g = _reorder_w_in(w_in[0])
    p, gl, pc = _in_proj(x.reshape(batch * t_len, D_MODEL), norm_w[0][None, :], w_r, w_g, cos_f, sin_f,
                         batch, t_len, tiles.in_tm)

    half = CMP_STRIDE * HEAD_DIM
    w1cat = jnp.stack([jnp.concatenate([w[:half], w[half:]], axis=1) for w in (cmp_k_w1[0], cmp_v_w1[0])]).astype(BF16)
    w2 = jnp.stack([cmp_k_w2[0], cmp_v_w2[0]]).astype(BF16)
    pos2 = jnp.broadcast_to(cmp_pos[0].reshape(2, 1, half), (2, SUBLANE, half)).astype(BF16)
    kvc = _compress(pc, w1cat, pos2, w2, batch, n_rows)

    o_cmp, bias = _cmp_select(p, kvc, _sel_map(t_len, n_selp), batch, t_len, tq=tiles.cmp_tq)
    o_sel = _sel_attn(p, bias, _block_one_hot(t_len), batch, t_len,
                      tq=tiles.attn_tq, tk=tiles.sel_tk, rb=tiles.diag_rb, rl=tiles.sel_rl)
    o_win = _win_attn(p, batch, t_len, tq=tiles.win_tq)

    lam4 = jnp.pad(jnp.stack([lam_q1[0], lam_k1[0], lam_q2[0], lam_k2[0]]), ((0, SUBLANE - 4), (0, 0)))
    o_b = _diff_attn(p, lam4, subln_w[0][None, :], batch, t_len,
                     tq=tiles.attn_tq, tk=tiles.diff_tk, rb=tiles.diag_rb, rl=tiles.diff_rl)

    return _out_proj(x, o_cmp, o_sel, o_win, gl, p, o_b, w_out[0].astype(BF16), final_norm_w[None, :],
                     batch, t_len, tm=tiles.out_tm)
```

```python
import functools
import math
from typing import NamedTuple

import numpy as np
import jax
import jax.numpy as jnp
from jax import lax
from jax.experimental import pallas as pl
from jax.experimental.pallas import tpu as pltpu

F32 = jnp.float32
BF16 = jnp.bfloat16

D_MODEL = 2048
HEAD_DIM = 128
NSA_HEADS = 8
NSA_GROUPS = 2
NSA_HPG = NSA_HEADS // NSA_GROUPS
NSA_WIDTH = NSA_HEADS * HEAD_DIM
NSA_KV = NSA_GROUPS * HEAD_DIM
NSA_BRANCHES = 3
CMP_BLOCK = 32
CMP_STRIDE = 16
CMP_HIDDEN = 512
SEL_BLOCK = 64
N_SELECT = 16
WINDOW = 512
DIFF_HEADS = 4
DIFF_QK = 128
DIFF_V = 256
DIFF_WIDTH = DIFF_HEADS * DIFF_V
MIX_WIDTH = NSA_WIDTH + DIFF_WIDTH
Q_BLOCK = 128
ROPE_THETA = 10000.0
NORM_EPS = 1e-6
SUBLN_EPS = 1e-5
NEG_INF = -1e30
LAMBDA_INIT = 0.8 - 0.6 * math.exp(-0.3 * 0)

LANE = 128
SUBLANE = 8
VMEM_LIMIT = 56 * 1024 * 1024

SLOT_QA = 0
SLOT_QB = 8
SLOT_KB = 16
SLOT_KC = 24
SLOT_KS = 26
SLOT_KW = 28
N_ROPE_SLOTS = 30
SLOT_VC = 30
SLOT_ZA = 32
SLOT_ZB = 40
SLOT_VB = 48
SLOT_VS = 56
SLOT_VW = 58
N_SLOTS = 60
SLOTS_PER_STEP = 12
N_QUERY_SLOTS = 16
QK_EXP2_SCALE = (HEAD_DIM ** -0.5) * math.log2(math.e)

_OFF = np.concatenate([[0], np.cumsum([NSA_WIDTH, NSA_KV, NSA_KV, NSA_KV, NSA_KV, NSA_KV, NSA_KV,
                                       NSA_HEADS * NSA_BRANCHES, NSA_WIDTH, 1024, 1024, DIFF_WIDTH, DIFF_WIDTH])])
(_O_QA, _O_KC, _O_VC, _O_KS, _O_VS, _O_KW, _O_VW, _O_G, _O_ZA, _O_QB, _O_KB, _O_VB, _O_ZB, _O_END) = [int(v) for v in _OFF]


def _cparams(sem):
    return pltpu.CompilerParams(dimension_semantics=sem, vmem_limit_bytes=VMEM_LIMIT)


def _in_proj_kernel(x_ref, nw_ref, w_ref, wg_ref, cos_ref, sin_ref, p_ref, g_ref, c_ref, h_ref, stage_ref):
    j = pl.program_id(1)

    @pl.when(j == 0)
    def _():
        x = x_ref[...]
        y = x * lax.rsqrt(jnp.mean(x * x, axis=-1, keepdims=True) + NORM_EPS)
        h = (y * nw_ref[...]).astype(BF16)
        h_ref[...] = h
        g_ref[0] = jnp.dot(h, wg_ref[...], preferred_element_type=F32)

    acc = jnp.dot(h_ref[...], w_ref[...], preferred_element_type=F32)

    cos_t = cos_ref[...]
    sin_t = sin_ref[...]
    for s in range(SLOTS_PER_STEP):
        slot = j * SLOTS_PER_STEP + s
        cos = jnp.where(slot < N_ROPE_SLOTS, cos_t, 1.0)
        sin = jnp.where(slot < N_ROPE_SLOTS, sin_t, 0.0)
        a = acc[:, s * LANE:(s + 1) * LANE]
        r = a * cos + pltpu.roll(a, HEAD_DIM // 2, 1) * sin
        f = jnp.where(slot < N_QUERY_SLOTS, QK_EXP2_SCALE, 1.0)
        p_ref[0, s] = (r * f).astype(BF16)

    @pl.when(j == SLOT_KC // SLOTS_PER_STEP)
    def _():
        n_out = stage_ref.shape[0] // CMP_STRIDE
        for n, slot in enumerate((SLOT_KC, SLOT_KC + 1, SLOT_VC, SLOT_VC + 1)):
            stage_ref[...] = p_ref[0, slot % SLOTS_PER_STEP].astype(F32)
            for l in range(CMP_STRIDE):
                c_ref[0, n, :, l * LANE:(l + 1) * LANE] = stage_ref[pl.ds(l, n_out, stride=CMP_STRIDE), :].astype(BF16)


def _in_proj(x2, norm_w, w_r, w_g, cos_f, sin_f, batch, t_len, tm):
    m = x2.shape[0]
    tiles_per_seq = t_len // tm
    assert SLOT_KC // SLOTS_PER_STEP == (SLOT_VC + 1) // SLOTS_PER_STEP
    n_steps = N_SLOTS // SLOTS_PER_STEP
    tn = SLOTS_PER_STEP * LANE
    return pl.pallas_call(
        _in_proj_kernel,
        grid=(m // tm, n_steps),
        in_specs=[
            pl.BlockSpec((tm, D_MODEL), lambda i, j: (i, 0)),
            pl.BlockSpec((1, D_MODEL), lambda i, j: (0, 0)),
            pl.BlockSpec((D_MODEL, tn), lambda i, j: (0, j)),
            pl.BlockSpec((D_MODEL, LANE), lambda i, j: (0, 0)),
            pl.BlockSpec((tm, LANE), lambda i, j: (i % tiles_per_seq, 0)),
            pl.BlockSpec((tm, LANE), lambda i, j: (i % tiles_per_seq, 0)),
        ],
        out_specs=[
            pl.BlockSpec((1, SLOTS_PER_STEP, tm, LANE), lambda i, j: (i // tiles_per_seq, j, i % tiles_per_seq, 0)),
            pl.BlockSpec((1, tm, LANE), lambda i, j: (i // tiles_per_seq, i % tiles_per_seq, 0)),
            pl.BlockSpec((1, 2 * NSA_GROUPS, tm // CMP_STRIDE, CMP_STRIDE * HEAD_DIM),
                         lambda i, j: (i // tiles_per_seq, 0, i % tiles_per_seq, 0)),
        ],
        out_shape=[
            jax.ShapeDtypeStruct((batch, N_SLOTS, t_len, LANE), BF16),
            jax.ShapeDtypeStruct((batch, t_len, LANE), F32),
            jax.ShapeDtypeStruct((batch, 2 * NSA_GROUPS, t_len // CMP_STRIDE, CMP_STRIDE * HEAD_DIM), BF16),
        ],
        scratch_shapes=[pltpu.VMEM((tm, D_MODEL), BF16), pltpu.VMEM((tm, LANE), F32)],
        compiler_params=_cparams(("parallel", "arbitrary")),
        name="in_proj",
    )(x2, norm_w, w_r, w_g, cos_f, sin_f)


def _compress_kernel(r_ref, w1_ref, pos_ref, w2_ref, o_ref, *, n_rows):
    w1 = w1_ref[0]
    a = jnp.dot(r_ref[0, 0], w1, preferred_element_type=F32)
    c = (jnp.dot(pos_ref[0], w1[:, :CMP_HIDDEN], preferred_element_type=F32)
         + jnp.dot(pos_ref[1], w1[:, CMP_HIDDEN:], preferred_element_type=F32))[0:1]
    nxt = pltpu.roll(a[:, CMP_HIDDEN:], n_rows - 1, 0)
    hid = a[:, :CMP_HIDDEN] + nxt + c
    hid = hid * jax.nn.sigmoid(hid)
    out = jnp.dot(hid.astype(BF16), w2_ref[0], preferred_element_type=F32)
    row = lax.broadcasted_iota(jnp.int32, out.shape, 0)
    o_ref[0, 0, 0] = jnp.where(row < n_rows - 1, out, 0.0).astype(BF16)


def _compress(p16, w1cat, pos2, w2, batch, n_rows):
    kern = functools.partial(_compress_kernel, n_rows=n_rows)

    return pl.pallas_call(
        kern,
        grid=(batch, 2, NSA_GROUPS),
        in_specs=[
            pl.BlockSpec((1, 1, n_rows, CMP_STRIDE * HEAD_DIM), lambda b, kv, g: (b, kv * NSA_GROUPS + g, 0, 0)),
            pl.BlockSpec((1, CMP_STRIDE * HEAD_DIM, 2 * CMP_HIDDEN), lambda b, kv, g: (kv, 0, 0)),
            pl.BlockSpec((2, SUBLANE, CMP_STRIDE * HEAD_DIM), lambda b, kv, g: (0, 0, 0)),
            pl.BlockSpec((1, CMP_HIDDEN, HEAD_DIM), lambda b, kv, g: (kv, 0, 0)),
        ],
        out_specs=pl.BlockSpec((1, 1, 1, n_rows, HEAD_DIM), lambda b, kv, g: (b, kv, g, 0, 0)),
        out_shape=jax.ShapeDtypeStruct((batch, 2, NSA_GROUPS, n_rows, HEAD_DIM), BF16),
        compiler_params=_cparams(("parallel", "parallel", "parallel")),
        name="compress",
    )(p16, w1cat, pos2, w2)


_NT = (((1,), (1,)), ((), ()))
CMP_PER_SEL = SEL_BLOCK // CMP_STRIDE


def _cmp_select_body(q_ref, kc_ref, vc_ref, map_ref, o_ref, bias_ref, start, g, q_rows, *, n_top, w, bw):
    n_selp = map_ref.shape[0]
    q = q_ref[0, g * NSA_HPG:(g + 1) * NSA_HPG, q_rows, :].reshape(NSA_HPG * Q_BLOCK, HEAD_DIM)
    s = lax.dot_general(q, kc_ref[0, 0, g, :w, :], _NT, preferred_element_type=F32)
    s = s.reshape(NSA_HPG, Q_BLOCK, w)
    t_q = start + lax.broadcasted_iota(jnp.int32, (1, Q_BLOCK, w), 1)
    n_id = lax.broadcasted_iota(jnp.int32, (1, Q_BLOCK, w), 2)
    valid = n_id * CMP_STRIDE + (CMP_BLOCK - 1) <= t_q
    s = jnp.where(valid, s, NEG_INF)
    e = jnp.exp2(s - jnp.max(s, axis=-1, keepdims=True))
    den = jnp.sum(e, axis=-1, keepdims=True)
    sees_any = start + lax.broadcasted_iota(jnp.int32, (1, Q_BLOCK, 1), 1) >= CMP_BLOCK - 1
    p = e * jnp.where(sees_any, 1.0 / den, 0.0)
    o = jnp.dot(p.reshape(NSA_HPG * Q_BLOCK, w).astype(BF16), vc_ref[0, 0, g, :w, :],
                preferred_element_type=F32)
    for h in range(NSA_HPG):
        hd = g * NSA_HPG + h
        o_ref[0, q_rows, hd * HEAD_DIM:(hd + 1) * HEAD_DIM] = o[h * Q_BLOCK:(h + 1) * Q_BLOCK].astype(BF16)

    psum = p[0] + p[1] + p[2] + p[3]
    p_hi = psum.astype(BF16)
    p_lo = (psum - p_hi.astype(F32)).astype(BF16)
    smap_t = map_ref[:bw, :w]
    imp = (lax.dot_general(smap_t, p_hi, _NT, preferred_element_type=F32)
           + lax.dot_general(smap_t, p_lo, _NT, preferred_element_type=F32))
    blk = lax.broadcasted_iota(jnp.int32, (bw, Q_BLOCK), 0)
    cur = (start + lax.broadcasted_iota(jnp.int32, (bw, Q_BLOCK), 1)) // SEL_BLOCK
    forced = (blk == 0) | (blk == cur) | (blk == cur - 1)
    blk_f = blk.astype(F32)
    work = jnp.where(forced, -jnp.inf, jnp.where(blk > cur, NEG_INF, imp))
    for _ in range(n_top - 3):
        top = jnp.max(work, axis=0, keepdims=True)
        first = jnp.min(jnp.where(work == top, blk_f, 1e9), axis=0, keepdims=True)
        work = jnp.where(blk_f == first, -jnp.inf, work)
    keep = (work == -jnp.inf) & (blk <= cur)
    bias_t = jnp.where(keep, 0.0, NEG_INF)
    if bw < n_selp:
        bias_t = jnp.concatenate([bias_t, jnp.full((n_selp - bw, Q_BLOCK), NEG_INF, F32)], axis=0)
    bias_ref[0, g, q_rows, :] = bias_t.T.astype(BF16)


def _cmp_select_kernel(q_ref, kc_ref, vc_ref, map_ref, o_ref, bias_ref, *, n_top, chunk, tq):
    start = pl.program_id(1) * tq
    variant = ((start + tq - 1) // SEL_BLOCK) // (chunk // CMP_PER_SEL)

    def all_chains(w):
        for qc in range(tq // Q_BLOCK):
            for g in range(NSA_GROUPS):
                _cmp_select_body(q_ref, kc_ref, vc_ref, map_ref, o_ref, bias_ref, start + qc * Q_BLOCK, g,
                                 slice(qc * Q_BLOCK, (qc + 1) * Q_BLOCK),
                                 n_top=n_top, w=w, bw=min(w // CMP_PER_SEL, map_ref.shape[0]))

    for v in range(map_ref.shape[1] // chunk):
        pl.when(variant == v)(functools.partial(all_chains, (v + 1) * chunk))


def _cmp_select(p, kvc, sel_map, batch, t_len, tq):
    n_rows = t_len // CMP_STRIDE
    n_selp = sel_map.shape[0]
    n_top = min(N_SELECT, t_len // SEL_BLOCK)
    assert n_top >= 3
    chunk = min(256, n_rows)
    assert n_rows % chunk == 0 and (chunk // CMP_PER_SEL) % 8 == 0
    kern = functools.partial(_cmp_select_kernel, n_top=n_top, chunk=chunk, tq=tq)
    return pl.pallas_call(
        kern,
        grid=(batch, t_len // tq),
        in_specs=[
            pl.BlockSpec((1, NSA_HEADS, tq, HEAD_DIM), lambda b, c: (b, 0, c, 0)),
            pl.BlockSpec((1, 1, NSA_GROUPS, n_rows, HEAD_DIM), lambda b, c: (b, 0, 0, 0, 0)),
            pl.BlockSpec((1, 1, NSA_GROUPS, n_rows, HEAD_DIM), lambda b, c: (b, 1, 0, 0, 0)),
            pl.BlockSpec((n_selp, n_rows), lambda b, c: (0, 0)),
        ],
        out_specs=[
            pl.BlockSpec((1, tq, NSA_WIDTH), lambda b, c: (b, c, 0)),
            pl.BlockSpec((1, NSA_GROUPS, tq, n_selp), lambda b, c: (b, 0, c, 0)),
        ],
        out_shape=[
            jax.ShapeDtypeStruct((batch, t_len, NSA_WIDTH), BF16),
            jax.ShapeDtypeStruct((batch, NSA_GROUPS, t_len, n_selp), BF16),
        ],
        compiler_params=_cparams(("parallel", "parallel")),
        name="cmp_select",
    )(p, kvc, kvc, sel_map)


def _lane_tile(x, n):
    return x if n == 1 else jnp.concatenate([x] * n, axis=1)


def _flash_update(s, v, rows, m_ref, l_ref, acc_ref):
    n_keys = s.shape[1]
    dv = acc_ref.shape[-1]
    m_prev = m_ref[rows, :]
    m_new = jnp.maximum(m_prev, jnp.max(s, axis=-1, keepdims=True))
    alpha = jnp.exp2(m_prev - m_new)
    p = jnp.exp2(s - _lane_tile(m_new, n_keys // LANE))
    if l_ref is not None:
        psum = p[:, :LANE]
        for cc in range(1, n_keys // LANE):
            psum = psum + p[:, cc * LANE:(cc + 1) * LANE]
        l_ref[rows, :] = alpha * l_ref[rows, :] + psum
    acc_ref[rows, :] = (_lane_tile(alpha, dv // LANE) * acc_ref[rows, :]
                        + jnp.dot(p.astype(BF16), v, preferred_element_type=F32))
    m_ref[rows, :] = m_new


def _causal_triangle(rb):
    return lax.broadcasted_iota(jnp.int32, (rb, rb), 1) <= lax.broadcasted_iota(jnp.int32, (rb, rb), 0)


def _sel_attn_kernel(q_ref, bias_ref, k_ref, v_ref, hot_ref, o_ref, qa_ref, m_ref, acc_ref,
                     *, tq, tk, rb, rl, n_halves):
    start = pl.program_id(2) * tq
    n_rows = NSA_HPG * tq
    keys_per_half = LANE * SEL_BLOCK
    for hf in range(n_halves):
        for h in range(NSA_HPG):
            qa_ref[hf, h * tq:(h + 1) * tq, :LANE] = q_ref[0, h]
            qa_ref[hf, h * tq:(h + 1) * tq, LANE:] = bias_ref[0, 0, :, hf * LANE:(hf + 1) * LANE]
    m_ref[...] = jnp.full(m_ref.shape, NEG_INF, F32)
    acc_ref[...] = jnp.zeros(acc_ref.shape, F32)

    def k_aug(k0, n):
        return jnp.concatenate([k_ref[0, 0, pl.ds(k0, n), :], hot_ref[pl.ds(k0, n), :]], axis=1)

    def v_aug(k0, n):
        return jnp.concatenate([v_ref[0, 0, pl.ds(k0, n), :], jnp.ones((n, LANE), BF16)], axis=1)

    def body(j, carry):
        k0 = pl.multiple_of(j * tk, tk)
        ka, v, hf = k_aug(k0, tk), v_aug(k0, tk), k0 // keys_per_half
        for r in range(n_rows // rl):
            rows = pl.ds(r * rl, rl)
            s = lax.dot_general(qa_ref[hf, rows, :], ka, _NT, preferred_element_type=F32)
            _flash_update(s, v, rows, m_ref, None, acc_ref)
        return carry

    lax.fori_loop(0, start // tk, body, 0)

    tri = _causal_triangle(rb)
    for d in range(tq // rb):
        k0 = pl.multiple_of(start + d * rb, rb)
        ka, v, hf = k_aug(k0, rb), v_aug(k0, rb), k0 // keys_per_half
        for r in range(n_rows // rb):
            q_off = (r * rb) % tq
            if d * rb > q_off:
                continue
            rows = pl.ds(r * rb, rb)
            s = lax.dot_general(qa_ref[hf, rows, :], ka, _NT, preferred_element_type=F32)
            if d * rb == q_off:
                s = jnp.where(tri, s, NEG_INF)
            _flash_update(s, v, rows, m_ref, None, acc_ref)

    for h in range(NSA_HPG):
        hr = slice(h * tq, (h + 1) * tq)
        o_ref[0, :, h * HEAD_DIM:(h + 1) * HEAD_DIM] = (acc_ref[hr, :HEAD_DIM] / acc_ref[hr, HEAD_DIM:]).astype(BF16)


def _resident(block_shape, index_map):
    return pl.BlockSpec(block_shape, index_map, pipeline_mode=pl.Buffered(1))


def _sel_attn(p, bias, hot, batch, t_len, tq, tk, rb, rl):
    assert tq % tk == 0 and tk % rb == 0 and (LANE * SEL_BLOCK) % tk == 0
    n_selp = bias.shape[-1]
    n_halves = n_selp // LANE
    n_rows = NSA_HPG * tq
    assert n_rows % rl == 0
    kern = functools.partial(_sel_attn_kernel, tq=tq, tk=tk, rb=rb, rl=rl, n_halves=n_halves)
    return pl.pallas_call(
        kern,
        grid=(batch, NSA_GROUPS, t_len // tq),
        in_specs=[
            pl.BlockSpec((1, NSA_HPG, tq, HEAD_DIM), lambda b, g, c: (b, g, c, 0)),
            pl.BlockSpec((1, 1, tq, n_selp), lambda b, g, c: (b, g, c, 0)),
            _resident((1, 1, t_len, HEAD_DIM), lambda b, g, c: (b, SLOT_KS + g, 0, 0)),
            _resident((1, 1, t_len, HEAD_DIM), lambda b, g, c: (b, SLOT_VS + g, 0, 0)),
            _resident((t_len, LANE), lambda b, g, c: (0, 0)),
        ],
        out_specs=pl.BlockSpec((1, tq, NSA_HPG * HEAD_DIM), lambda b, g, c: (b, c, g)),
        out_shape=jax.ShapeDtypeStruct((batch, t_len, NSA_WIDTH), BF16),
        scratch_shapes=[
            pltpu.VMEM((n_halves, n_rows, 2 * LANE), BF16),
            pltpu.VMEM((n_rows, LANE), F32),
            pltpu.VMEM((n_rows, 2 * HEAD_DIM), F32),
        ],
        compiler_params=_cparams(("parallel", "parallel", "arbitrary")),
        name="sel_attn",
    )(p, bias, p, p, hot)


def _win_attn_kernel(q_ref, k_ref, v_ref, o_ref, *, tq, span):
    start = pl.program_id(1) * tq
    rows = NSA_HPG * Q_BLOCK
    for qc in range(tq // Q_BLOCK):
        q_rows = slice(qc * Q_BLOCK, (qc + 1) * Q_BLOCK)
        q0 = start + qc * Q_BLOCK
        k0 = pl.multiple_of(jnp.maximum(q0 + Q_BLOCK - span, 0), Q_BLOCK)
        t_q = q0 + lax.broadcasted_iota(jnp.int32, (1, Q_BLOCK, span), 1)
        k_pos = k0 + lax.broadcasted_iota(jnp.int32, (1, Q_BLOCK, span), 2)
        valid = (k_pos <= t_q) & (k_pos > t_q - WINDOW)
        for g in range(NSA_GROUPS):
            q = q_ref[0, g * NSA_HPG:(g + 1) * NSA_HPG, q_rows, :].reshape(rows, HEAD_DIM)
            s = lax.dot_general(q, k_ref[0, g, pl.ds(k0, span), :], _NT, preferred_element_type=F32)
            s = jnp.where(valid, s.reshape(NSA_HPG, Q_BLOCK, span), NEG_INF)
            e = jnp.exp2(s - jnp.max(s, axis=-1, keepdims=True))
            v = jnp.concatenate([v_ref[0, g, pl.ds(k0, span), :], jnp.ones((span, LANE), BF16)], axis=1)
            o = jnp.dot(e.reshape(rows, span).astype(BF16), v, preferred_element_type=F32)
            o = o[:, :HEAD_DIM] / o[:, HEAD_DIM:]
            for h in range(NSA_HPG):
                hd = g * NSA_HPG + h
                o_ref[0, q_rows, hd * HEAD_DIM:(hd + 1) * HEAD_DIM] = o[h * Q_BLOCK:(h + 1) * Q_BLOCK].astype(BF16)


def _win_attn(p, batch, t_len, tq):
    span = min(WINDOW + Q_BLOCK, t_len)
    kern = functools.partial(_win_attn_kernel, tq=tq, span=span)
    return pl.pallas_call(
        kern,
        grid=(batch, t_len // tq),
        in_specs=[
            pl.BlockSpec((1, NSA_HEADS, tq, HEAD_DIM), lambda b, c: (b, 0, c, 0)),
            _resident((1, NSA_GROUPS, t_len, HEAD_DIM), lambda b, c: (b, SLOT_KW // NSA_GROUPS, 0, 0)),
            _resident((1, NSA_GROUPS, t_len, HEAD_DIM), lambda b, c: (b, SLOT_VW // NSA_GROUPS, 0, 0)),
        ],
        out_specs=pl.BlockSpec((1, tq, NSA_WIDTH), lambda b, c: (b, c, 0)),
        out_shape=jax.ShapeDtypeStruct((batch, t_len, NSA_WIDTH), BF16),
        compiler_params=_cparams(("parallel", "parallel")),
        name="win_attn",
    )(p, p, p)


def _diff_attn_kernel(q1_ref, q2_ref, k1_ref, k2_ref, v_ref, lam_ref, sw_ref, o_ref,
                      m_ref, l_ref, acc_ref, *, tq, tk, rb, rl):
    start = pl.program_id(2) * tq
    m_ref[...] = jnp.full(m_ref.shape, NEG_INF, F32)
    l_ref[...] = jnp.zeros(l_ref.shape, F32)
    acc_ref[...] = jnp.zeros(acc_ref.shape, F32)
    q_refs = (q1_ref, q2_ref)
    k_refs = (k1_ref, k2_ref)

    def v_tile(k0, n):
        return jnp.concatenate([v_ref[0, 0, pl.ds(k0, n), :], v_ref[0, 1, pl.ds(k0, n), :]], axis=1)

    def body(j, carry):
        k0 = pl.multiple_of(j * tk, tk)
        v = v_tile(k0, tk)
        for mp in range(2):
            kt = k_refs[mp][0, 0, pl.ds(k0, tk), :]
            for r in range(tq // rl):
                rows = pl.ds(r * rl, rl)
                s = lax.dot_general(q_refs[mp][0, 0, rows, :], kt, _NT, preferred_element_type=F32)
                _flash_update(s, v, rows, m_ref.at[mp], l_ref.at[mp], acc_ref.at[mp])
        return carry

    lax.fori_loop(0, start // tk, body, 0)

    tri = _causal_triangle(rb)
    for d in range(tq // rb):
        k0 = pl.multiple_of(start + d * rb, rb)
        v = v_tile(k0, rb)
        for mp in range(2):
            kt = k_refs[mp][0, 0, pl.ds(k0, rb), :]
            for r in range(d, tq // rb):
                rows = pl.ds(r * rb, rb)
                s = lax.dot_general(q_refs[mp][0, 0, rows, :], kt, _NT, preferred_element_type=F32)
                if r == d:
                    s = jnp.where(tri, s, NEG_INF)
                _flash_update(s, v, rows, m_ref.at[mp], l_ref.at[mp], acc_ref.at[mp])

    lv = lam_ref[...]
    lam = (jnp.exp(jnp.sum(lv[0:1] * lv[1:2], axis=-1, keepdims=True))
           - jnp.exp(jnp.sum(lv[2:3] * lv[3:4], axis=-1, keepdims=True)) + LAMBDA_INIT)
    l1 = jnp.sum(l_ref[0], axis=-1, keepdims=True)
    l2 = jnp.sum(l_ref[1], axis=-1, keepdims=True)
    o = acc_ref[0] / l1 - lam * (acc_ref[1] / l2)
    y = o * lax.rsqrt(jnp.mean(o * o, axis=-1, keepdims=True) + SUBLN_EPS)
    o_ref[0] = (y * sw_ref[...] * (1.0 - LAMBDA_INIT)).astype(BF16)


def _diff_attn(p, lam4, subln_w, batch, t_len, tq, tk, rb, rl):
    assert tq % tk == 0 and tk % rb == 0 and tq % rl == 0
    kern = functools.partial(_diff_attn_kernel, tq=tq, tk=tk, rb=rb, rl=rl)
    return pl.pallas_call(
        kern,
        grid=(batch, DIFF_HEADS, t_len // tq),
        in_specs=[
            pl.BlockSpec((1, 1, tq, DIFF_QK), lambda b, h, i: (b, SLOT_QB + h, i, 0)),
            pl.BlockSpec((1, 1, tq, DIFF_QK), lambda b, h, i: (b, SLOT_QB + DIFF_HEADS + h, i, 0)),
            _resident((1, 1, t_len, DIFF_QK), lambda b, h, i: (b, SLOT_KB + h, 0, 0)),
            _resident((1, 1, t_len, DIFF_QK), lambda b, h, i: (b, SLOT_KB + DIFF_HEADS + h, 0, 0)),
            _resident((1, 2, t_len, LANE), lambda b, h, i: (b, SLOT_VB // 2 + h, 0, 0)),
            pl.BlockSpec((SUBLANE, DIFF_QK), lambda b, h, i: (0, 0)),
            pl.BlockSpec((1, DIFF_V), lambda b, h, i: (0, 0)),
        ],
        out_specs=pl.BlockSpec((1, tq, DIFF_V), lambda b, h, i: (b, i, h)),
        out_shape=jax.ShapeDtypeStruct((batch, t_len, DIFF_WIDTH), BF16),
        scratch_shapes=[
            pltpu.VMEM((2, tq, LANE), F32),
            pltpu.VMEM((2, tq, LANE), F32),
            pltpu.VMEM((2, tq, DIFF_V), F32),
        ],
        compiler_params=_cparams(("parallel", "parallel", "arbitrary")),
        name="diff_attn",
    )(p, p, p, p, p, lam4, subln_w)


OUT_K_SLOTS = 2


def _out_proj_kernel(x_ref, oc_ref, os_ref, ow_ref, g_ref, za_ref, ob_ref, zb_ref, w_ref, fw_ref, o_ref):
    gates = jax.nn.sigmoid(g_ref[0])
    tm = gates.shape[0]

    def nsa_piece(hd):
        cols = slice(hd * HEAD_DIM, (hd + 1) * HEAD_DIM)
        g0 = jnp.broadcast_to(gates[:, 3 * hd:3 * hd + 1], (tm, HEAD_DIM))
        g1 = jnp.broadcast_to(gates[:, 3 * hd + 1:3 * hd + 2], (tm, HEAD_DIM))
        g2 = jnp.broadcast_to(gates[:, 3 * hd + 2:3 * hd + 3], (tm, HEAD_DIM))
        o = (g0 * oc_ref[0, :, cols].astype(F32) + g1 * os_ref[0, :, cols].astype(F32)
             + g2 * ow_ref[0, :, cols].astype(F32))
        z = za_ref[0, hd].astype(F32)
        return (o * (z * jax.nn.sigmoid(z))).astype(BF16)

    def diff_piece(sl):
        z = zb_ref[0, sl].astype(F32)
        return (ob_ref[0, :, sl * LANE:(sl + 1) * LANE].astype(F32) * (z * jax.nn.sigmoid(z))).astype(BF16)

    pieces = [functools.partial(nsa_piece, hd) for hd in range(NSA_HEADS)]
    pieces += [functools.partial(diff_piece, sl) for sl in range(DIFF_WIDTH // LANE)]
    y = x_ref[0]
    for kc in range(0, len(pieces), OUT_K_SLOTS):
        chunk = jnp.concatenate([f() for f in pieces[kc:kc + OUT_K_SLOTS]], axis=1)
        y = y + jnp.dot(chunk, w_ref[kc * LANE:(kc + OUT_K_SLOTS) * LANE, :], preferred_element_type=F32)
    y = y * lax.rsqrt(jnp.mean(y * y, axis=-1, keepdims=True) + NORM_EPS)
    o_ref[0] = y * fw_ref[...]


def _out_proj(x, o_cmp, o_sel, o_win, gl, p, o_b, w_out, final_w, batch, t_len, tm):
    row = lambda b, i: (b, i, 0)
    return pl.pallas_call(
        _out_proj_kernel,
        grid=(batch, t_len // tm),
        in_specs=[
            pl.BlockSpec((1, tm, D_MODEL), row),
            pl.BlockSpec((1, tm, NSA_WIDTH), row),
            pl.BlockSpec((1, tm, NSA_WIDTH), row),
            pl.BlockSpec((1, tm, NSA_WIDTH), row),
            pl.BlockSpec((1, tm, LANE), row),
            pl.BlockSpec((1, NSA_HEADS, tm, LANE), lambda b, i: (b, SLOT_ZA // NSA_HEADS, i, 0)),
            pl.BlockSpec((1, tm, DIFF_WIDTH), row),
            pl.BlockSpec((1, DIFF_WIDTH // LANE, tm, LANE), lambda b, i: (b, SLOT_ZB // (DIFF_WIDTH // LANE), i, 0)),
            _resident((MIX_WIDTH, D_MODEL), lambda b, i: (0, 0)),
            pl.BlockSpec((1, D_MODEL), lambda b, i: (0, 0)),
        ],
        out_specs=pl.BlockSpec((1, tm, D_MODEL), row),
        out_shape=jax.ShapeDtypeStruct((batch, t_len, D_MODEL), F32),
        compiler_params=_cparams(("parallel", "parallel")),
        name="out_proj",
    )(x, o_cmp, o_sel, o_win, gl, p, o_b, p, w_out, final_w)


def _reorder_w_in(w):
    def section(off, n_slots):
        return w[:, off:off + n_slots * LANE]
    cols = [section(_O_QA, 8), section(_O_QB, 8), section(_O_KB, 8), section(_O_KC, 2), section(_O_KS, 2),
            section(_O_KW, 2), section(_O_VC, 2), section(_O_ZA, 8), section(_O_ZB, 8), section(_O_VB, 8),
            section(_O_VS, 2), section(_O_VW, 2)]
    w_r = jnp.concatenate(cols, axis=1).astype(BF16)
    assert w_r.shape[1] == N_SLOTS * LANE
    w_g = jnp.pad(w[:, _O_G:_O_G + NSA_HEADS * NSA_BRANCHES],
                  ((0, 0), (0, LANE - NSA_HEADS * NSA_BRANCHES))).astype(BF16)
    return w_r, w_g


def _sel_map(t_len, n_selp):
    n_rows = t_len // CMP_STRIDE
    c0 = np.arange(n_rows)[:, None] * CMP_STRIDE
    s0 = np.arange(n_selp)[None, :] * SEL_BLOCK
    ov = np.minimum(c0 + CMP_BLOCK, s0 + SEL_BLOCK) - np.maximum(c0, s0)
    m = np.maximum(ov, 0) / CMP_BLOCK
    m[n_rows - 1:] = 0.0
    m[:, t_len // SEL_BLOCK:] = 0.0
    return jnp.asarray(m.T, dtype=BF16)


def _block_one_hot(t_len):
    blk = (np.arange(t_len) // SEL_BLOCK) % LANE
    return jnp.asarray(blk[:, None] == np.arange(LANE)[None, :], dtype=BF16)


def _rope_tables(t_len):
    inv = ROPE_THETA ** (-np.arange(0, HEAD_DIM, 2, dtype=np.float64) / HEAD_DIM)
    ang = np.arange(t_len, dtype=np.float64)[:, None] * inv[None, :]
    cos, sin = np.cos(ang), np.sin(ang)
    return (jnp.asarray(np.concatenate([cos, cos], axis=1), dtype=F32),
            jnp.asarray(np.concatenate([-sin, sin], axis=1), dtype=F32))


class _Tiles(NamedTuple):
    in_tm: int
    out_tm: int
    cmp_tq: int
    win_tq: int
    attn_tq: int
    diag_rb: int
    sel_tk: int
    sel_rl: int
    diff_tk: int
    diff_rl: int


def _tiles(t_len):
    return _Tiles(in_tm=min(1024, t_len), out_tm=min(512, t_len), cmp_tq=min(512, t_len), win_tq=min(2048, t_len),
                  attn_tq=min(2048, t_len), diag_rb=256, sel_tk=min(2048, t_len), sel_rl=512, diff_tk=min(2048, t_len), diff_rl=128)


def kernel(x, norm_w, w_in, cmp_pos, cmp_k_w1, cmp_k_w2, cmp_v_w1, cmp_v_w2, lam_q1, lam_k1, lam_q2, lam_k2,
           subln_w, w_out, final_norm_w):
    batch, t_len, _ = x.shape
    n_rows = t_len // CMP_STRIDE
    n_selp = -(-(t_len // SEL_BLOCK) // LANE) * LANE

    tiles = _tiles(t_len)
    cos_f, sin_f = _rope_tables(t_len)
    w_r, w_g = _reorder_w_in(w_in[0])
    p, gl, pc = _in_proj(x.reshape(batch * t_len, D_MODEL), norm_w[0][None, :], w_r, w_g, cos_f, sin_f,
                         batch, t_len, tiles.in_tm)

    half = CMP_STRIDE * HEAD_DIM
    w1cat = jnp.stack([jnp.concatenate([w[:half], w[half:]], axis=1) for w in (cmp_k_w1[0], cmp_v_w1[0])]).astype(BF16)
    w2 = jnp.stack([cmp_k_w2[0], cmp_v_w2[0]]).astype(BF16)
    pos2 = jnp.broadcast_to(cmp_pos[0].reshape(2, 1, half), (2, SUBLANE, half)).astype(BF16)
    kvc = _compress(pc, w1cat, pos2, w2, batch, n_rows)

    o_cmp, bias = _cmp_select(p, kvc, _sel_map(t_len, n_selp), batch, t_len, tq=tiles.cmp_tq)
    o_sel = _sel_attn(p, bias, _block_one_hot(t_len), batch, t_len,
                      tq=tiles.attn_tq, tk=tiles.sel_tk, rb=tiles.diag_rb, rl=tiles.sel_rl)
    o_win = _win_attn(p, batch, t_len, tq=tiles.win_tq)

    lam4 = jnp.pad(jnp.stack([lam_q1[0], lam_k1[0], lam_q2[0], lam_k2[0]]), ((0, SUBLANE - 4), (0, 0)))
    o_b = _diff_attn(p, lam4, subln_w[0][None, :], batch, t_len,
                     tq=tiles.attn_tq, tk=tiles.diff_tk, rb=tiles.diag_rb, rl=tiles.diff_rl)

    return _out_proj(x, o_cmp, o_sel, o_win, gl, p, o_b, w_out[0].astype(BF16), final_norm_w[None, :],
                     batch, t_len, tm=tiles.out_tm)
```

```python
import functools
import math
from typing import NamedTuple

import numpy as np
import jax
import jax.numpy as jnp
from jax import lax
from jax.experimental import pallas as pl
from jax.experimental.pallas import tpu as pltpu

F32 = jnp.float32
BF16 = jnp.bfloat16

D_MODEL = 2048
HEAD_DIM = 128
NSA_HEADS = 8
NSA_GROUPS = 2
NSA_HPG = NSA_HEADS // NSA_GROUPS
NSA_WIDTH = NSA_HEADS * HEAD_DIM
NSA_KV = NSA_GROUPS * HEAD_DIM
NSA_BRANCHES = 3
CMP_BLOCK = 32
CMP_STRIDE = 16
CMP_HIDDEN = 512
SEL_BLOCK = 64
N_SELECT = 16
WINDOW = 512
DIFF_HEADS = 4
DIFF_QK = 128
DIFF_V = 256
DIFF_WIDTH = DIFF_HEADS * DIFF_V
MIX_WIDTH = NSA_WIDTH + DIFF_WIDTH
Q_BLOCK = 128
ROPE_THETA = 10000.0
NORM_EPS = 1e-6
SUBLN_EPS = 1e-5
NEG_INF = -1e30
LAMBDA_INIT = 0.8 - 0.6 * math.exp(-0.3 * 0)

LANE = 128
SUBLANE = 8
VMEM_LIMIT = 56 * 1024 * 1024

SLOT_QA = 0
SLOT_QB = 8
SLOT_KB = 16
SLOT_KC = 24
SLOT_KS = 26
SLOT_KW = 28
N_ROPE_SLOTS = 30
SLOT_VC = 30
SLOT_ZA = 32
SLOT_ZB = 40
SLOT_VB = 48
SLOT_VS = 56
SLOT_VW = 58
N_SLOTS = 60
SLOTS_PER_STEP = 12
N_QUERY_SLOTS = 16
QK_EXP2_SCALE = (HEAD_DIM ** -0.5) * math.log2(math.e)

_OFF = np.concatenate([[0], np.cumsum([NSA_WIDTH, NSA_KV, NSA_KV, NSA_KV, NSA_KV, NSA_KV, NSA_KV,
                                       NSA_HEADS * NSA_BRANCHES, NSA_WIDTH, 1024, 1024, DIFF_WIDTH, DIFF_WIDTH])])
(_O_QA, _O_KC, _O_VC, _O_KS, _O_VS, _O_KW, _O_VW, _O_G, _O_ZA, _O_QB, _O_KB, _O_VB, _O_ZB, _O_END) = [int(v) for v in _OFF]


def _cparams(sem):
    return pltpu.CompilerParams(dimension_semantics=sem, vmem_limit_bytes=VMEM_LIMIT)


def _in_proj_kernel(x_ref, w_ref, wg_ref, cos_ref, sin_ref, p_ref, g_ref, c_ref, h_ref, stage_ref, r_ref):
    j = pl.program_id(1)

    @pl.when(j == 0)
    def _():
        x = x_ref[...]
        h = x.astype(BF16)
        h_ref[...] = h
        r = lax.rsqrt(jnp.mean(x * x, axis=-1, keepdims=True) + NORM_EPS)
        r_ref[...] = jnp.broadcast_to(r, r_ref.shape)
        g_ref[0] = jnp.dot(h, wg_ref[...], preferred_element_type=F32) * r

    acc = jnp.dot(h_ref[...], w_ref[...], preferred_element_type=F32)

    row_scale = r_ref[...]
    cos_t = cos_ref[...] * row_scale
    sin_t = sin_ref[...] * row_scale
    for s in range(SLOTS_PER_STEP):
        slot = j * SLOTS_PER_STEP + s
        cos = jnp.where(slot < N_ROPE_SLOTS, cos_t, row_scale)
        sin = jnp.where(slot < N_ROPE_SLOTS, sin_t, 0.0)
        a = acc[:, s * LANE:(s + 1) * LANE]
        r = a * cos + pltpu.roll(a, HEAD_DIM // 2, 1) * sin
        f = jnp.where(slot < N_QUERY_SLOTS, QK_EXP2_SCALE, 1.0)
        p_ref[0, s] = (r * f).astype(BF16)

    @pl.when(j == SLOT_KC // SLOTS_PER_STEP)
    def _():
        n_out = stage_ref.shape[0] // CMP_STRIDE
        for n, slot in enumerate((SLOT_KC, SLOT_KC + 1, SLOT_VC, SLOT_VC + 1)):
            stage_ref[...] = p_ref[0, slot % SLOTS_PER_STEP].astype(F32)
            for l in range(CMP_STRIDE):
                c_ref[0, n, :, l * LANE:(l + 1) * LANE] = stage_ref[pl.ds(l, n_out, stride=CMP_STRIDE), :].astype(BF16)


def _in_proj(x2, w_r, w_g, cos_f, sin_f, batch, t_len, tm):
    m = x2.shape[0]
    tiles_per_seq = t_len // tm
    assert SLOT_KC // SLOTS_PER_STEP == (SLOT_VC + 1) // SLOTS_PER_STEP
    n_steps = N_SLOTS // SLOTS_PER_STEP
    tn = SLOTS_PER_STEP * LANE
    return pl.pallas_call(
        _in_proj_kernel,
        grid=(m // tm, n_steps),
        in_specs=[
            pl.BlockSpec((tm, D_MODEL), lambda i, j: (i, 0)),
            pl.BlockSpec((D_MODEL, tn), lambda i, j: (0, j)),
            pl.BlockSpec((D_MODEL, LANE), lambda i, j: (0, 0)),
            pl.BlockSpec((tm, LANE), lambda i, j: (i % tiles_per_seq, 0)),
            pl.BlockSpec((tm, LANE), lambda i, j: (i % tiles_per_seq, 0)),
        ],
        out_specs=[
            pl.BlockSpec((1, SLOTS_PER_STEP, tm, LANE), lambda i, j: (i // tiles_per_seq, j, i % tiles_per_seq, 0)),
            pl.BlockSpec((1, tm, LANE), lambda i, j: (i // tiles_per_seq, i % tiles_per_seq, 0)),
            pl.BlockSpec((1, 2 * NSA_GROUPS, tm // CMP_STRIDE, CMP_STRIDE * HEAD_DIM),
                         lambda i, j: (i // tiles_per_seq, 0, i % tiles_per_seq, 0)),
        ],
        out_shape=[
            jax.ShapeDtypeStruct((batch, N_SLOTS, t_len, LANE), BF16),
            jax.ShapeDtypeStruct((batch, t_len, LANE), F32),
            jax.ShapeDtypeStruct((batch, 2 * NSA_GROUPS, t_len // CMP_STRIDE, CMP_STRIDE * HEAD_DIM), BF16),
        ],
        scratch_shapes=[pltpu.VMEM((tm, D_MODEL), BF16), pltpu.VMEM((tm, LANE), F32), pltpu.VMEM((tm, LANE), F32)],
        compiler_params=_cparams(("parallel", "arbitrary")),
        name="in_proj",
    )(x2, w_r, w_g, cos_f, sin_f)


def _compress_kernel(r_ref, w1_ref, pos_ref, w2_ref, o_ref, *, n_rows):
    w1 = w1_ref[0]
    a = jnp.dot(r_ref[0, 0], w1, preferred_element_type=F32)
    c = (jnp.dot(pos_ref[0], w1[:, :CMP_HIDDEN], preferred_element_type=F32)
         + jnp.dot(pos_ref[1], w1[:, CMP_HIDDEN:], preferred_element_type=F32))[0:1]
    nxt = pltpu.roll(a[:, CMP_HIDDEN:], n_rows - 1, 0)
    hid = a[:, :CMP_HIDDEN] + nxt + c
    hid = hid * jax.nn.sigmoid(hid)
    out = jnp.dot(hid.astype(BF16), w2_ref[0], preferred_element_type=F32)
    row = lax.broadcasted_iota(jnp.int32, out.shape, 0)
    o_ref[0, 0, 0] = jnp.where(row < n_rows - 1, out, 0.0).astype(BF16)


def _compress(p16, w1cat, pos2, w2, batch, n_rows):
    kern = functools.partial(_compress_kernel, n_rows=n_rows)

    return pl.pallas_call(
        kern,
        grid=(batch, 2, NSA_GROUPS),
        in_specs=[
            pl.BlockSpec((1, 1, n_rows, CMP_STRIDE * HEAD_DIM), lambda b, kv, g: (b, kv * NSA_GROUPS + g, 0, 0)),
            pl.BlockSpec((1, CMP_STRIDE * HEAD_DIM, 2 * CMP_HIDDEN), lambda b, kv, g: (kv, 0, 0)),
            pl.BlockSpec((2, SUBLANE, CMP_STRIDE * HEAD_DIM), lambda b, kv, g: (0, 0, 0)),
            pl.BlockSpec((1, CMP_HIDDEN, HEAD_DIM), lambda b, kv, g: (kv, 0, 0)),
        ],
        out_specs=pl.BlockSpec((1, 1, 1, n_rows, HEAD_DIM), lambda b, kv, g: (b, kv, g, 0, 0)),
        out_shape=jax.ShapeDtypeStruct((batch, 2, NSA_GROUPS, n_rows, HEAD_DIM), BF16),
        compiler_params=_cparams(("parallel", "parallel", "parallel")),
        name="compress",
    )(p16, w1cat, pos2, w2)


_NT = (((1,), (1,)), ((), ()))
CMP_PER_SEL = SEL_BLOCK // CMP_STRIDE


def _cmp_select_body(q_ref, kc_ref, vc_ref, map_ref, o_ref, bias_ref, start, g, q_rows, *, n_top, w, bw):
    n_selp = map_ref.shape[0]
    q = q_ref[0, g * NSA_HPG:(g + 1) * NSA_HPG, q_rows, :].reshape(NSA_HPG * Q_BLOCK, HEAD_DIM)
    s = lax.dot_general(q, kc_ref[0, 0, g, :w, :], _NT, preferred_element_type=F32)
    s = s.reshape(NSA_HPG, Q_BLOCK, w)
    t_q = start + lax.broadcasted_iota(jnp.int32, (1, Q_BLOCK, w), 1)
    n_id = lax.broadcasted_iota(jnp.int32, (1, Q_BLOCK, w), 2)
    valid = n_id * CMP_STRIDE + (CMP_BLOCK - 1) <= t_q
    s = jnp.where(valid, s, NEG_INF)
    e = jnp.exp2(s - jnp.max(s, axis=-1, keepdims=True))
    den = jnp.sum(e, axis=-1, keepdims=True)
    sees_any = start + lax.broadcasted_iota(jnp.int32, (1, Q_BLOCK, 1), 1) >= CMP_BLOCK - 1
    p = e * jnp.where(sees_any, 1.0 / den, 0.0)
    o = jnp.dot(p.reshape(NSA_HPG * Q_BLOCK, w).astype(BF16), vc_ref[0, 0, g, :w, :],
                preferred_element_type=F32)
    for h in range(NSA_HPG):
        hd = g * NSA_HPG + h
        o_ref[0, q_rows, hd * HEAD_DIM:(hd + 1) * HEAD_DIM] = o[h * Q_BLOCK:(h + 1) * Q_BLOCK].astype(BF16)

    psum = p[0] + p[1] + p[2] + p[3]
    p_hi = psum.astype(BF16)
    p_lo = (psum - p_hi.astype(F32)).astype(BF16)
    smap_t = map_ref[:bw, :w]
    imp = (lax.dot_general(smap_t, p_hi, _NT, preferred_element_type=F32)
           + lax.dot_general(smap_t, p_lo, _NT, preferred_element_type=F32))
    blk = lax.broadcasted_iota(jnp.int32, (bw, Q_BLOCK), 0)
    cur = (start + lax.broadcasted_iota(jnp.int32, (bw, Q_BLOCK), 1)) // SEL_BLOCK
    forced = (blk == 0) | (blk == cur) | (blk == cur - 1)
    blk_f = blk.astype(F32)
    work = jnp.where(forced, -jnp.inf, jnp.where(blk > cur, NEG_INF, imp))
    for _ in range(n_top - 3):
        top = jnp.max(work, axis=0, keepdims=True)
        first = jnp.min(jnp.where(work == top, blk_f, 1e9), axis=0, keepdims=True)
        work = jnp.where(blk_f == first, -jnp.inf, work)
    keep = (work == -jnp.inf) & (blk <= cur)
    bias_t = jnp.where(keep, 0.0, NEG_INF)
    if bw < n_selp:
        bias_t = jnp.concatenate([bias_t, jnp.full((n_selp - bw, Q_BLOCK), NEG_INF, F32)], axis=0)
    bias_ref[0, g, q_rows, :] = bias_t.T.astype(BF16)


def _cmp_select_kernel(q_ref, kc_ref, vc_ref, map_ref, o_ref, bias_ref, *, n_top, chunk, tq):
    start = pl.program_id(1) * tq
    variant = ((start + tq - 1) // SEL_BLOCK) // (chunk // CMP_PER_SEL)

    def all_chains(w):
        for qc in range(tq // Q_BLOCK):
            for g in range(NSA_GROUPS):
                _cmp_select_body(q_ref, kc_ref, vc_ref, map_ref, o_ref, bias_ref, start + qc * Q_BLOCK, g,
                                 slice(qc * Q_BLOCK, (qc + 1) * Q_BLOCK),
                                 n_top=n_top, w=w, bw=min(w // CMP_PER_SEL, map_ref.shape[0]))

    for v in range(map_ref.shape[1] // chunk):
        pl.when(variant == v)(functools.partial(all_chains, (v + 1) * chunk))


def _cmp_select(p, kvc, sel_map, batch, t_len, tq):
    n_rows = t_len // CMP_STRIDE
    n_selp = sel_map.shape[0]
    n_top = min(N_SELECT, t_len // SEL_BLOCK)
    assert n_top >= 3
    chunk = min(256, n_rows)
    assert n_rows % chunk == 0 and (chunk // CMP_PER_SEL) % 8 == 0
    kern = functools.partial(_cmp_select_kernel, n_top=n_top, chunk=chunk, tq=tq)
    return pl.pallas_call(
        kern,
        grid=(batch, t_len // tq),
        in_specs=[
            pl.BlockSpec((1, NSA_HEADS, tq, HEAD_DIM), lambda b, c: (b, 0, c, 0)),
            pl.BlockSpec((1, 1, NSA_GROUPS, n_rows, HEAD_DIM), lambda b, c: (b, 0, 0, 0, 0)),
            pl.BlockSpec((1, 1, NSA_GROUPS, n_rows, HEAD_DIM), lambda b, c: (b, 1, 0, 0, 0)),
            pl.BlockSpec((n_selp, n_rows), lambda b, c: (0, 0)),
        ],
        out_specs=[
            pl.BlockSpec((1, tq, NSA_WIDTH), lambda b, c: (b, c, 0)),
            pl.BlockSpec((1, NSA_GROUPS, tq, n_selp), lambda b, c: (b, 0, c, 0)),
        ],
        out_shape=[
            jax.ShapeDtypeStruct((batch, t_len, NSA_WIDTH), BF16),
            jax.ShapeDtypeStruct((batch, NSA_GROUPS, t_len, n_selp), BF16),
        ],
        compiler_params=_cparams(("parallel", "parallel")),
        name="cmp_select",
    )(p, kvc, kvc, sel_map)


def _lane_tile(x, n):
    return x if n == 1 else jnp.concatenate([x] * n, axis=1)


def _flash_update(s, v, rows, m_ref, l_ref, acc_ref):
    n_keys = s.shape[1]
    dv = acc_ref.shape[-1]
    m_prev = m_ref[rows, :]
    m_new = jnp.maximum(m_prev, jnp.max(s, axis=-1, keepdims=True))
    alpha = jnp.exp2(m_prev - m_new)
    p = jnp.exp2(s - _lane_tile(m_new, n_keys // LANE))
    if l_ref is not None:
        psum = p[:, :LANE]
        for cc in range(1, n_keys // LANE):
            psum = psum + p[:, cc * LANE:(cc + 1) * LANE]
        l_ref[rows, :] = alpha * l_ref[rows, :] + psum
    acc_ref[rows, :] = (_lane_tile(alpha, dv // LANE) * acc_ref[rows, :]
                        + jnp.dot(p.astype(BF16), v, preferred_element_type=F32))
    m_ref[rows, :] = m_new


def _causal_triangle(rb):
    return lax.broadcasted_iota(jnp.int32, (rb, rb), 1) <= lax.broadcasted_iota(jnp.int32, (rb, rb), 0)


def _sel_attn_kernel(q_ref, bias_ref, k_ref, v_ref, hot_ref, o_ref, qa_ref, m_ref, acc_ref,
                     *, tq, tk, rb, rl, n_halves):
    start = pl.program_id(2) * tq
    n_rows = NSA_HPG * tq
    keys_per_half = LANE * SEL_BLOCK
    for hf in range(n_halves):
        for h in range(NSA_HPG):
            qa_ref[hf, h * tq:(h + 1) * tq, :LANE] = q_ref[0, h]
            qa_ref[hf, h * tq:(h + 1) * tq, LANE:] = bias_ref[0, 0, :, hf * LANE:(hf + 1) * LANE]
    m_ref[...] = jnp.full(m_ref.shape, NEG_INF, F32)
    acc_ref[...] = jnp.zeros(acc_ref.shape, F32)

    def k_aug(k0, n):
        return jnp.concatenate([k_ref[0, 0, pl.ds(k0, n), :], hot_ref[pl.ds(k0, n), :]], axis=1)

    def v_aug(k0, n):
        return jnp.concatenate([v_ref[0, 0, pl.ds(k0, n), :], jnp.ones((n, LANE), BF16)], axis=1)

    def body(j, carry):
        k0 = pl.multiple_of(j * tk, tk)
        ka, v, hf = k_aug(k0, tk), v_aug(k0, tk), k0 // keys_per_half
        for r in range(n_rows // rl):
            rows = pl.ds(r * rl, rl)
            s = lax.dot_general(qa_ref[hf, rows, :], ka, _NT, preferred_element_type=F32)
            _flash_update(s, v, rows, m_ref, None, acc_ref)
        return carry

    lax.fori_loop(0, start // tk, body, 0)

    tri = _causal_triangle(rb)
    for d in range(tq // rb):
        k0 = pl.multiple_of(start + d * rb, rb)
        ka, v, hf = k_aug(k0, rb), v_aug(k0, rb), k0 // keys_per_half
        for r in range(n_rows // rb):
            q_off = (r * rb) % tq
            if d * rb > q_off:
                continue
            rows = pl.ds(r * rb, rb)
            s = lax.dot_general(qa_ref[hf, rows, :], ka, _NT, preferred_element_type=F32)
            if d * rb == q_off:
                s = jnp.where(tri, s, NEG_INF)
            _flash_update(s, v, rows, m_ref, None, acc_ref)

    for h in range(NSA_HPG):
        hr = slice(h * tq, (h + 1) * tq)
        o_ref[0, :, h * HEAD_DIM:(h + 1) * HEAD_DIM] = (acc_ref[hr, :HEAD_DIM] / acc_ref[hr, HEAD_DIM:]).astype(BF16)


def _resident(block_shape, index_map):
    return pl.BlockSpec(block_shape, index_map, pipeline_mode=pl.Buffered(1))


def _sel_attn(p, bias, hot, batch, t_len, tq, tk, rb, rl):
    assert tq % tk == 0 and tk % rb == 0 and (LANE * SEL_BLOCK) % tk == 0
    n_selp = bias.shape[-1]
    n_halves = n_selp // LANE
    n_rows = NSA_HPG * tq
    assert n_rows % rl == 0
    kern = functools.partial(_sel_attn_kernel, tq=tq, tk=tk, rb=rb, rl=rl, n_halves=n_halves)
    return pl.pallas_call(
        kern,
        grid=(batch, NSA_GROUPS, t_len // tq),
        in_specs=[
            pl.BlockSpec((1, NSA_HPG, tq, HEAD_DIM), lambda b, g, c: (b, g, c, 0)),
            pl.BlockSpec((1, 1, tq, n_selp), lambda b, g, c: (b, g, c, 0)),
            _resident((1, 1, t_len, HEAD_DIM), lambda b, g, c: (b, SLOT_KS + g, 0, 0)),
            _resident((1, 1, t_len, HEAD_DIM), lambda b, g, c: (b, SLOT_VS + g, 0, 0)),
            _resident((t_len, LANE), lambda b, g, c: (0, 0)),
        ],
        out_specs=pl.BlockSpec((1, tq, NSA_HPG * HEAD_DIM), lambda b, g, c: (b, c, g)),
        out_shape=jax.ShapeDtypeStruct((batch, t_len, NSA_WIDTH), BF16),
        scratch_shapes=[
            pltpu.VMEM((n_halves, n_rows, 2 * LANE), BF16),
            pltpu.VMEM((n_rows, LANE), F32),
            pltpu.VMEM((n_rows, 2 * HEAD_DIM), F32),
        ],
        compiler_params=_cparams(("parallel", "parallel", "arbitrary")),
        name="sel_attn",
    )(p, bias, p, p, hot)


def _win_attn_kernel(q_ref, k_ref, v_ref, o_ref, *, tq, span):
    start = pl.program_id(1) * tq
    rows = NSA_HPG * Q_BLOCK
    for qc in range(tq // Q_BLOCK):
        q_rows = slice(qc * Q_BLOCK, (qc + 1) * Q_BLOCK)
        q0 = start + qc * Q_BLOCK
        k0 = pl.multiple_of(jnp.maximum(q0 + Q_BLOCK - span, 0), Q_BLOCK)
        t_q = q0 + lax.broadcasted_iota(jnp.int32, (1, Q_BLOCK, span), 1)
        k_pos = k0 + lax.broadcasted_iota(jnp.int32, (1, Q_BLOCK, span), 2)
        valid = (k_pos <= t_q) & (k_pos > t_q - WINDOW)
        for g in range(NSA_GROUPS):
            q = q_ref[0, g * NSA_HPG:(g + 1) * NSA_HPG, q_rows, :].reshape(rows, HEAD_DIM)
            s = lax.dot_general(q, k_ref[0, g, pl.ds(k0, span), :], _NT, preferred_element_type=F32)
            s = jnp.where(valid, s.reshape(NSA_HPG, Q_BLOCK, span), NEG_INF)
            e = jnp.exp2(s - jnp.max(s, axis=-1, keepdims=True))
            v = jnp.concatenate([v_ref[0, g, pl.ds(k0, span), :], jnp.ones((span, LANE), BF16)], axis=1)
            o = jnp.dot(e.reshape(rows, span).astype(BF16), v, preferred_element_type=F32)
            o = o[:, :HEAD_DIM] / o[:, HEAD_DIM:]
            for h in range(NSA_HPG):
                hd = g * NSA_HPG + h
                o_ref[0, q_rows, hd * HEAD_DIM:(hd + 1) * HEAD_DIM] = o[h * Q_BLOCK:(h + 1) * Q_BLOCK].astype(BF16)


def _win_attn(p, batch, t_len, tq):
    span = min(WINDOW + Q_BLOCK, t_len)
    kern = functools.partial(_win_attn_kernel, tq=tq, span=span)
    return pl.pallas_call(
        kern,
        grid=(batch, t_len // tq),
        in_specs=[
            pl.BlockSpec((1, NSA_HEADS, tq, HEAD_DIM), lambda b, c: (b, 0, c, 0)),
            _resident((1, NSA_GROUPS, t_len, HEAD_DIM), lambda b, c: (b, SLOT_KW // NSA_GROUPS, 0, 0)),
            _resident((1, NSA_GROUPS, t_len, HEAD_DIM), lambda b, c: (b, SLOT_VW // NSA_GROUPS, 0, 0)),
        ],
        out_specs=pl.BlockSpec((1, tq, NSA_WIDTH), lambda b, c: (b, c, 0)),
        out_shape=jax.ShapeDtypeStruct((batch, t_len, NSA_WIDTH), BF16),
        compiler_params=_cparams(("parallel", "parallel")),
        name="win_attn",
    )(p, p, p)


def _diff_attn_kernel(q1_ref, q2_ref, k1_ref, k2_ref, v_ref, lam_ref, sw_ref, o_ref,
                      m_ref, l_ref, acc_ref, *, tq, tk, rb, rl):
    start = pl.program_id(2) * tq
    m_ref[...] = jnp.full(m_ref.shape, NEG_INF, F32)
    l_ref[...] = jnp.zeros(l_ref.shape, F32)
    acc_ref[...] = jnp.zeros(acc_ref.shape, F32)
    q_refs = (q1_ref, q2_ref)
    k_refs = (k1_ref, k2_ref)

    def v_tile(k0, n):
        return jnp.concatenate([v_ref[0, 0, pl.ds(k0, n), :], v_ref[0, 1, pl.ds(k0, n), :]], axis=1)

    def body(j, carry):
        k0 = pl.multiple_of(j * tk, tk)
        v = v_tile(k0, tk)
        for mp in range(2):
            kt = k_refs[mp][0, 0, pl.ds(k0, tk), :]
            for r in range(tq // rl):
                rows = pl.ds(r * rl, rl)
                s = lax.dot_general(q_refs[mp][0, 0, rows, :], kt, _NT, preferred_element_type=F32)
                _flash_update(s, v, rows, m_ref.at[mp], l_ref.at[mp], acc_ref.at[mp])
        return carry

    lax.fori_loop(0, start // tk, body, 0)

    tri = _causal_triangle(rb)
    for d in range(tq // rb):
        k0 = pl.multiple_of(start + d * rb, rb)
        v = v_tile(k0, rb)
        for mp in range(2):
            kt = k_refs[mp][0, 0, pl.ds(k0, rb), :]
            for r in range(d, tq // rb):
                rows = pl.ds(r * rb, rb)
                s = lax.dot_general(q_refs[mp][0, 0, rows, :], kt, _NT, preferred_element_type=F32)
                if r == d:
                    s = jnp.where(tri, s, NEG_INF)
                _flash_update(s, v, rows, m_ref.at[mp], l_ref.at[mp], acc_ref.at[mp])

    lv = lam_ref[...]
    lam = (jnp.exp(jnp.sum(lv[0:1] * lv[1:2], axis=-1, keepdims=True))
           - jnp.exp(jnp.sum(lv[2:3] * lv[3:4], axis=-1, keepdims=True)) + LAMBDA_INIT)
    l1 = jnp.sum(l_ref[0], axis=-1, keepdims=True)
    l2 = jnp.sum(l_ref[1], axis=-1, keepdims=True)
    o = acc_ref[0] / l1 - lam * (acc_ref[1] / l2)
    y = o * lax.rsqrt(jnp.mean(o * o, axis=-1, keepdims=True) + SUBLN_EPS)
    o_ref[0] = (y * sw_ref[...] * (1.0 - LAMBDA_INIT)).astype(BF16)


def _diff_attn(p, lam4, subln_w, batch, t_len, tq, tk, rb, rl):
    assert tq % tk == 0 and tk % rb == 0 and tq % rl == 0
    kern = functools.partial(_diff_attn_kernel, tq=tq, tk=tk, rb=rb, rl=rl)
    return pl.pallas_call(
        kern,
        grid=(batch, DIFF_HEADS, t_len // tq),
        in_specs=[
            pl.BlockSpec((1, 1, tq, DIFF_QK), lambda b, h, i: (b, SLOT_QB + h, i, 0)),
            pl.BlockSpec((1, 1, tq, DIFF_QK), lambda b, h, i: (b, SLOT_QB + DIFF_HEADS + h, i, 0)),
            _resident((1, 1, t_len, DIFF_QK), lambda b, h, i: (b, SLOT_KB + h, 0, 0)),
            _resident((1, 1, t_len, DIFF_QK), lambda b, h, i: (b, SLOT_KB + DIFF_HEADS + h, 0, 0)),
            _resident((1, 2, t_len, LANE), lambda b, h, i: (b, SLOT_VB // 2 + h, 0, 0)),
            pl.BlockSpec((SUBLANE, DIFF_QK), lambda b, h, i: (0, 0)),
            pl.BlockSpec((1, DIFF_V), lambda b, h, i: (0, 0)),
        ],
        out_specs=pl.BlockSpec((1, tq, DIFF_V), lambda b, h, i: (b, i, h)),
        out_shape=jax.ShapeDtypeStruct((batch, t_len, DIFF_WIDTH), BF16),
        scratch_shapes=[
            pltpu.VMEM((2, tq, LANE), F32),
            pltpu.VMEM((2, tq, LANE), F32),
            pltpu.VMEM((2, tq, DIFF_V), F32),
        ],
        compiler_params=_cparams(("parallel", "parallel", "arbitrary")),
        name="diff_attn",
    )(p, p, p, p, p, lam4, subln_w)


OUT_K_SLOTS = 2


def _out_proj_kernel(x_ref, oc_ref, os_ref, ow_ref, g_ref, za_ref, ob_ref, zb_ref, w_ref, fw_ref, o_ref):
    gates = jax.nn.sigmoid(g_ref[0])
    tm = gates.shape[0]

    def nsa_piece(hd):
        cols = slice(hd * HEAD_DIM, (hd + 1) * HEAD_DIM)
        g0 = jnp.broadcast_to(gates[:, 3 * hd:3 * hd + 1], (tm, HEAD_DIM))
        g1 = jnp.broadcast_to(gates[:, 3 * hd + 1:3 * hd + 2], (tm, HEAD_DIM))
        g2 = jnp.broadcast_to(gates[:, 3 * hd + 2:3 * hd + 3], (tm, HEAD_DIM))
        o = (g0 * oc_ref[0, :, cols].astype(F32) + g1 * os_ref[0, :, cols].astype(F32)
             + g2 * ow_ref[0, :, cols].astype(F32))
        z = za_ref[0, hd].astype(F32)
        return (o * (z * jax.nn.sigmoid(z))).astype(BF16)

    def diff_piece(sl):
        z = zb_ref[0, sl].astype(F32)
        return (ob_ref[0, :, sl * LANE:(sl + 1) * LANE].astype(F32) * (z * jax.nn.sigmoid(z))).astype(BF16)

    pieces = [functools.partial(nsa_piece, hd) for hd in range(NSA_HEADS)]
    pieces += [functools.partial(diff_piece, sl) for sl in range(DIFF_WIDTH // LANE)]
    y = x_ref[0]
    for kc in range(0, len(pieces), OUT_K_SLOTS):
        chunk = jnp.concatenate([f() for f in pieces[kc:kc + OUT_K_SLOTS]], axis=1)
        y = y + jnp.dot(chunk, w_ref[kc * LANE:(kc + OUT_K_SLOTS) * LANE, :], preferred_element_type=F32)
    y = y * lax.rsqrt(jnp.mean(y * y, axis=-1, keepdims=True) + NORM_EPS)
    o_ref[0] = y * fw_ref[...]


def _out_proj(x, o_cmp, o_sel, o_win, gl, p, o_b, w_out, final_w, batch, t_len, tm):
    row = lambda b, i: (b, i, 0)
    return pl.pallas_call(
        _out_proj_kernel,
        grid=(batch, t_len // tm),
        in_specs=[
            pl.BlockSpec((1, tm, D_MODEL), row),
            pl.BlockSpec((1, tm, NSA_WIDTH), row),
            pl.BlockSpec((1, tm, NSA_WIDTH), row),
            pl.BlockSpec((1, tm, NSA_WIDTH), row),
            pl.BlockSpec((1, tm, LANE), row),
            pl.BlockSpec((1, NSA_HEADS, tm, LANE), lambda b, i: (b, SLOT_ZA // NSA_HEADS, i, 0)),
            pl.BlockSpec((1, tm, DIFF_WIDTH), row),
            pl.BlockSpec((1, DIFF_WIDTH // LANE, tm, LANE), lambda b, i: (b, SLOT_ZB // (DIFF_WIDTH // LANE), i, 0)),
            _resident((MIX_WIDTH, D_MODEL), lambda b, i: (0, 0)),
            pl.BlockSpec((1, D_MODEL), lambda b, i: (0, 0)),
        ],
        out_specs=pl.BlockSpec((1, tm, D_MODEL), row),
        out_shape=jax.ShapeDtypeStruct((batch, t_len, D_MODEL), F32),
        compiler_params=_cparams(("parallel", "parallel")),
        name="out_proj",
    )(x, o_cmp, o_sel, o_win, gl, p, o_b, p, w_out, final_w)


def _reorder_w_in(w, gain):
    w = w * gain[:, None]

    def section(off, n_slots):
        return w[:, off:off + n_slots * LANE]
    cols = [section(_O_QA, 8), section(_O_QB, 8), section(_O_KB, 8), section(_O_KC, 2), section(_O_KS, 2),
            section(_O_KW, 2), section(_O_VC, 2), section(_O_ZA, 8), section(_O_ZB, 8), section(_O_VB, 8),
            section(_O_VS, 2), section(_O_VW, 2)]
    w_r = jnp.concatenate(cols, axis=1).astype(BF16)
    assert w_r.shape[1] == N_SLOTS * LANE
    w_g = jnp.pad(w[:, _O_G:_O_G + NSA_HEADS * NSA_BRANCHES],
                  ((0, 0), (0, LANE - NSA_HEADS * NSA_BRANCHES))).astype(BF16)
    return w_r, w_g


def _sel_map(t_len, n_selp):
    n_rows = t_len // CMP_STRIDE
    c0 = np.arange(n_rows)[:, None] * CMP_STRIDE
    s0 = np.arange(n_selp)[None, :] * SEL_BLOCK
    ov = np.minimum(c0 + CMP_BLOCK, s0 + SEL_BLOCK) - np.maximum(c0, s0)
    m = np.maximum(ov, 0) / CMP_BLOCK
    m[n_rows - 1:] = 0.0
    m[:, t_len // SEL_BLOCK:] = 0.0
    return jnp.asarray(m.T, dtype=BF16)


def _block_one_hot(t_len):
    blk = (np.arange(t_len) // SEL_BLOCK) % LANE
    return jnp.asarray(blk[:, None] == np.arange(LANE)[None, :], dtype=BF16)


def _rope_tables(t_len):
    inv = ROPE_THETA ** (-np.arange(0, HEAD_DIM, 2, dtype=np.float64) / HEAD_DIM)
    ang = np.arange(t_len, dtype=np.float64)[:, None] * inv[None, :]
    cos, sin = np.cos(ang), np.sin(ang)
    return (jnp.asarray(np.concatenate([cos, cos], axis=1), dtype=F32),
            jnp.asarray(np.concatenate([-sin, sin], axis=1), dtype=F32))


class _Tiles(NamedTuple):
    in_tm: int
    out_tm: int
    cmp_tq: int
    win_tq: int
    attn_tq: int
    diag_rb: int
    sel_tk: int
    sel_rl: int
    diff_tk: int
    diff_rl: int


def _tiles(t_len):
    return _Tiles(in_tm=min(1024, t_len), out_tm=min(512, t_len), cmp_tq=min(512, t_len), win_tq=min(2048, t_len),
                  attn_tq=min(2048, t_len), diag_rb=256, sel_tk=min(2048, t_len), sel_rl=512, diff_tk=min(2048, t_len), diff_rl=128)


def kernel(x, norm_w, w_in, cmp_pos, cmp_k_w1, cmp_k_w2, cmp_v_w1, cmp_v_w2, lam_q1, lam_k1, lam_q2, lam_k2,
           subln_w, w_out, final_norm_w):
    batch, t_len, _ = x.shape
    n_rows = t_len // CMP_STRIDE
    n_selp = -(-(t_len // SEL_BLOCK) // LANE) * LANE

    tiles = _tiles(t_len)
    cos_f, sin_f = _rope_tables(t_len)
    w_r, w_g = _reorder_w_in(w_in[0], norm_w[0])
    p, gl, pc = _in_proj(x.reshape(batch * t_len, D_MODEL), w_r, w_g, cos_f, sin_f, batch, t_len, tiles.in_tm)

    half = CMP_STRIDE * HEAD_DIM
    w1cat = jnp.stack([jnp.concatenate([w[:half], w[half:]], axis=1) for w in (cmp_k_w1[0], cmp_v_w1[0])]).astype(BF16)
    w2 = jnp.stack([cmp_k_w2[0], cmp_v_w2[0]]).astype(BF16)
    pos2 = jnp.broadcast_to(cmp_pos[0].reshape(2, 1, half), (2, SUBLANE, half)).astype(BF16)
    kvc = _compress(pc, w1cat, pos2, w2, batch, n_rows)

    o_cmp, bias = _cmp_select(p, kvc, _sel_map(t_len, n_selp), batch, t_len, tq=tiles.cmp_tq)
    o_sel = _sel_attn(p, bias, _block_one_hot(t_len), batch, t_len,
                      tq=tiles.attn_tq, tk=tiles.sel_tk, rb=tiles.diag_rb, rl=tiles.sel_rl)
    o_win = _win_attn(p, batch, t_len, tq=tiles.win_tq)

    lam4 = jnp.pad(jnp.stack([lam_q1[0], lam_k1[0], lam_q2[0], lam_k2[0]]), ((0, SUBLANE - 4), (0, 0)))
    o_b = _diff_attn(p, lam4, subln_w[0][None, :], batch, t_len,
                     tq=tiles.attn_tq, tk=tiles.diff_tk, rb=tiles.diag_rb, rl=tiles.diff_rl)

    return _out_proj(x, o_cmp, o_sel, o_win, gl, p, o_b, w_out[0].astype(BF16), final_norm_w[None, :],
                     batch, t_len, tm=tiles.out_tm)
```
